```python
import math
import jax, jax.numpy as jnp
from jax import lax
import numpy as np

D_MODEL = 2048
BATCH = 1
SEQ = 8192
DEPTH = 4

N_MIXERS = 3
N_A = (DEPTH + 2) // 3
N_B = (DEPTH + 1) // 3
N_C = DEPTH // 3
HEAD_DIM = 128
ROPE_THETA = 500000.0
PARTIAL_ROT = HEAD_DIM // 4
NORM_EPS = 1e-6
NEG = -1e30
A_HEADS = D_MODEL // HEAD_DIM
A_KV_HEADS = A_HEADS // 4
A_HALF_WINDOW = 128
B_HEADS = D_MODEL // HEAD_DIM
B_Q_RANK = 512
B_KV_RANK = 512
B_NOPE = 128
B_ROPE = 64
B_V = 128
B_QK = B_NOPE + B_ROPE
C_PATTERNS = ((128, 1), (512, 4), (2048, 16))
C_GROUPS = len(C_PATTERNS)
C_HEADS = D_MODEL // HEAD_DIM
D_FF = ((8 * D_MODEL // 3 + 255) // 256) * 256
PLE_DIM = 256
Q_BLOCK = 128

kernel_name = "interleaved_hybrid_encoder_swa_mla_dilated"


def rms_norm(x, g):
    xf = x.astype(jnp.float32)
    y = xf * lax.rsqrt(jnp.mean(xf * xf, axis=-1, keepdims=True) + NORM_EPS)
    return (y * g.astype(jnp.float32)).astype(x.dtype)


def rope(x, pos, rot_dim):
    half = rot_dim // 2
    inv = ROPE_THETA ** (-jnp.arange(half, dtype=jnp.float32) * 2.0 / rot_dim)
    ang = pos.astype(jnp.float32)[..., None] * inv
    cos = jnp.cos(ang)[:, :, None, :]
    sin = jnp.sin(ang)[:, :, None, :]
    xr = x[..., :rot_dim].astype(jnp.float32)
    x1, x2 = xr[..., :half], xr[..., half:]
    rot = jnp.concatenate([x1 * cos - x2 * sin, x2 * cos + x1 * sin], axis=-1).astype(x.dtype)
    return jnp.concatenate([rot, x[..., rot_dim:]], axis=-1)


def banded_attention(q, k, v, half_w, sink=None):
    N, L, Hq, hd = q.shape
    Hkv = k.shape[2]
    G = Hq // Hkv
    blk = half_w
    nb = -(-L // blk)
    Lp = nb * blk
    pad = Lp - L
    q = jnp.pad(q, ((0, 0), (0, pad), (0, 0), (0, 0)))
    k = jnp.pad(k, ((0, 0), (blk, pad + blk), (0, 0), (0, 0)))
    v = jnp.pad(v, ((0, 0), (blk, pad + blk), (0, 0), (0, 0)))
    kb = k.reshape(N, nb + 2, blk, Hkv, hd)
    vb = v.reshape(N, nb + 2, blk, Hkv, hd)
    kn = jnp.concatenate([kb[:, :-2], kb[:, 1:-1], kb[:, 2:]], axis=2)
    vn = jnp.concatenate([vb[:, :-2], vb[:, 1:-1], vb[:, 2:]], axis=2)
    qb = q.reshape(N, nb, blk, Hkv, G, hd)
    s = jnp.einsum('nbqkgd,nbjkd->nbkgqj', qb, kn,
                   preferred_element_type=jnp.float32) * (1.0 / math.sqrt(hd))
    qpos = jnp.arange(Lp).reshape(nb, blk)
    kpos = (jnp.arange(nb)[:, None] - 1) * blk + jnp.arange(3 * blk)[None, :]
    valid = ((jnp.abs(qpos[:, :, None] - kpos[:, None, :]) <= half_w)
             & (kpos >= 0)[:, None, :] & (kpos < L)[:, None, :])
    s = jnp.where(valid[None, :, None, None], s, NEG)
    m = jnp.max(s, axis=-1)
    if sink is not None:
        sink_b = sink.astype(jnp.float32).reshape(1, 1, Hkv, G, 1)
        m = jnp.maximum(m, sink_b)
    pr = jnp.exp(s - m[..., None])
    denom = jnp.sum(pr, axis=-1)
    if sink is not None:
        denom = denom + jnp.exp(sink_b - m)
    o = jnp.einsum('nbkgqj,nbjkd->nbqkgd', pr.astype(v.dtype), vn,
                   preferred_element_type=jnp.float32)
    o = o / denom.transpose(0, 1, 4, 2, 3)[..., None]
    lse = (m + jnp.log(denom)).transpose(0, 1, 4, 2, 3).reshape(N, Lp, Hq)[:, :L]
    o = o.reshape(N, Lp, Hq, hd)[:, :L].astype(q.dtype)
    return o, lse


def dense_attention(q, k, v):
    B, S, H, dq = q.shape
    dv = v.shape[-1]
    nq = S // Q_BLOCK
    scale = 1.0 / math.sqrt(dq)
    qb = q.reshape(B, nq, Q_BLOCK, H, dq).transpose(1, 0, 2, 3, 4)

    def one_block(qi):
        s = jnp.einsum('bqhd,bkhd->bhqk', qi, k, preferred_element_type=jnp.float32) * scale
        pr = jax.nn.softmax(s, axis=-1)
        return jnp.einsum('bhqk,bkhd->bqhd', pr.astype(v.dtype), v)

    o = lax.map(one_block, qb)
    return o.transpose(1, 0, 2, 3, 4).reshape(B, S, H, dv)


def mixer_a(hn, pos, w_in, gq, gk, sink, w_o):
    B, S, _ = hn.shape
    qkv = hn @ w_in
    nq = A_HEADS * HEAD_DIM
    nk = A_KV_HEADS * HEAD_DIM
    q = qkv[..., :nq].reshape(B, S, A_HEADS, HEAD_DIM)
    k = qkv[..., nq:nq + nk].reshape(B, S, A_KV_HEADS, HEAD_DIM)
    v = qkv[..., nq + nk:].reshape(B, S, A_KV_HEADS, HEAD_DIM)
    q = rope(rms_norm(q, gq), pos, PARTIAL_ROT)
    k = rope(rms_norm(k, gk), pos, PARTIAL_ROT)
    o, _ = banded_attention(q, k, v, A_HALF_WINDOW, sink)
    return o.reshape(B, S, nq) @ w_o


def mixer_b(hn, pos, w_in, g_qlat, g_kvlat, w_q_up, w_kv_up, gq, gk, w_o):
    B, S, _ = hn.shape
    lat = hn @ w_in
    q_lat = lat[..., :B_Q_RANK]
    kv_lat = lat[..., B_Q_RANK:B_Q_RANK + B_KV_RANK]
    k_rope = lat[..., B_Q_RANK + B_KV_RANK:]
    q = (rms_norm(q_lat, g_qlat) @ w_q_up).reshape(B, S, B_HEADS, B_QK)
    kv = (rms_norm(kv_lat, g_kvlat) @ w_kv_up).reshape(B, S, B_HEADS, B_NOPE + B_V)
    k_nope, v = kv[..., :B_NOPE], kv[..., B_NOPE:]
    k = jnp.concatenate(
        [k_nope, jnp.broadcast_to(k_rope[:, :, None, :], (B, S, B_HEADS, B_ROPE))], axis=-1)
    q = rms_norm(q, gq)
    k = rms_norm(k, gk)
    q = jnp.concatenate([q[..., :B_NOPE], rope(q[..., B_NOPE:], pos, B_ROPE)], axis=-1)
    k = jnp.concatenate([k[..., :B_NOPE], rope(k[..., B_NOPE:], pos, B_ROPE)], axis=-1)
    o = dense_attention(q, k, v)
    return o.reshape(B, S, B_HEADS * B_V) @ w_o


def to_chains(x, dil):
    B, S, H, d = x.shape
    return x.reshape(B, S // dil, dil, H, d).transpose(0, 2, 1, 3, 4).reshape(B * dil, S // dil, H, d)


def from_chains(x, B, dil):
    N, Lc = x.shape[0], x.shape[1]
    rest = x.shape[2:]
    x = x.reshape((B, dil, Lc) + rest)
    x = jnp.moveaxis(x, 1, 2)
    return x.reshape((B, Lc * dil) + rest)


def mixer_c(hn, pos, w_in, gq, gk, w_o):
    B, S, _ = hn.shape
    nq = C_GROUPS * C_HEADS * HEAD_DIM
    nkv = C_HEADS * HEAD_DIM
    qkv = hn @ w_in
    q = qkv[..., :nq].reshape(B, S, C_GROUPS * C_HEADS, HEAD_DIM)
    k = qkv[..., nq:nq + nkv].reshape(B, S, C_HEADS, HEAD_DIM)
    v = qkv[..., nq + nkv:].reshape(B, S, C_HEADS, HEAD_DIM)
    q = rope(rms_norm(q, gq), pos, PARTIAL_ROT)
    k = rope(rms_norm(k, gk), pos, PARTIAL_ROT)
    outs, lses = [], []
    for g, (window, dil) in enumerate(C_PATTERNS):
        qg = q[:, :, g * C_HEADS:(g + 1) * C_HEADS]
        half_steps = window // 2 // dil
        o, lse = banded_attention(to_chains(qg, dil), to_chains(k, dil), to_chains(v, dil), half_steps)
        outs.append(from_chains(o, B, dil))
        lses.append(from_chains(lse, B, dil))
    w = jax.nn.softmax(jnp.stack(lses, axis=0), axis=0)
    o = jnp.sum(w[..., None] * jnp.stack(outs, axis=0).astype(jnp.float32), axis=0).astype(hn.dtype)
    return o.reshape(B, S, nkv) @ w_o


def swiglu(hn, wg, wu, wd):
    return (jax.nn.silu(hn @ wg) * (hn @ wu)) @ wd


def setup_inputs(seed: int = 0) -> dict:
    key = jax.random.key(seed)
    ks = iter(jax.random.split(key, 40))
    f32 = jnp.float32

    def nrm(shape, fan_in):
        return jax.random.normal(next(ks), shape, f32) * (fan_in ** -0.5)

    def gain(shape):
        return 1.0 + 0.05 * jax.random.normal(next(ks), shape, f32)

    x = jax.random.normal(next(ks), (BATCH, SEQ, D_MODEL), f32)
    p = jax.random.normal(next(ks), (DEPTH, BATCH, SEQ, PLE_DIM), f32)
    offs = jax.random.randint(next(ks), (BATCH, 1), 0, 4096, dtype=jnp.int32)
    positions = jnp.arange(SEQ, dtype=jnp.int32)[None, :] + offs
    a_in = A_HEADS * HEAD_DIM + 2 * A_KV_HEADS * HEAD_DIM
    b_in = B_Q_RANK + B_KV_RANK + B_ROPE
    c_in = C_GROUPS * C_HEADS * HEAD_DIM + 2 * C_HEADS * HEAD_DIM
    return {
        "x": x,
        "p": p,
        "positions": positions,
        "g_mix": gain((DEPTH, D_MODEL)),
        "g_ffn": gain((DEPTH, D_MODEL)),
        "g_ple": gain((DEPTH, D_MODEL)),
        "w_ple_gate": nrm((DEPTH, D_MODEL, D_MODEL), D_MODEL),
        "w_ple_proj": nrm((DEPTH, PLE_DIM, D_MODEL), PLE_DIM),
        "w_ffn_gate": nrm((DEPTH, D_MODEL, D_FF), D_MODEL),
        "w_ffn_up": nrm((DEPTH, D_MODEL, D_FF), D_MODEL),
        "w_ffn_down": nrm((DEPTH, D_FF, D_MODEL), D_FF),
        "a_w_in": nrm((N_A, D_MODEL, a_in), D_MODEL),
        "a_q_norm": gain((N_A, HEAD_DIM)),
        "a_k_norm": gain((N_A, HEAD_DIM)),
        "a_sink": 0.5 * jax.random.normal(next(ks), (N_A, A_HEADS), f32),
        "a_w_o": nrm((N_A, A_HEADS * HEAD_DIM, D_MODEL), A_HEADS * HEAD_DIM),
        "b_w_in": nrm((N_B, D_MODEL, b_in), D_MODEL),
        "b_q_lat_norm": gain((N_B, B_Q_RANK)),
        "b_kv_lat_norm": gain((N_B, B_KV_RANK)),
        "b_w_q_up": nrm((N_B, B_Q_RANK, B_HEADS * B_QK), B_Q_RANK),
        "b_w_kv_up": nrm((N_B, B_KV_RANK, B_HEADS * (B_NOPE + B_V)), B_KV_RANK),
        "b_q_norm": gain((N_B, B_QK)),
        "b_k_norm": gain((N_B, B_QK)),
        "b_w_o": nrm((N_B, B_HEADS * B_V, D_MODEL), B_HEADS * B_V),
        "c_w_in": nrm((N_C, D_MODEL, c_in), D_MODEL),
        "c_q_norm": gain((N_C, HEAD_DIM)),
        "c_k_norm": gain((N_C, HEAD_DIM)),
        "c_w_o": nrm((N_C, C_HEADS * HEAD_DIM, D_MODEL), C_HEADS * HEAD_DIM),
    }


def reference(x, p, positions, g_mix, g_ffn, g_ple, w_ple_gate, w_ple_proj,
              w_ffn_gate, w_ffn_up, w_ffn_down,
              a_w_in, a_q_norm, a_k_norm, a_sink, a_w_o,
              b_w_in, b_q_lat_norm, b_kv_lat_norm, b_w_q_up, b_w_kv_up, b_q_norm, b_k_norm, b_w_o,
              c_w_in, c_q_norm, c_k_norm, c_w_o):
    h = x
    for i in range(DEPTH):
        kind, slot = i % N_MIXERS, i // N_MIXERS
        hn = rms_norm(h, g_mix[i])
        if kind == 0:
            mix = mixer_a(hn, positions, a_w_in[slot], a_q_norm[slot], a_k_norm[slot],
                          a_sink[slot], a_w_o[slot])
        elif kind == 1:
            mix = mixer_b(hn, positions, b_w_in[slot], b_q_lat_norm[slot], b_kv_lat_norm[slot],
                          b_w_q_up[slot], b_w_kv_up[slot], b_q_norm[slot], b_k_norm[slot],
                          b_w_o[slot])
        else:
            mix = mixer_c(hn, positions, c_w_in[slot], c_q_norm[slot], c_k_norm[slot], c_w_o[slot])
        h = h + mix
        h = h + swiglu(rms_norm(h, g_ffn[i]), w_ffn_gate[i], w_ffn_up[i], w_ffn_down[i])
        gate = jax.nn.sigmoid(rms_norm(h, g_ple[i]) @ w_ple_gate[i])
        h = h + gate * (p[i] @ w_ple_proj[i])
    return h
```

```python
import functools
import math

import jax
import jax.numpy as jnp
from jax import lax
from jax.experimental import pallas as pl
from jax.experimental.pallas import tpu as pltpu

F32 = jnp.float32
BF16 = jnp.bfloat16

SEQ = 8192
D_MODEL = 2048
DEPTH = 4
HEAD_DIM = 128
ROPE_THETA = 500000.0
PARTIAL_ROT = HEAD_DIM // 4
NORM_EPS = 1e-6
NEG = -1e30
LANES = 128

A_HEADS = 16
A_KV_HEADS = 4
A_HALF_WINDOW = 128
B_HEADS = 16
B_Q_RANK = 512
B_KV_RANK = 512
B_NOPE = 128
B_ROPE = 64
B_QK = B_NOPE + B_ROPE
B_HEAD_PAD = 256
B_IN_PAD = 1152
C_PATTERNS = ((128, 1), (512, 4), (2048, 16))
C_GROUPS = 3
C_HEADS = 16
D_FF = 5632
PLE_DIM = 256

VMEM_LIMIT = 56 * 1024 * 1024


def _params(*sem):
    return pltpu.CompilerParams(dimension_semantics=sem, vmem_limit_bytes=VMEM_LIMIT)


def _rms_scale(x, width):
    ss = jnp.sum(x * x, axis=-1, keepdims=True)
    return x * lax.rsqrt(ss * (1.0 / width) + NORM_EPS)


def _rope_lanes(y, cos, sin_lo, sin_hi, half):
    return (y * cos + pltpu.roll(y, LANES - half, 1) * sin_lo
            + pltpu.roll(y, half, 1) * sin_hi)


def _rope_table_kernel(pos_ref, inv_ref, lo_ref, hi_ref, cos_ref, slo_ref, shi_ref):
    ang = pos_ref[...].astype(F32) * inv_ref[...]
    s = jnp.sin(ang)
    cos_ref[...] = jnp.cos(ang)
    slo_ref[...] = s * lo_ref[...]
    shi_ref[...] = s * hi_ref[...]


def _rope_tables(pos_col, rot_dim):
    half = rot_dim // 2
    inv = ROPE_THETA ** (-jnp.arange(half, dtype=F32) * 2.0 / rot_dim)
    zeros = jnp.zeros((LANES - rot_dim,), F32)
    inv_l = jnp.concatenate([inv, inv, zeros])[None, :]
    lo = jnp.concatenate([-jnp.ones((half,), F32), jnp.zeros((half,), F32), zeros])[None, :]
    hi = jnp.concatenate([jnp.zeros((half,), F32), jnp.ones((half,), F32), zeros])[None, :]
    tm = 1024
    row = pl.BlockSpec((1, LANES), lambda i: (0, 0))
    tab = pl.BlockSpec((tm, LANES), lambda i: (i, 0))
    return pl.pallas_call(
        _rope_table_kernel,
        grid=(SEQ // tm,),
        in_specs=[pl.BlockSpec((tm, 1), lambda i: (i, 0)), row, row, row],
        out_specs=[tab, tab, tab],
        out_shape=[jax.ShapeDtypeStruct((SEQ, LANES), F32)] * 3,
        compiler_params=_params("parallel"),
        name="rope_tables",
    )(pos_col, inv_l, lo, hi)


def _norm_matmul_kernel(x_ref, g_ref, w_ref, *rest, n_extra, epilogue):
    extra = rest[:n_extra]
    outs = rest[n_extra:-1]
    xn_ref = rest[-1]
    j = pl.program_id(1)

    @pl.when(j == 0)
    def _():
        x = x_ref[...]
        xn_ref[...] = (_rms_scale(x, x.shape[-1]) * g_ref[...]).astype(BF16)

    acc = jnp.dot(xn_ref[...], w_ref[...], preferred_element_type=F32)
    epilogue(acc, j, extra, outs)


def _norm_matmul(x, x_col, kin, gain, w, *, tm, tn, extra, extra_specs, epilogue,
                 out_shape, out_specs, name):
    m = x.shape[0]
    n = w.shape[1]
    kernel = functools.partial(_norm_matmul_kernel, n_extra=len(extra), epilogue=epilogue)
    return pl.pallas_call(
        kernel,
        grid=(m // tm, n // tn),
        in_specs=[pl.BlockSpec((tm, kin), lambda i, j: (i, x_col)),
                  pl.BlockSpec((1, kin), lambda i, j: (0, 0)),
                  pl.BlockSpec((kin, tn), lambda i, j: (0, j))] + list(extra_specs),
        out_specs=out_specs,
        out_shape=out_shape,
        scratch_shapes=[pltpu.VMEM((tm, kin), BF16)],
        compiler_params=_params("parallel", "arbitrary"),
        name=name,
    )(x, gain, w, *extra)


def _qkv_epilogue(acc, j, extra, outs, *, n_norm_tiles, tn):
    gain_ref, cos_ref, slo_ref, shi_ref = extra
    (o_ref,) = outs

    @pl.when(j < n_norm_tiles)
    def _():
        cos, slo, shi = cos_ref[...], slo_ref[...], shi_ref[...]
        for c in range(tn // HEAD_DIM):
            cols = slice(c * HEAD_DIM, (c + 1) * HEAD_DIM)
            y = _rms_scale(acc[:, cols], HEAD_DIM) * gain_ref[:, cols]
            o_ref[:, cols] = _rope_lanes(y, cos, slo, shi, PARTIAL_ROT // 2).astype(BF16)

    @pl.when(j >= n_norm_tiles)
    def _():
        o_ref[...] = acc.astype(BF16)


def _qkv_projection(h, g_mix, w_in, head_gain, n_norm_cols, tables, name):
    tm, tn = 1024, 512
    n = w_in.shape[1]
    tab = pl.BlockSpec((tm, LANES), lambda i, j: (i, 0))
    return _norm_matmul(
        h, 0, D_MODEL, g_mix, w_in, tm=tm, tn=tn,
        extra=(head_gain,) + tuple(tables),
        extra_specs=[pl.BlockSpec((1, tn), lambda i, j: (0, j)), tab, tab, tab],
        epilogue=functools.partial(_qkv_epilogue, n_norm_tiles=n_norm_cols // tn, tn=tn),
        out_shape=jax.ShapeDtypeStruct((SEQ, n), BF16),
        out_specs=pl.BlockSpec((tm, tn), lambda i, j: (i, j)),
        name=name)


def _plain_epilogue(acc, j, extra, outs):
    outs[0][...] = acc.astype(outs[0].dtype)


def _b_q_epilogue(acc, j, extra, outs, *, tn):
    gain_ref, cos_ref, slo_ref, shi_ref = extra
    (o_ref,) = outs
    cos, slo, shi = cos_ref[...], slo_ref[...], shi_ref[...]
    for c in range(tn // B_HEAD_PAD):
        c0 = c * B_HEAD_PAD
        y = _rms_scale(acc[:, c0:c0 + B_HEAD_PAD], B_QK) * gain_ref[:, c0:c0 + B_HEAD_PAD]
        o_ref[:, c0:c0 + B_NOPE] = y[:, :B_NOPE].astype(BF16)
        o_ref[:, c0 + B_NOPE:c0 + B_HEAD_PAD] = _rope_lanes(
            y[:, B_NOPE:], cos, slo, shi, B_ROPE // 2).astype(BF16)


def _b_k_epilogue(acc, j, extra, outs, *, tn):
    gain_ref, krope_ref, cos_ref, slo_ref, shi_ref = extra
    (o_ref,) = outs
    cos, slo, shi = cos_ref[...], slo_ref[...], shi_ref[...]
    kr = krope_ref[...]
    kr_ss = jnp.sum(kr * kr, axis=-1, keepdims=True)
    g_nope = gain_ref[:, :B_NOPE]
    g_rope = gain_ref[:, B_NOPE:]
    for c in range(tn // B_NOPE):
        y = acc[:, c * B_NOPE:(c + 1) * B_NOPE]
        ss = jnp.sum(y * y, axis=-1, keepdims=True) + kr_ss
        rinv = lax.rsqrt(ss * (1.0 / B_QK) + NORM_EPS)
        c0 = c * B_HEAD_PAD
        o_ref[:, c0:c0 + B_NOPE] = (y * rinv * g_nope).astype(BF16)
        o_ref[:, c0 + B_NOPE:c0 + B_HEAD_PAD] = _rope_lanes(
            kr * rinv * g_rope, cos, slo, shi, B_ROPE // 2).astype(BF16)


def _b_v_epilogue(acc, j, extra, outs, *, tn):
    (o_ref,) = outs
    tm = acc.shape[0]
    ones_col = (lax.broadcasted_iota(jnp.int32, (tm, LANES), 1) == 0).astype(BF16)
    for c in range(tn // B_NOPE):
        c0 = c * B_HEAD_PAD
        o_ref[:, c0:c0 + B_NOPE] = acc[:, c * B_NOPE:(c + 1) * B_NOPE].astype(BF16)
        o_ref[:, c0 + B_NOPE:c0 + B_HEAD_PAD] = ones_col


def _banded_kernel(*refs, tq, hw, n_kv, group, length, q_axis, has_sink, has_lse):
    q_ref, kp_ref, kc_ref, kn_ref, vp_ref, vc_ref, vn_ref = refs[:7]
    nxt = 7
    sink_ref = None
    if has_sink:
        sink_ref = refs[nxt]
        nxt += 1
    o_ref = refs[nxt]
    lse_ref = refs[nxt + 1] if has_lse else None

    i = pl.program_id(q_axis)
    win = tq + 2 * hw
    row = lax.broadcasted_iota(jnp.int32, (group * tq, win), 0) & (tq - 1)
    col = lax.broadcasted_iota(jnp.int32, (group * tq, win), 1)
    kpos = i * tq - hw + col
    valid = (jnp.abs(row + hw - col) <= hw) & (kpos >= 0) & (kpos < length)
    lane = lax.broadcasted_iota(jnp.int32, (tq, LANES), 1)
    lse_tile = jnp.zeros((tq, LANES), F32)

    for kv in range(n_kv):
        kcols = slice(kv * HEAD_DIM, (kv + 1) * HEAD_DIM)
        k_win = jnp.concatenate(
            [kp_ref[tq - hw:, kcols], kc_ref[:, kcols], kn_ref[:hw, kcols]], axis=0)
        v_win = jnp.concatenate(
            [vp_ref[tq - hw:, kcols], vc_ref[:, kcols], vn_ref[:hw, kcols]], axis=0)
        heads = [kv * group + c for c in range(group)]
        q = jnp.concatenate(
            [q_ref[:, hd * HEAD_DIM:(hd + 1) * HEAD_DIM] for hd in heads], axis=0)
        s = lax.dot_general(q, k_win, (((1,), (1,)), ((), ())), preferred_element_type=F32)
        s = jnp.where(valid, s, NEG)
        m = jnp.max(s, axis=-1, keepdims=True)
        if has_sink:
            sink = sink_ref[kv * group * tq:(kv + 1) * group * tq, :]
            m = jnp.maximum(m, sink)
        p = jnp.exp(s - m)
        denom = jnp.sum(p, axis=-1, keepdims=True)
        if has_sink:
            denom = denom + jnp.exp(sink - m)
        o = jnp.dot(p.astype(BF16), v_win, preferred_element_type=F32) / denom
        for c, hd in enumerate(heads):
            o_ref[:, hd * HEAD_DIM:(hd + 1) * HEAD_DIM] = o[c * tq:(c + 1) * tq].astype(o_ref.dtype)
        if has_lse:
            lse = m + jnp.log(denom)
            lse_tile = jnp.where(lane == kv, lse, lse_tile)
    if has_lse:
        lse_ref[...] = lse_tile


def _banded_attention(qkv, *, dil, row_width, q_col, k_col, v_col, n_q, n_kv, hw, sink_col, has_lse, name):
    tq = 128
    length = SEQ // dil
    nb = length // tq
    group = n_q // n_kv
    qw, kw = n_q * HEAD_DIM, n_kv * HEAD_DIM
    view = qkv.reshape(length, dil * row_width)
    q_per_row, k_per_row = row_width // qw, row_width // kw

    def kv_spec(col, shift):
        return pl.BlockSpec(
            (tq, kw),
            lambda r, i: (jnp.clip(i + shift, 0, nb - 1), r * k_per_row + col))

    in_specs = [pl.BlockSpec((tq, qw), lambda r, i: (i, r * q_per_row + q_col)),
                kv_spec(k_col, -1), kv_spec(k_col, 0), kv_spec(k_col, 1),
                kv_spec(v_col, -1), kv_spec(v_col, 0), kv_spec(v_col, 1)]
    args = [view] * 7
    if sink_col is not None:
        in_specs.append(pl.BlockSpec(sink_col.shape, lambda r, i: (0, 0)))
        args.append(sink_col)
    out_shape = [jax.ShapeDtypeStruct((length, dil * qw), BF16)]
    out_specs = [pl.BlockSpec((tq, qw), lambda r, i: (i, r))]
    if has_lse:
        out_shape.append(jax.ShapeDtypeStruct((length, dil * LANES), F32))
        out_specs.append(pl.BlockSpec((tq, LANES), lambda r, i: (i, r)))
    kernel = functools.partial(
        _banded_kernel, tq=tq, hw=hw, n_kv=n_kv, group=group, length=length, q_axis=1,
        has_sink=sink_col is not None, has_lse=has_lse)
    outs = pl.pallas_call(
        kernel, grid=(dil, nb), in_specs=in_specs, out_specs=out_specs, out_shape=out_shape,
        compiler_params=_params("parallel", "parallel"), name=name,
    )(*args)
    o = outs[0].reshape(SEQ, qw)
    if has_lse:
        return o, outs[1].reshape(SEQ, LANES)
    return o


def _flash_kernel(q_ref, k_ref, v_ref, o_ref, m_ref, acc_ref, *, tk):
    tq = q_ref.shape[0]
    n_chunks = k_ref.shape[0] // tk
    m_ref[...] = jnp.full((tq, LANES), NEG, F32)
    acc_ref[...] = jnp.zeros((tq, B_HEAD_PAD), F32)
    q = q_ref[...]

    def body(c, carry):
        start = pl.multiple_of(c * tk, tk)
        k = k_ref[pl.ds(start, tk), :]
        v = v_ref[pl.ds(start, tk), :]
        s = lax.dot_general(q, k, (((1,), (1,)), ((), ())), preferred_element_type=F32)
        m_old = m_ref[...]
        m_new = jnp.maximum(m_old, jnp.max(s, axis=-1, keepdims=True))
        alpha = jnp.exp(m_old - m_new)
        p = jnp.exp(s - m_new[:, :1]).astype(BF16)
        pv = jnp.dot(p, v, preferred_element_type=F32)
        acc_ref[...] = acc_ref[...] * jnp.concatenate([alpha, alpha], axis=1) + pv
        m_ref[...] = m_new
        return carry

    lax.fori_loop(0, n_chunks, body, 0)
    acc = acc_ref[...]
    o_ref[...] = (acc[:, :B_NOPE] / acc[:, B_NOPE:B_NOPE + 1]).astype(o_ref.dtype)


def _dense_attention(q, k, v_ext):
    tq, tk = 512, 512
    kv_spec = pl.BlockSpec((SEQ, B_HEAD_PAD), lambda h, i: (0, h))
    return pl.pallas_call(
        functools.partial(_flash_kernel, tk=tk),
        grid=(B_HEADS, SEQ // tq),
        in_specs=[pl.BlockSpec((tq, B_HEAD_PAD), lambda h, i: (i, h)), kv_spec, kv_spec],
        out_specs=pl.BlockSpec((tq, B_NOPE), lambda h, i: (i, h)),
        out_shape=jax.ShapeDtypeStruct((SEQ, B_HEADS * B_NOPE), BF16),
        scratch_shapes=[pltpu.VMEM((tq, LANES), F32), pltpu.VMEM((tq, B_HEAD_PAD), F32)],
        compiler_params=_params("parallel", "arbitrary"),
        name="b_flash_attention",
    )(q, k, v_ext)


def _out_proj_kernel(o_ref, w_ref, h_ref, out_ref):
    out_ref[...] = h_ref[...] + jnp.dot(o_ref[...], w_ref[...], preferred_element_type=F32)


def _out_projection(o, w, h, name):
    tm, tn = 1024, 1024
    return pl.pallas_call(
        _out_proj_kernel,
        grid=(SEQ // tm, D_MODEL // tn),
        in_specs=[pl.BlockSpec((tm, D_MODEL), lambda i, j: (i, 0)),
                  pl.BlockSpec((D_MODEL, tn), lambda i, j: (0, j)),
                  pl.BlockSpec((tm, tn), lambda i, j: (i, j))],
        out_specs=pl.BlockSpec((tm, tn), lambda i, j: (i, j)),
        out_shape=jax.ShapeDtypeStruct((SEQ, D_MODEL), F32),
        compiler_params=_params("parallel", "parallel"),
        name=name,
    )(o, w, h)


def _merge_out_proj_kernel(o0_ref, o1_ref, o2_ref, l0_ref, l1_ref, l2_ref, w_ref, h_ref,
                           out_ref, merged_ref):
    j = pl.program_id(1)

    @pl.when(j == 0)
    def _():
        l0, l1, l2 = l0_ref[...], l1_ref[...], l2_ref[...]
        m = jnp.maximum(jnp.maximum(l0, l1), l2)
        e0, e1, e2 = jnp.exp(l0 - m), jnp.exp(l1 - m), jnp.exp(l2 - m)
        inv = 1.0 / (e0 + e1 + e2)
        w0, w1, w2 = e0 * inv, e1 * inv, e2 * inv
        for hd in range(C_HEADS):
            cols = slice(hd * HEAD_DIM, (hd + 1) * HEAD_DIM)
            merged = (w0[:, hd:hd + 1] * o0_ref[:, cols].astype(F32)
                      + w1[:, hd:hd + 1] * o1_ref[:, cols].astype(F32)
                      + w2[:, hd:hd + 1] * o2_ref[:, cols].astype(F32))
            merged_ref[:, cols] = merged.astype(BF16)

    out_ref[...] = h_ref[...] + jnp.dot(merged_ref[...], w_ref[...], preferred_element_type=F32)


def _merge_out_projection(outs, lses, w, h):
    tm, tn = 512, 1024
    o_spec = pl.BlockSpec((tm, D_MODEL), lambda i, j: (i, 0))
    l_spec = pl.BlockSpec((tm, LANES), lambda i, j: (i, 0))
    return pl.pallas_call(
        _merge_out_proj_kernel,
        grid=(SEQ // tm, D_MODEL // tn),
        in_specs=[o_spec] * 3 + [l_spec] * 3 + [
            pl.BlockSpec((D_MODEL, tn), lambda i, j: (0, j)),
            pl.BlockSpec((tm, tn), lambda i, j: (i, j))],
        out_specs=pl.BlockSpec((tm, tn), lambda i, j: (i, j)),
        out_shape=jax.ShapeDtypeStruct((SEQ, D_MODEL), F32),
        scratch_shapes=[pltpu.VMEM((tm, D_MODEL), BF16)],
        compiler_params=_params("parallel", "arbitrary"),
        name="c_merge_out_proj",
    )(*outs, *lses, w, h)


def _ffn_kernel(x_ref, g_ref, wg_ref, wu_ref, wd_ref, o_ref, xn_ref):
    f = pl.program_id(1)

    @pl.when(f == 0)
    def _():
        x = x_ref[...]
        xn_ref[...] = (_rms_scale(x, D_MODEL) * g_ref[...]).astype(BF16)
        o_ref[...] = x

    xn = xn_ref[...]
    gate = jnp.dot(xn, wg_ref[...], preferred_element_type=F32)
    up = jnp.dot(xn, wu_ref[...], preferred_element_type=F32)
    act = (gate * jax.nn.sigmoid(gate) * up).astype(BF16)
    o_ref[...] += jnp.dot(act, wd_ref[...], preferred_element_type=F32)


def _ffn(h, g, wg, wu, wd):
    tm, tf = 512, 512
    return pl.pallas_call(
        _ffn_kernel,
        grid=(SEQ // tm, D_FF // tf),
        in_specs=[pl.BlockSpec((tm, D_MODEL), lambda i, f: (i, 0)),
                  pl.BlockSpec((1, D_MODEL), lambda i, f: (0, 0)),
                  pl.BlockSpec((D_MODEL, tf), lambda i, f: (0, f)),
                  pl.BlockSpec((D_MODEL, tf), lambda i, f: (0, f)),
                  pl.BlockSpec((tf, D_MODEL), lambda i, f: (f, 0))],
        out_specs=pl.BlockSpec((tm, D_MODEL), lambda i, f: (i, 0)),
        out_shape=jax.ShapeDtypeStruct((SEQ, D_MODEL), F32),
        scratch_shapes=[pltpu.VMEM((tm, D_MODEL), BF16)],
        compiler_params=_params("parallel", "arbitrary"),
        name="ffn_swiglu",
    )(h, g, wg, wu, wd)


def _ple_epilogue(acc, j, extra, outs):
    p_ref, wp_ref, h_ref = extra
    proj = jnp.dot(p_ref[...].astype(BF16), wp_ref[...], preferred_element_type=F32)
    outs[0][...] = h_ref[...] + jax.nn.sigmoid(acc) * proj


def _ple(h, g, w_gate, p, w_proj):
    tm, tn = 1024, 512
    return _norm_matmul(
        h, 0, D_MODEL, g, w_gate, tm=tm, tn=tn,
        extra=(p, w_proj, h),
        extra_specs=[pl.BlockSpec((tm, PLE_DIM), lambda i, j: (i, 0)),
                     pl.BlockSpec((PLE_DIM, tn), lambda i, j: (0, j)),
                     pl.BlockSpec((tm, tn), lambda i, j: (i, j))],
        epilogue=_ple_epilogue,
        out_shape=jax.ShapeDtypeStruct((SEQ, D_MODEL), F32),
        out_specs=pl.BlockSpec((tm, tn), lambda i, j: (i, j)),
        name="ple_gate")


def _mixer_a(h, g_mix, w_in, gq, gk, sink, w_o, tables):
    nq, nk = A_HEADS * HEAD_DIM, A_KV_HEADS * HEAD_DIM
    scale = 1.0 / math.sqrt(HEAD_DIM)
    head_gain = jnp.concatenate(
        [jnp.tile(gq * scale, A_HEADS), jnp.tile(gk, A_KV_HEADS), jnp.ones((nk,), F32)])[None, :]
    qkv = _qkv_projection(h, g_mix, w_in.astype(BF16), head_gain, nq + nk, tables, "a_qkv_proj")
    sink_col = jnp.repeat(sink, 128)[:, None]
    o = _banded_attention(
        qkv, dil=1, row_width=nq + 2 * nk, q_col=0, k_col=nq // nk, v_col=nq // nk + 1,
        n_q=A_HEADS, n_kv=A_KV_HEADS, hw=A_HALF_WINDOW, sink_col=sink_col, has_lse=False,
        name="a_banded_attention")
    return _out_projection(o, w_o.astype(BF16), h, "a_out_proj")


def _mixer_b(h, g_mix, w_in, g_qlat, g_kvlat, w_q_up, w_kv_up, gq, gk, w_o, tables):
    b_in = B_Q_RANK + B_KV_RANK + B_ROPE
    scale = 1.0 / math.sqrt(B_QK)
    w_in_pad = jnp.pad(w_in.astype(BF16), ((0, 0), (0, B_IN_PAD - b_in)))
    tm = 1024
    lat = _norm_matmul(
        h, 0, D_MODEL, g_mix, w_in_pad, tm=tm, tn=B_IN_PAD, extra=(), extra_specs=[],
        epilogue=_plain_epilogue,
        out_shape=jax.ShapeDtypeStruct((SEQ, B_IN_PAD), F32),
        out_specs=pl.BlockSpec((tm, B_IN_PAD), lambda i, j: (i, j)),
        name="b_latent_proj")

    tab = pl.BlockSpec((tm, LANES), lambda i, j: (i, 0))
    pad = B_HEAD_PAD - B_QK
    tn = 1024
    wq = jnp.pad(w_q_up.astype(BF16).reshape(B_Q_RANK, B_HEADS, B_QK),
                 ((0, 0), (0, 0), (0, pad))).reshape(B_Q_RANK, B_HEADS * B_HEAD_PAD)
    gq_pad = jnp.tile(jnp.pad(gq * scale, (0, pad)), B_HEADS)[None, :]
    q = _norm_matmul(
        lat, 0, B_Q_RANK, g_qlat, wq, tm=tm, tn=tn,
        extra=(gq_pad,) + tuple(tables),
        extra_specs=[pl.BlockSpec((1, tn), lambda i, j: (0, j)), tab, tab, tab],
        epilogue=functools.partial(_b_q_epilogue, tn=tn),
        out_shape=jax.ShapeDtypeStruct((SEQ, B_HEADS * B_HEAD_PAD), BF16),
        out_specs=pl.BlockSpec((tm, tn), lambda i, j: (i, j)),
        name="b_q_proj")

    wkv = w_kv_up.astype(BF16).reshape(B_KV_RANK, B_HEADS, 2 * B_NOPE)
    wk = wkv[:, :, :B_NOPE].reshape(B_KV_RANK, B_HEADS * B_NOPE)
    wv = wkv[:, :, B_NOPE:].reshape(B_KV_RANK, B_HEADS * B_NOPE)
    gk_pad = jnp.pad(gk, (0, pad))[None, :]
    k = _norm_matmul(
        lat, 1, B_KV_RANK, g_kvlat, wk, tm=tm, tn=tn,
        extra=(gk_pad, lat) + tuple(tables),
        extra_specs=[pl.BlockSpec((1, B_HEAD_PAD), lambda i, j: (0, 0)),
                     pl.BlockSpec((tm, LANES), lambda i, j: (i, (B_Q_RANK + B_KV_RANK) // LANES)),
                     tab, tab, tab],
        epilogue=functools.partial(_b_k_epilogue, tn=tn),
        out_shape=jax.ShapeDtypeStruct((SEQ, B_HEADS * B_HEAD_PAD), BF16),
        out_specs=pl.BlockSpec((tm, 2 * tn), lambda i, j: (i, j)),
        name="b_k_proj")
    v_ext = _norm_matmul(
        lat, 1, B_KV_RANK, g_kvlat, wv, tm=tm, tn=tn, extra=(), extra_specs=[],
        epilogue=functools.partial(_b_v_epilogue, tn=tn),
        out_shape=jax.ShapeDtypeStruct((SEQ, B_HEADS * B_HEAD_PAD), BF16),
        out_specs=pl.BlockSpec((tm, 2 * tn), lambda i, j: (i, j)),
        name="b_v_proj")
    o = _dense_attention(q, k, v_ext)
    return _out_projection(o, w_o.astype(BF16), h, "b_out_proj")


def _mixer_c(h, g_mix, w_in, gq, gk, w_o, tables):
    nq = C_GROUPS * C_HEADS * HEAD_DIM
    nkv = C_HEADS * HEAD_DIM
    scale = 1.0 / math.sqrt(HEAD_DIM)
    head_gain = jnp.concatenate(
        [jnp.tile(gq * scale, C_GROUPS * C_HEADS), jnp.tile(gk, C_HEADS), jnp.ones((nkv,), F32)])[None, :]
    qkv = _qkv_projection(h, g_mix, w_in.astype(BF16), head_gain, nq + nkv, tables, "c_qkv_proj")
    outs, lses = [], []
    for g, (window, dil) in enumerate(C_PATTERNS):
        o, lse = _banded_attention(
            qkv, dil=dil, row_width=nq + 2 * nkv, q_col=g, k_col=C_GROUPS, v_col=C_GROUPS + 1,
            n_q=C_HEADS, n_kv=C_HEADS, hw=window // 2 // dil, sink_col=None, has_lse=True,
            name=f"c_banded_attention_g{g}")
        outs.append(o)
        lses.append(lse)
    return _merge_out_projection(outs, lses, w_o.astype(BF16), h)


def kernel(x, p, positions, g_mix, g_ffn, g_ple, w_ple_gate, w_ple_proj,
           w_ffn_gate, w_ffn_up, w_ffn_down,
           a_w_in, a_q_norm, a_k_norm, a_sink, a_w_o,
           b_w_in, b_q_lat_norm, b_kv_lat_norm, b_w_q_up, b_w_kv_up, b_q_norm, b_k_norm, b_w_o,
           c_w_in, c_q_norm, c_k_norm, c_w_o):
    h = x.reshape(SEQ, D_MODEL)
    pos_col = positions.reshape(SEQ, 1)
    tables_ac = _rope_tables(pos_col, PARTIAL_ROT)
    tables_b = _rope_tables(pos_col, B_ROPE)
    for i in range(DEPTH):
        kind, slot = i % 3, i // 3
        gm = g_mix[i][None, :]
        if kind == 0:
            h = _mixer_a(h, gm, a_w_in[slot], a_q_norm[slot], a_k_norm[slot], a_sink[slot],
                         a_w_o[slot], tables_ac)
        elif kind == 1:
            h = _mixer_b(h, gm, b_w_in[slot], b_q_lat_norm[slot][None, :],
                         b_kv_lat_norm[slot][None, :], b_w_q_up[slot], b_w_kv_up[slot],
                         b_q_norm[slot], b_k_norm[slot], b_w_o[slot], tables_b)
        else:
            h = _mixer_c(h, gm, c_w_in[slot], c_q_norm[slot], c_k_norm[slot], c_w_o[slot],
                         tables_ac)
        h = _ffn(h, g_ffn[i][None, :], w_ffn_gate[i].astype(BF16), w_ffn_up[i].astype(BF16),
                 w_ffn_down[i].astype(BF16))
        h = _ple(h, g_ple[i][None, :], w_ple_gate[i].astype(BF16), p[i].reshape(SEQ, PLE_DIM),
                 w_ple_proj[i].astype(BF16))
    return h.reshape(1, SEQ, D_MODEL)
```

```python
import functools
import math

import jax
import jax.numpy as jnp
from jax import lax
from jax.experimental import pallas as pl
from jax.experimental.pallas import tpu as pltpu

F32 = jnp.float32
BF16 = jnp.bfloat16

SEQ = 8192
D_MODEL = 2048
DEPTH = 4
HEAD_DIM = 128
ROPE_THETA = 500000.0
PARTIAL_ROT = HEAD_DIM // 4
NORM_EPS = 1e-6
NEG = -1e30
LANES = 128
HALF_LANES = LANES // 2

A_HEADS = 16
A_KV_HEADS = 4
A_HALF_WINDOW = 128
B_HEADS = 16
B_Q_RANK = 512
B_KV_RANK = 512
B_NOPE = 128
B_ROPE = 64
B_QK = B_NOPE + B_ROPE
B_HEAD_PAD = 256
B_IN_PAD = B_Q_RANK + B_KV_RANK + LANES
C_PATTERNS = ((128, 1), (512, 4), (2048, 16))
C_GROUPS = 3
C_HEADS = 16
D_FF = 5632
PLE_DIM = 256

VMEM_LIMIT = 56 * 1024 * 1024


def _params(*sem):
    return pltpu.CompilerParams(dimension_semantics=sem, vmem_limit_bytes=VMEM_LIMIT)


def _rms_scale(x, width):
    ss = jnp.sum(x * x, axis=-1, keepdims=True)
    return x * lax.rsqrt(ss * (1.0 / width) + NORM_EPS)


def _rope_lanes(y, cos, sin):
    return y * cos + pltpu.roll(y, HALF_LANES, 1) * sin


def _rotary_slab(x, half):
    pad = jnp.zeros(x.shape[:-1] + (HALF_LANES - half,), x.dtype)
    return jnp.concatenate([x[..., :half], pad, x[..., half:], pad], axis=-1)


def _pair_head_lanes(w, n_heads):
    lead = w.shape[:-1]
    w = w.reshape(lead + (n_heads, HEAD_DIM))
    h = PARTIAL_ROT // 2
    w = jnp.concatenate(
        [w[..., :h], w[..., 2 * h:HALF_LANES + h], w[..., h:2 * h], w[..., HALF_LANES + h:]], axis=-1)
    return w.reshape(lead + (n_heads * HEAD_DIM,))


def _rope_table_kernel(pos_ref, inv_ref, sign_ref, cos_ref, sin_ref):
    ang = pos_ref[...].astype(F32) * inv_ref[...]
    cos_ref[...] = jnp.cos(ang)
    sin_ref[...] = jnp.sin(ang) * sign_ref[...]


def _rope_tables(pos_col, rot_dim):
    half = rot_dim // 2
    inv = ROPE_THETA ** (-jnp.arange(half, dtype=F32) * 2.0 / rot_dim)
    inv_l = _rotary_slab(jnp.concatenate([inv, inv]), half)[None, :]
    sign = _rotary_slab(jnp.concatenate([-jnp.ones((half,), F32), jnp.ones((half,), F32)]), half)[None, :]
    tm = 1024
    row = pl.BlockSpec((1, LANES), lambda i: (0, 0))
    tab = pl.BlockSpec((tm, LANES), lambda i: (i, 0))
    return pl.pallas_call(
        _rope_table_kernel,
        grid=(SEQ // tm,),
        in_specs=[pl.BlockSpec((tm, 1), lambda i: (i, 0)), row, row],
        out_specs=[tab, tab],
        out_shape=[jax.ShapeDtypeStruct((SEQ, LANES), F32)] * 2,
        compiler_params=_params("parallel"),
        name="rope_tables",
    )(pos_col, inv_l, sign)


def _norm_matmul_kernel(x_ref, g_ref, w_ref, *rest, n_extra, epilogue, n_sub):
    extra = rest[:n_extra]
    outs = rest[n_extra:-1]
    xn_ref = rest[-1]

    @pl.when(pl.program_id(1) == 0)
    def _():
        x = x_ref[...]
        xn_ref[...] = (_rms_scale(x, x.shape[-1]) * g_ref[...]).astype(BF16)

    sub = xn_ref.shape[0] // n_sub
    for c in range(n_sub):
        rows = slice(c * sub, (c + 1) * sub)
        acc = jnp.dot(xn_ref[rows, :], w_ref[...], preferred_element_type=F32)
        epilogue(acc, rows, extra, outs)


def _norm_matmul(x, x_col, kin, gain, w, *, tm, tn, extra, extra_specs, epilogue,
                 out_shape, out_specs, name, n_sub=4):
    m = x.shape[0]
    n = w.shape[1]
    kernel = functools.partial(_norm_matmul_kernel, n_extra=len(extra), epilogue=epilogue,
                               n_sub=n_sub)
    return pl.pallas_call(
        kernel,
        grid=(m // tm, n // tn),
        in_specs=[pl.BlockSpec((tm, kin), lambda i, j: (i, x_col)),
                  pl.BlockSpec((1, kin), lambda i, j: (0, 0)),
                  pl.BlockSpec((kin, tn), lambda i, j: (0, j))] + list(extra_specs),
        out_specs=out_specs,
        out_shape=out_shape,
        scratch_shapes=[pltpu.VMEM((tm, kin), BF16)],
        compiler_params=_params("parallel", "arbitrary"),
        name=name,
    )(x, gain, w, *extra)


def _plain_epilogue(acc, rows, extra, outs):
    outs[0][rows, :] = acc.astype(outs[0].dtype)


def _head_norm_rope_epilogue(acc, rows, extra, outs):
    gain_ref, cos_ref, sin_ref = extra
    (o_ref,) = outs
    cos, sin = cos_ref[rows, :], sin_ref[rows, :]
    for c in range(acc.shape[1] // HEAD_DIM):
        cols = slice(c * HEAD_DIM, (c + 1) * HEAD_DIM)
        y = _rms_scale(acc[:, cols], HEAD_DIM) * gain_ref[:, cols]
        o_ref[rows, cols] = _rope_lanes(y, cos, sin).astype(BF16)


def _qkv_projection(h, g_mix, w_qk, w_v, head_gain, tables, name):
    tm, tn = 1024, 512
    tab = pl.BlockSpec((tm, LANES), lambda i, j: (i, 0))
    qk = _norm_matmul(
        h, 0, D_MODEL, g_mix, w_qk, tm=tm, tn=tn,
        extra=(head_gain,) + tuple(tables),
        extra_specs=[pl.BlockSpec((1, tn), lambda i, j: (0, j)), tab, tab],
        epilogue=_head_norm_rope_epilogue,
        out_shape=jax.ShapeDtypeStruct((SEQ, w_qk.shape[1]), BF16),
        out_specs=pl.BlockSpec((tm, tn), lambda i, j: (i, j)),
        name=name + "_qk_proj")
    v = _norm_matmul(
        h, 0, D_MODEL, g_mix, w_v, tm=tm, tn=tn, extra=(), extra_specs=[],
        epilogue=_plain_epilogue,
        out_shape=jax.ShapeDtypeStruct((SEQ, w_v.shape[1]), BF16),
        out_specs=pl.BlockSpec((tm, tn), lambda i, j: (i, j)),
        name=name + "_v_proj")
    return qk, v


def _b_q_epilogue(acc, rows, extra, outs):
    gain_ref, cos_ref, sin_ref = extra
    (o_ref,) = outs
    cos, sin = cos_ref[rows, :], sin_ref[rows, :]
    for c in range(acc.shape[1] // B_HEAD_PAD):
        c0 = c * B_HEAD_PAD
        y = _rms_scale(acc[:, c0:c0 + B_HEAD_PAD], B_QK) * gain_ref[:, c0:c0 + B_HEAD_PAD]
        o_ref[rows, c0:c0 + B_NOPE] = y[:, :B_NOPE].astype(BF16)
        o_ref[rows, c0 + B_NOPE:c0 + B_HEAD_PAD] = _rope_lanes(y[:, B_NOPE:], cos, sin).astype(BF16)


def _b_k_epilogue(acc, rows, extra, outs):
    gain_ref, krope_ref, cos_ref, sin_ref = extra
    (o_ref,) = outs
    cos, sin = cos_ref[rows, :], sin_ref[rows, :]
    kr = krope_ref[rows, :]
    kr_ss = jnp.sum(kr * kr, axis=-1, keepdims=True)
    g_nope = gain_ref[:, :B_NOPE]
    g_rope = gain_ref[:, B_NOPE:]
    for c in range(acc.shape[1] // B_NOPE):
        y = acc[:, c * B_NOPE:(c + 1) * B_NOPE]
        ss = jnp.sum(y * y, axis=-1, keepdims=True) + kr_ss
        rinv = lax.rsqrt(ss * (1.0 / B_QK) + NORM_EPS)
        c0 = c * B_HEAD_PAD
        o_ref[rows, c0:c0 + B_NOPE] = (y * rinv * g_nope).astype(BF16)
        o_ref[rows, c0 + B_NOPE:c0 + B_HEAD_PAD] = _rope_lanes(
            kr * rinv * g_rope, cos, sin).astype(BF16)


def _b_v_epilogue(acc, rows, extra, outs):
    (o_ref,) = outs
    ones_blk = jnp.ones((acc.shape[0], LANES), BF16)
    for c in range(acc.shape[1] // B_NOPE):
        c0 = c * B_HEAD_PAD
        o_ref[rows, c0:c0 + B_NOPE] = acc[:, c * B_NOPE:(c + 1) * B_NOPE].astype(BF16)
        o_ref[rows, c0 + B_NOPE:c0 + B_HEAD_PAD] = ones_blk


def _banded_kernel(*refs, tq, hw, n_kv, group, length, q_axis, has_sink, has_lse):
    q_ref, kp_ref, kc_ref, kn_ref, vp_ref, vc_ref, vn_ref, band_ref = refs[:8]
    nxt = 8
    sink_ref = None
    if has_sink:
        sink_ref = refs[nxt]
        nxt += 1
    o_ref = refs[nxt]
    lse_ref = refs[nxt + 1] if has_lse else None

    i = pl.program_id(q_axis)
    win = tq + 2 * hw
    n_col = win // LANES
    kpos = i * tq - hw + lax.broadcasted_iota(jnp.int32, (1, win), 1)
    edge = jnp.where((kpos >= 0) & (kpos < length), 0.0, NEG)
    bias = band_ref[...] + edge
    if group > 1:
        bias = jnp.concatenate([bias] * group, axis=0)
    ones_blk = jnp.ones((win, LANES), BF16)
    lane = lax.broadcasted_iota(jnp.int32, (tq, LANES), 1)
    lse_tile = jnp.zeros((tq, LANES), F32)

    for kv in range(n_kv):
        kcols = slice(kv * HEAD_DIM, (kv + 1) * HEAD_DIM)
        k_win = jnp.concatenate(
            [kp_ref[tq - hw:, kcols], kc_ref[:, kcols], kn_ref[:hw, kcols]], axis=0)
        v_win = jnp.concatenate(
            [vp_ref[tq - hw:, kcols], vc_ref[:, kcols], vn_ref[:hw, kcols]], axis=0)
        v_ext = jnp.concatenate([v_win, ones_blk], axis=1)
        heads = [kv * group + c for c in range(group)]
        q = jnp.concatenate(
            [q_ref[:, hd * HEAD_DIM:(hd + 1) * HEAD_DIM] for hd in heads], axis=0)
        s = lax.dot_general(q, k_win, (((1,), (1,)), ((), ())), preferred_element_type=F32) + bias
        mx = s[:, :LANES]
        for cb in range(1, n_col):
            mx = jnp.maximum(mx, s[:, cb * LANES:(cb + 1) * LANES])
        m = jnp.broadcast_to(jnp.max(mx, axis=-1, keepdims=True), mx.shape)
        if has_sink:
            sink = sink_ref[kv * group * tq:(kv + 1) * group * tq, :]
            m = jnp.maximum(m, sink)
        p = jnp.exp(s - jnp.concatenate([m] * n_col, axis=1)).astype(BF16)
        o_ext = jnp.dot(p, v_ext, preferred_element_type=F32)
        denom = o_ext[:, HEAD_DIM:]
        if has_sink:
            denom = denom + jnp.exp(sink - m)
        o = o_ext[:, :HEAD_DIM] / denom
        for c, hd in enumerate(heads):
            o_ref[:, hd * HEAD_DIM:(hd + 1) * HEAD_DIM] = o[c * tq:(c + 1) * tq].astype(o_ref.dtype)
        if has_lse:
            lse_tile = jnp.where(lane == kv, m + jnp.log(denom), lse_tile)
    if has_lse:
        lse_ref[...] = lse_tile


def _banded_attention(q_src, k_src, v_src, *, dil, n_q, n_kv, hw, sink_rep, has_lse, name):
    tq = 128
    length = SEQ // dil
    nb = length // tq
    group = n_q // n_kv
    qw, kw = n_q * HEAD_DIM, n_kv * HEAD_DIM

    def chain_view(src):
        arr, _ = src
        return arr.reshape(length, dil * arr.shape[1])

    def spec(src, width, shift):
        arr, col = src
        per_row = arr.shape[1] // width
        return pl.BlockSpec(
            (tq, width), lambda r, i: (jnp.clip(i + shift, 0, nb - 1), r * per_row + col))

    win = tq + 2 * hw
    rows = jnp.arange(tq, dtype=jnp.int32)[:, None]
    cols = jnp.arange(win, dtype=jnp.int32)[None, :]
    band = jnp.where(jnp.abs(rows + hw - cols) <= hw, 0.0, NEG).astype(F32)

    in_specs = [spec(q_src, qw, 0),
                spec(k_src, kw, -1), spec(k_src, kw, 0), spec(k_src, kw, 1),
                spec(v_src, kw, -1), spec(v_src, kw, 0), spec(v_src, kw, 1),
                pl.BlockSpec((tq, win), lambda r, i: (0, 0))]
    args = [chain_view(q_src)] + [chain_view(k_src)] * 3 + [chain_view(v_src)] * 3 + [band]
    if sink_rep is not None:
        in_specs.append(pl.BlockSpec(sink_rep.shape, lambda r, i: (0, 0)))
        args.append(sink_rep)
    out_shape = [jax.ShapeDtypeStruct((length, dil * qw), BF16)]
    out_specs = [pl.BlockSpec((tq, qw), lambda r, i: (i, r))]
    if has_lse:
        out_shape.append(jax.ShapeDtypeStruct((length, dil * LANES), F32))
        out_specs.append(pl.BlockSpec((tq, LANES), lambda r, i: (i, r)))
    kernel = functools.partial(
        _banded_kernel, tq=tq, hw=hw, n_kv=n_kv, group=group, length=length, q_axis=1,
        has_sink=sink_rep is not None, has_lse=has_lse)
    outs = pl.pallas_call(
        kernel, grid=(dil, nb), in_specs=in_specs, out_specs=out_specs, out_shape=out_shape,
        compiler_params=_params("parallel", "parallel"), name=name,
    )(*args)
    o = outs[0].reshape(SEQ, qw)
    if has_lse:
        return o, outs[1].reshape(SEQ, LANES)
    return o


def _flash_kernel(q_ref, k_ref, v_ref, o_ref, m_ref, acc_ref, *, tk):
    tq = q_ref.shape[0]
    n_chunks = k_ref.shape[0] // tk
    n_col = tk // LANES
    m_ref[...] = jnp.full((tq, LANES), NEG, F32)
    acc_ref[...] = jnp.zeros((tq, B_HEAD_PAD), F32)
    q = q_ref[...]

    for c in range(n_chunks):
        k = k_ref[c * tk:(c + 1) * tk, :]
        v = v_ref[c * tk:(c + 1) * tk, :]
        s = lax.dot_general(q, k, (((1,), (1,)), ((), ())), preferred_element_type=F32)
        mx = s[:, :LANES]
        for cb in range(1, n_col):
            mx = jnp.maximum(mx, s[:, cb * LANES:(cb + 1) * LANES])
        m_old = m_ref[...]
        m_new = jnp.maximum(m_old, jnp.max(mx, axis=-1, keepdims=True))
        alpha = jnp.exp2(m_old - m_new)
        p = jnp.exp2(s - jnp.concatenate([m_new] * n_col, axis=1)).astype(BF16)
        pv = jnp.dot(p, v, preferred_element_type=F32)
        acc_ref[...] = acc_ref[...] * jnp.concatenate([alpha, alpha], axis=1) + pv
        m_ref[...] = m_new

    acc = acc_ref[...]
    o_ref[...] = (acc[:, :B_NOPE] / acc[:, B_NOPE:]).astype(o_ref.dtype)


def _dense_attention(q, k, v_ext):
    tq, tk = 1024, 1024
    kv_spec = pl.BlockSpec((SEQ, B_HEAD_PAD), lambda h, i: (0, h))
    return pl.pallas_call(
        functools.partial(_flash_kernel, tk=tk),
        grid=(B_HEADS, SEQ // tq),
        in_specs=[pl.BlockSpec((tq, B_HEAD_PAD), lambda h, i: (i, h)), kv_spec, kv_spec],
        out_specs=pl.BlockSpec((tq, B_NOPE), lambda h, i: (i, h)),
        out_shape=jax.ShapeDtypeStruct((SEQ, B_HEADS * B_NOPE), BF16),
        scratch_shapes=[pltpu.VMEM((tq, LANES), F32), pltpu.VMEM((tq, B_HEAD_PAD), F32)],
        compiler_params=_params("parallel", "arbitrary"),
        name="b_flash_attention",
    )(q, k, v_ext)


def _out_proj_kernel(o_ref, w_ref, h_ref, out_ref):
    out_ref[...] = h_ref[...] + jnp.dot(o_ref[...], w_ref[...], preferred_element_type=F32)


def _out_projection(o, w, h, name):
    tm, tn = 1024, 1024
    return pl.pallas_call(
        _out_proj_kernel,
        grid=(SEQ // tm, D_MODEL // tn),
        in_specs=[pl.BlockSpec((tm, D_MODEL), lambda i, j: (i, 0)),
                  pl.BlockSpec((D_MODEL, tn), lambda i, j: (0, j)),
                  pl.BlockSpec((tm, tn), lambda i, j: (i, j))],
        out_specs=pl.BlockSpec((tm, tn), lambda i, j: (i, j)),
        out_shape=jax.ShapeDtypeStruct((SEQ, D_MODEL), F32),
        compiler_params=_params("parallel", "parallel"),
        name=name,
    )(o, w, h)


def _merge_out_proj_kernel(o0_ref, o1_ref, o2_ref, l0_ref, l1_ref, l2_ref, w_ref, h_ref,
                           out_ref, merged_ref):
    @pl.when(pl.program_id(1) == 0)
    def _():
        l0, l1, l2 = l0_ref[...], l1_ref[...], l2_ref[...]
        m = jnp.maximum(jnp.maximum(l0, l1), l2)
        e0, e1, e2 = jnp.exp(l0 - m), jnp.exp(l1 - m), jnp.exp(l2 - m)
        inv = 1.0 / (e0 + e1 + e2)
        w0, w1, w2 = e0 * inv, e1 * inv, e2 * inv
        for hd in range(C_HEADS):
            cols = slice(hd * HEAD_DIM, (hd + 1) * HEAD_DIM)
            merged = (w0[:, hd:hd + 1] * o0_ref[:, cols].astype(F32)
                      + w1[:, hd:hd + 1] * o1_ref[:, cols].astype(F32)
                      + w2[:, hd:hd + 1] * o2_ref[:, cols].astype(F32))
            merged_ref[:, cols] = merged.astype(BF16)

    out_ref[...] = h_ref[...] + jnp.dot(merged_ref[...], w_ref[...], preferred_element_type=F32)


def _merge_out_projection(outs, lses, w, h):
    tm, tn = 512, 1024
    o_spec = pl.BlockSpec((tm, D_MODEL), lambda i, j: (i, 0))
    l_spec = pl.BlockSpec((tm, LANES), lambda i, j: (i, 0))
    return pl.pallas_call(
        _merge_out_proj_kernel,
        grid=(SEQ // tm, D_MODEL // tn),
        in_specs=[o_spec] * 3 + [l_spec] * 3 + [
            pl.BlockSpec((D_MODEL, tn), lambda i, j: (0, j)),
            pl.BlockSpec((tm, tn), lambda i, j: (i, j))],
        out_specs=pl.BlockSpec((tm, tn), lambda i, j: (i, j)),
        out_shape=jax.ShapeDtypeStruct((SEQ, D_MODEL), F32),
        scratch_shapes=[pltpu.VMEM((tm, D_MODEL), BF16)],
        compiler_params=_params("parallel", "arbitrary"),
        name="c_merge_out_proj",
    )(*outs, *lses, w, h)


def _ffn_kernel(x_ref, g_ref, wg_ref, wu_ref, wd_ref, o_ref, xn_ref):
    @pl.when(pl.program_id(1) == 0)
    def _():
        x = x_ref[...]
        xn_ref[...] = (_rms_scale(x, D_MODEL) * g_ref[...]).astype(BF16)
        o_ref[...] = x

    xn = xn_ref[...]
    gate = jnp.dot(xn, wg_ref[...], preferred_element_type=F32)
    up = jnp.dot(xn, wu_ref[...], preferred_element_type=F32)
    act = (gate * jax.nn.sigmoid(gate) * up).astype(BF16)
    o_ref[...] += jnp.dot(act, wd_ref[...], preferred_element_type=F32)


def _ffn(h, g, wg, wu, wd):
    tm, tf = 512, 512
    return pl.pallas_call(
        _ffn_kernel,
        grid=(SEQ // tm, D_FF // tf),
        in_specs=[pl.BlockSpec((tm, D_MODEL), lambda i, f: (i, 0)),
                  pl.BlockSpec((1, D_MODEL), lambda i, f: (0, 0)),
                  pl.BlockSpec((D_MODEL, tf), lambda i, f: (0, f)),
                  pl.BlockSpec((D_MODEL, tf), lambda i, f: (0, f)),
                  pl.BlockSpec((tf, D_MODEL), lambda i, f: (f, 0))],
        out_specs=pl.BlockSpec((tm, D_MODEL), lambda i, f: (i, 0)),
        out_shape=jax.ShapeDtypeStruct((SEQ, D_MODEL), F32),
        scratch_shapes=[pltpu.VMEM((tm, D_MODEL), BF16)],
        compiler_params=_params("parallel", "arbitrary"),
        name="ffn_swiglu",
    )(h, g, wg, wu, wd)


def _ple_epilogue(acc, rows, extra, outs):
    p_ref, wp_ref, h_ref = extra
    proj = jnp.dot(p_ref[rows, :].astype(BF16), wp_ref[...], preferred_element_type=F32)
    outs[0][rows, :] = h_ref[rows, :] + jax.nn.sigmoid(acc) * proj


def _ple(h, g, w_gate, p, w_proj):
    tm, tn = 1024, 512
    return _norm_matmul(
        h, 0, D_MODEL, g, w_gate, tm=tm, tn=tn,
        extra=(p, w_proj, h),
        extra_specs=[pl.BlockSpec((tm, PLE_DIM), lambda i, j: (i, 0)),
                     pl.BlockSpec((PLE_DIM, tn), lambda i, j: (0, j)),
                     pl.BlockSpec((tm, tn), lambda i, j: (i, j))],
        epilogue=_ple_epilogue,
        out_shape=jax.ShapeDtypeStruct((SEQ, D_MODEL), F32),
        out_specs=pl.BlockSpec((tm, tn), lambda i, j: (i, j)),
        name="ple_gate")


def _mixer_a(h, g_mix, w_in, gq, gk, sink, w_o, tables):
    nq, nk = A_HEADS * HEAD_DIM, A_KV_HEADS * HEAD_DIM
    scale = 1.0 / math.sqrt(HEAD_DIM)
    w = w_in.astype(BF16)
    w_qk = _pair_head_lanes(w[:, :nq + nk], A_HEADS + A_KV_HEADS)
    head_gain = _pair_head_lanes(
        jnp.concatenate([jnp.tile(gq * scale, A_HEADS), jnp.tile(gk, A_KV_HEADS)]),
        A_HEADS + A_KV_HEADS)[None, :]
    qk, v = _qkv_projection(h, g_mix, w_qk, w[:, nq + nk:], head_gain, tables, "a")
    sink_rep = jnp.broadcast_to(jnp.repeat(sink, 128)[:, None], (A_HEADS * 128, LANES))
    o = _banded_attention(
        (qk, 0), (qk, nq // nk), (v, 0), dil=1, n_q=A_HEADS, n_kv=A_KV_HEADS, hw=A_HALF_WINDOW,
        sink_rep=sink_rep, has_lse=False, name="a_banded_attention")
    return _out_projection(o, w_o.astype(BF16), h, "a_out_proj")


def _mixer_b(h, g_mix, w_in, g_qlat, g_kvlat, w_q_up, w_kv_up, gq, gk, w_o, tables):
    scale = math.log2(math.e) / math.sqrt(B_QK)
    half = B_ROPE // 2
    w_in = w_in.astype(BF16)
    n_lat = B_Q_RANK + B_KV_RANK
    w_in_pad = jnp.concatenate([w_in[:, :n_lat], _rotary_slab(w_in[:, n_lat:], half)], axis=1)
    tm = 1024
    lat = _norm_matmul(
        h, 0, D_MODEL, g_mix, w_in_pad, tm=tm, tn=B_IN_PAD, extra=(), extra_specs=[],
        epilogue=_plain_epilogue,
        out_shape=jax.ShapeDtypeStruct((SEQ, B_IN_PAD), F32),
        out_specs=pl.BlockSpec((tm, B_IN_PAD), lambda i, j: (i, j)),
        name="b_latent_proj")

    tab = pl.BlockSpec((tm, LANES), lambda i, j: (i, 0))
    tn = 1024

    def head_slab(x):
        return jnp.concatenate([x[..., :B_NOPE], _rotary_slab(x[..., B_NOPE:], half)], axis=-1)

    wq = head_slab(w_q_up.astype(BF16).reshape(B_Q_RANK, B_HEADS, B_QK)).reshape(
        B_Q_RANK, B_HEADS * B_HEAD_PAD)
    gq_slab = jnp.tile(head_slab(gq * scale), B_HEADS)[None, :]
    q = _norm_matmul(
        lat, 0, B_Q_RANK, g_qlat, wq, tm=tm, tn=tn,
        extra=(gq_slab,) + tuple(tables),
        extra_specs=[pl.BlockSpec((1, tn), lambda i, j: (0, j)), tab, tab],
        epilogue=_b_q_epilogue,
        out_shape=jax.ShapeDtypeStruct((SEQ, B_HEADS * B_HEAD_PAD), BF16),
        out_specs=pl.BlockSpec((tm, tn), lambda i, j: (i, j)),
        name="b_q_proj")

    wkv = w_kv_up.astype(BF16).reshape(B_KV_RANK, B_HEADS, 2 * B_NOPE)
    wk = wkv[:, :, :B_NOPE].reshape(B_KV_RANK, B_HEADS * B_NOPE)
    wv = wkv[:, :, B_NOPE:].reshape(B_KV_RANK, B_HEADS * B_NOPE)
    gk_slab = head_slab(gk)[None, :]
    k = _norm_matmul(
        lat, 1, B_KV_RANK, g_kvlat, wk, tm=tm, tn=tn,
        extra=(gk_slab, lat) + tuple(tables),
        extra_specs=[pl.BlockSpec((1, B_HEAD_PAD), lambda i, j: (0, 0)),
                     pl.BlockSpec((tm, LANES), lambda i, j: (i, n_lat // LANES)),
                     tab, tab],
        epilogue=_b_k_epilogue,
        out_shape=jax.ShapeDtypeStruct((SEQ, B_HEADS * B_HEAD_PAD), BF16),
        out_specs=pl.BlockSpec((tm, 2 * tn), lambda i, j: (i, j)),
        name="b_k_proj")
    v_ext = _norm_matmul(
        lat, 1, B_KV_RANK, g_kvlat, wv, tm=tm, tn=tn, extra=(), extra_specs=[],
        epilogue=_b_v_epilogue,
        out_shape=jax.ShapeDtypeStruct((SEQ, B_HEADS * B_HEAD_PAD), BF16),
        out_specs=pl.BlockSpec((tm, 2 * tn), lambda i, j: (i, j)),
        name="b_v_proj")
    o = _dense_attention(q, k, v_ext)
    return _out_projection(o, w_o.astype(BF16), h, "b_out_proj")


def _mixer_c(h, g_mix, w_in, gq, gk, w_o, tables):
    n_qh = C_GROUPS * C_HEADS
    nq = n_qh * HEAD_DIM
    nkv = C_HEADS * HEAD_DIM
    scale = 1.0 / math.sqrt(HEAD_DIM)
    w = w_in.astype(BF16)
    w_qk = _pair_head_lanes(w[:, :nq + nkv], n_qh + C_HEADS)
    head_gain = _pair_head_lanes(
        jnp.concatenate([jnp.tile(gq * scale, n_qh), jnp.tile(gk, C_HEADS)]),
        n_qh + C_HEADS)[None, :]
    qk, v = _qkv_projection(h, g_mix, w_qk, w[:, nq + nkv:], head_gain, tables, "c")
    outs, lses = [], []
    for g, (window, dil) in enumerate(C_PATTERNS):
        o, lse = _banded_attention(
            (qk, g), (qk, C_GROUPS), (v, 0), dil=dil, n_q=C_HEADS, n_kv=C_HEADS,
            hw=window // 2 // dil, sink_rep=None, has_lse=True,
            name=f"c_banded_attention_g{g}")
        outs.append(o)
        lses.append(lse)
    return _merge_out_projection(outs, lses, w_o.astype(BF16), h)


def kernel(x, p, positions, g_mix, g_ffn, g_ple, w_ple_gate, w_ple_proj,
           w_ffn_gate, w_ffn_up, w_ffn_down,
           a_w_in, a_q_norm, a_k_norm, a_sink, a_w_o,
           b_w_in, b_q_lat_norm, b_kv_lat_norm, b_w_q_up, b_w_kv_up, b_q_norm, b_k_norm, b_w_o,
           c_w_in, c_q_norm, c_k_norm, c_w_o):
    h = x.reshape(SEQ, D_MODEL)
    pos_col = positions.reshape(SEQ, 1)
    tables_ac = _rope_tables(pos_col, PARTIAL_ROT)
    tables_b = _rope_tables(pos_col, B_ROPE)
    for i in range(DEPTH):
        kind, slot = i % 3, i // 3
        gm = g_mix[i][None, :]
        if kind == 0:
            h = _mixer_a(h, gm, a_w_in[slot], a_q_norm[slot], a_k_norm[slot], a_sink[slot],
                         a_w_o[slot], tables_ac)
        elif kind == 1:
            h = _mixer_b(h, gm, b_w_in[slot], b_q_lat_norm[slot][None, :],
                         b_kv_lat_norm[slot][None, :], b_w_q_up[slot], b_w_kv_up[slot],
                         b_q_norm[slot], b_k_norm[slot], b_w_o[slot], tables_b)
        else:
            h = _mixer_c(h, gm, c_w_in[slot], c_q_norm[slot], c_k_norm[slot], c_w_o[slot],
                         tables_ac)
        h = _ffn(h, g_ffn[i][None, :], w_ffn_gate[i].astype(BF16), w_ffn_up[i].astype(BF16),
                 w_ffn_down[i].astype(BF16))
        h = _ple(h, g_ple[i][None, :], w_ple_gate[i].astype(BF16), p[i].reshape(SEQ, PLE_DIM),
                 w_ple_proj[i].astype(BF16))
    return h.reshape(1, SEQ, D_MODEL)
```

```python
import functools
import math

import jax
import jax.numpy as jnp
from jax import lax
from jax.experimental import pallas as pl
from jax.experimental.pallas import tpu as pltpu

F32 = jnp.float32
BF16 = jnp.bfloat16

SEQ = 8192
D_MODEL = 2048
DEPTH = 4
HEAD_DIM = 128
ROPE_THETA = 500000.0
PARTIAL_ROT = HEAD_DIM // 4
NORM_EPS = 1e-6
NEG = -1e30
LANES = 128
HALF_LANES = LANES // 2

A_HEADS = 16
A_KV_HEADS = 4
A_HALF_WINDOW = 128
B_HEADS = 16
B_Q_RANK = 512
B_KV_RANK = 512
B_NOPE = 128
B_ROPE = 64
B_QK = B_NOPE + B_ROPE
B_HEAD_PAD = 256
B_IN_PAD = B_Q_RANK + B_KV_RANK + LANES
C_PATTERNS = ((128, 1), (512, 4), (2048, 16))
C_GROUPS = 3
C_HEADS = 16
D_FF = 5632
PLE_DIM = 256

VMEM_LIMIT = 56 * 1024 * 1024


def _params(*sem):
    return pltpu.CompilerParams(dimension_semantics=sem, vmem_limit_bytes=VMEM_LIMIT)


def _rms_scale(x, width):
    ss = jnp.sum(x * x, axis=-1, keepdims=True)
    return x * lax.rsqrt(ss * (1.0 / width) + NORM_EPS)


def _rope_adjacent(y, cos, sin_lo, sin_hi, half):
    return (y * cos + pltpu.roll(y, LANES - half, 1) * sin_lo
            + pltpu.roll(y, half, 1) * sin_hi)


def _rope_split(y, cos, sin):
    return y * cos + pltpu.roll(y, HALF_LANES, 1) * sin


def _rotary_slab(x, half):
    pad = jnp.zeros(x.shape[:-1] + (HALF_LANES - half,), x.dtype)
    return jnp.concatenate([x[..., :half], pad, x[..., half:], pad], axis=-1)


def _rope_table_kernel(pos_ref, inv_ref, sign_ref, cos_ref, *sin_refs):
    ang = pos_ref[...].astype(F32) * inv_ref[...]
    cos_ref[...] = jnp.cos(ang)
    s = jnp.sin(ang)
    for r, sin_ref in enumerate(sin_refs):
        sin_ref[...] = s * sign_ref[r:r + 1, :]


def _rope_tables(pos_col, inv_lanes, signs):
    tm = 1024
    n = signs.shape[0]
    tab = pl.BlockSpec((tm, LANES), lambda i: (i, 0))
    return pl.pallas_call(
        _rope_table_kernel,
        grid=(SEQ // tm,),
        in_specs=[pl.BlockSpec((tm, 1), lambda i: (i, 0)),
                  pl.BlockSpec((1, LANES), lambda i: (0, 0)),
                  pl.BlockSpec((n, LANES), lambda i: (0, 0))],
        out_specs=[tab] * (n + 1),
        out_shape=[jax.ShapeDtypeStruct((SEQ, LANES), F32)] * (n + 1),
        compiler_params=_params("parallel"),
        name="rope_tables",
    )(pos_col, inv_lanes, signs)


def _inv_freq(rot_dim):
    half = rot_dim // 2
    return ROPE_THETA ** (-jnp.arange(half, dtype=F32) * 2.0 / rot_dim)


def _tables_adjacent(pos_col, rot_dim):
    half = rot_dim // 2
    inv = _inv_freq(rot_dim)
    rest = jnp.zeros((LANES - rot_dim,), F32)
    zero, one = jnp.zeros((half,), F32), jnp.ones((half,), F32)
    inv_l = jnp.concatenate([inv, inv, rest])[None, :]
    signs = jnp.stack([jnp.concatenate([-one, zero, rest]), jnp.concatenate([zero, one, rest])])
    return _rope_tables(pos_col, inv_l, signs)


def _tables_split(pos_col, rot_dim):
    half = rot_dim // 2
    inv = _inv_freq(rot_dim)
    one = jnp.ones((half,), F32)
    inv_l = _rotary_slab(jnp.concatenate([inv, inv]), half)[None, :]
    signs = _rotary_slab(jnp.concatenate([-one, one]), half)[None, :]
    return _rope_tables(pos_col, inv_l, signs)


def _norm_matmul_kernel(x_ref, g_ref, w_ref, *rest, n_extra, epilogue, n_sub):
    extra = rest[:n_extra]
    outs = rest[n_extra:-1]
    xn_ref = rest[-1]

    @pl.when(pl.program_id(1) == 0)
    def _():
        x = x_ref[...]
        xn_ref[...] = (_rms_scale(x, x.shape[-1]) * g_ref[...]).astype(BF16)

    sub = xn_ref.shape[0] // n_sub
    for c in range(n_sub):
        rows = slice(c * sub, (c + 1) * sub)
        acc = jnp.dot(xn_ref[rows, :], w_ref[...], preferred_element_type=F32)
        epilogue(acc, rows, extra, outs)


def _norm_matmul(x, x_col, kin, gain, w, w_slot, w_col0, n_tiles, *, tm, tn, extra, extra_specs,
                 epilogue, out_shape, out_specs, name, n_sub=4):
    m = x.shape[0]
    kernel = functools.partial(_norm_matmul_kernel, n_extra=len(extra), epilogue=epilogue,
                               n_sub=n_sub)
    return pl.pallas_call(
        kernel,
        grid=(m // tm, n_tiles),
        in_specs=[pl.BlockSpec((tm, kin), lambda i, j: (i, x_col)),
                  pl.BlockSpec((1, kin), lambda i, j: (0, 0)),
                  pl.BlockSpec((None, kin, tn), lambda i, j: (w_slot, 0, w_col0 + j))]
        + list(extra_specs),
        out_specs=out_specs,
        out_shape=out_shape,
        scratch_shapes=[pltpu.VMEM((tm, kin), BF16)],
        compiler_params=_params("parallel", "arbitrary"),
        name=name,
    )(x, gain, w, *extra)


def _plain_epilogue(acc, rows, extra, outs):
    outs[0][rows, :] = acc.astype(outs[0].dtype)


def _head_norm_rope_epilogue(acc, rows, extra, outs):
    gain_ref, cos_ref, slo_ref, shi_ref = extra
    (o_ref,) = outs
    cos, slo, shi = cos_ref[rows, :], slo_ref[rows, :], shi_ref[rows, :]
    for c in range(acc.shape[1] // HEAD_DIM):
        cols = slice(c * HEAD_DIM, (c + 1) * HEAD_DIM)
        y = _rms_scale(acc[:, cols], HEAD_DIM) * gain_ref[:, cols]
        o_ref[rows, cols] = _rope_adjacent(y, cos, slo, shi, PARTIAL_ROT // 2).astype(BF16)


def _qkv_projection(h, g_mix, w, slot, n_qk, n_v, head_gain, tables, name):
    tm, tn = 1024, 512
    tab = pl.BlockSpec((tm, LANES), lambda i, j: (i, 0))
    qk = _norm_matmul(
        h, 0, D_MODEL, g_mix, w, slot, 0, n_qk // tn, tm=tm, tn=tn,
        extra=(head_gain,) + tuple(tables),
        extra_specs=[pl.BlockSpec((1, tn), lambda i, j: (0, j)), tab, tab, tab],
        epilogue=_head_norm_rope_epilogue,
        out_shape=jax.ShapeDtypeStruct((SEQ, n_qk), BF16),
        out_specs=pl.BlockSpec((tm, tn), lambda i, j: (i, j)),
        name=name + "_qk_proj")
    v = _norm_matmul(
        h, 0, D_MODEL, g_mix, w, slot, n_qk // tn, n_v // tn, tm=tm, tn=tn,
        extra=(), extra_specs=[], epilogue=_plain_epilogue,
        out_shape=jax.ShapeDtypeStruct((SEQ, n_v), BF16),
        out_specs=pl.BlockSpec((tm, tn), lambda i, j: (i, j)),
        name=name + "_v_proj")
    return qk, v


def _b_q_epilogue(acc, rows, extra, outs):
    gain_ref, cos_ref, sin_ref = extra
    (o_ref,) = outs
    cos, sin = cos_ref[rows, :], sin_ref[rows, :]
    for c in range(acc.shape[1] // B_HEAD_PAD):
        c0 = c * B_HEAD_PAD
        y = _rms_scale(acc[:, c0:c0 + B_HEAD_PAD], B_QK) * gain_ref[:, c0:c0 + B_HEAD_PAD]
        o_ref[rows, c0:c0 + B_NOPE] = y[:, :B_NOPE].astype(BF16)
        o_ref[rows, c0 + B_NOPE:c0 + B_HEAD_PAD] = _rope_split(y[:, B_NOPE:], cos, sin).astype(BF16)


def _b_kv_epilogue(acc, rows, extra, outs):
    gain_ref, krope_ref, cos_ref, sin_ref = extra
    k_ref, v_ref = outs
    cos, sin = cos_ref[rows, :], sin_ref[rows, :]
    kr = krope_ref[rows, :]
    kr_ss = jnp.sum(kr * kr, axis=-1, keepdims=True)
    g_nope = gain_ref[:, :B_NOPE]
    g_rope = gain_ref[:, B_NOPE:]
    ones_blk = jnp.ones((acc.shape[0], LANES), BF16)
    for c in range(acc.shape[1] // B_HEAD_PAD):
        c0 = c * B_HEAD_PAD
        y = acc[:, c0:c0 + B_NOPE]
        ss = jnp.sum(y * y, axis=-1, keepdims=True) + kr_ss
        rinv = lax.rsqrt(ss * (1.0 / B_QK) + NORM_EPS)
        k_ref[rows, c0:c0 + B_NOPE] = (y * rinv * g_nope).astype(BF16)
        k_ref[rows, c0 + B_NOPE:c0 + B_HEAD_PAD] = _rope_split(
            kr * rinv * g_rope, cos, sin).astype(BF16)
        v_ref[rows, c0:c0 + B_NOPE] = acc[:, c0 + B_NOPE:c0 + B_HEAD_PAD].astype(BF16)
        v_ref[rows, c0 + B_NOPE:c0 + B_HEAD_PAD] = ones_blk


def _banded_kernel(*refs, tq, hw, n_kv, group, length, q_axis, has_sink, has_lse):
    q_ref, kp_ref, kc_ref, kn_ref, vp_ref, vc_ref, vn_ref, band_ref = refs[:8]
    nxt = 8
    sink_ref = None
    if has_sink:
        sink_ref = refs[nxt]
        nxt += 1
    o_ref = refs[nxt]
    lse_ref = refs[nxt + 1] if has_lse else None

    i = pl.program_id(q_axis)
    win = tq + 2 * hw
    n_col = win // LANES
    kpos = i * tq - hw + lax.broadcasted_iota(jnp.int32, (1, win), 1)
    edge = jnp.where((kpos >= 0) & (kpos < length), 0.0, NEG)
    bias = band_ref[...] + edge
    if group > 1:
        bias = jnp.concatenate([bias] * group, axis=0)
    ones_blk = jnp.ones((win, LANES), BF16)
    lane = lax.broadcasted_iota(jnp.int32, (tq, LANES), 1)
    lse_tile = jnp.zeros((tq, LANES), F32)

    for kv in range(n_kv):
        kcols = slice(kv * HEAD_DIM, (kv + 1) * HEAD_DIM)
        k_win = jnp.concatenate(
            [kp_ref[tq - hw:, kcols], kc_ref[:, kcols], kn_ref[:hw, kcols]], axis=0)
        v_win = jnp.concatenate(
            [vp_ref[tq - hw:, kcols], vc_ref[:, kcols], vn_ref[:hw, kcols]], axis=0)
        v_ext = jnp.concatenate([v_win, ones_blk], axis=1)
        heads = [kv * group + c for c in range(group)]
        q = jnp.concatenate(
            [q_ref[:, hd * HEAD_DIM:(hd + 1) * HEAD_DIM] for hd in heads], axis=0)
        s = lax.dot_general(q, k_win, (((1,), (1,)), ((), ())), preferred_element_type=F32) + bias
        mx = s[:, :LANES]
        for cb in range(1, n_col):
            mx = jnp.maximum(mx, s[:, cb * LANES:(cb + 1) * LANES])
        m = jnp.broadcast_to(jnp.max(mx, axis=-1, keepdims=True), mx.shape)
        if has_sink:
            sink = sink_ref[kv * group * tq:(kv + 1) * group * tq, :]
            m = jnp.maximum(m, sink)
        p = jnp.exp(s - jnp.concatenate([m] * n_col, axis=1)).astype(BF16)
        o_ext = jnp.dot(p, v_ext, preferred_element_type=F32)
        denom = o_ext[:, HEAD_DIM:]
        if has_sink:
            denom = denom + jnp.exp(sink - m)
        o = o_ext[:, :HEAD_DIM] / denom
        for c, hd in enumerate(heads):
            o_ref[:, hd * HEAD_DIM:(hd + 1) * HEAD_DIM] = o[c * tq:(c + 1) * tq].astype(o_ref.dtype)
        if has_lse:
            lse_tile = jnp.where(lane == kv, m + jnp.log(denom), lse_tile)
    if has_lse:
        lse_ref[...] = lse_tile


def _banded_attention(q_src, k_src, v_src, *, dil, n_q, n_kv, hw, sink_rep, has_lse, name):
    tq = 128
    length = SEQ // dil
    nb = length // tq
    group = n_q // n_kv
    qw, kw = n_q * HEAD_DIM, n_kv * HEAD_DIM

    def chain_view(src):
        arr, _ = src
        return arr.reshape(length, dil * arr.shape[1])

    def spec(src, width, shift):
        arr, col = src
        per_row = arr.shape[1] // width
        return pl.BlockSpec(
            (tq, width), lambda r, i: (jnp.clip(i + shift, 0, nb - 1), r * per_row + col))

    win = tq + 2 * hw
    rows = jnp.arange(tq, dtype=jnp.int32)[:, None]
    cols = jnp.arange(win, dtype=jnp.int32)[None, :]
    band = jnp.where(jnp.abs(rows + hw - cols) <= hw, 0.0, NEG).astype(F32)

    in_specs = [spec(q_src, qw, 0),
                spec(k_src, kw, -1), spec(k_src, kw, 0), spec(k_src, kw, 1),
                spec(v_src, kw, -1), spec(v_src, kw, 0), spec(v_src, kw, 1),
                pl.BlockSpec((tq, win), lambda r, i: (0, 0))]
    args = [chain_view(q_src)] + [chain_view(k_src)] * 3 + [chain_view(v_src)] * 3 + [band]
    if sink_rep is not None:
        in_specs.append(pl.BlockSpec(sink_rep.shape, lambda r, i: (0, 0)))
        args.append(sink_rep)
    out_shape = [jax.ShapeDtypeStruct((length, dil * qw), BF16)]
    out_specs = [pl.BlockSpec((tq, qw), lambda r, i: (i, r))]
    if has_lse:
        out_shape.append(jax.ShapeDtypeStruct((length, dil * LANES), F32))
        out_specs.append(pl.BlockSpec((tq, LANES), lambda r, i: (i, r)))
    kernel = functools.partial(
        _banded_kernel, tq=tq, hw=hw, n_kv=n_kv, group=group, length=length, q_axis=1,
        has_sink=sink_rep is not None, has_lse=has_lse)
    outs = pl.pallas_call(
        kernel, grid=(dil, nb), in_specs=in_specs, out_specs=out_specs, out_shape=out_shape,
        compiler_params=_params("parallel", "parallel"), name=name,
    )(*args)
    o = outs[0].reshape(SEQ, qw)
    if has_lse:
        return o, outs[1].reshape(SEQ, LANES)
    return o


def _flash_kernel(q_ref, k_ref, v_ref, o_ref, m_ref, acc_ref, *, tk):
    tq = q_ref.shape[0]
    n_chunks = k_ref.shape[0] // tk
    n_col = tk // LANES
    m_ref[...] = jnp.full((tq, LANES), NEG, F32)
    acc_ref[...] = jnp.zeros((tq, B_HEAD_PAD), F32)
    q = q_ref[...]

    for c in range(n_chunks):
        k = k_ref[c * tk:(c + 1) * tk, :]
        v = v_ref[c * tk:(c + 1) * tk, :]
        s = lax.dot_general(q, k, (((1,), (1,)), ((), ())), preferred_element_type=F32)
        mx = s[:, :LANES]
        for cb in range(1, n_col):
            mx = jnp.maximum(mx, s[:, cb * LANES:(cb + 1) * LANES])
        m_old = m_ref[...]
        m_new = jnp.maximum(m_old, jnp.max(mx, axis=-1, keepdims=True))
        alpha = jnp.exp2(m_old - m_new)
        p = jnp.exp2(s - jnp.concatenate([m_new] * n_col, axis=1)).astype(BF16)
        pv = jnp.dot(p, v, preferred_element_type=F32)
        acc_ref[...] = acc_ref[...] * jnp.concatenate([alpha, alpha], axis=1) + pv
        m_ref[...] = m_new

    acc = acc_ref[...]
    o_ref[...] = (acc[:, :B_NOPE] / acc[:, B_NOPE:]).astype(o_ref.dtype)


def _dense_attention(q, k, v_ext):
    tq, tk = 1024, 1024
    kv_spec = pl.BlockSpec((SEQ, B_HEAD_PAD), lambda h, i: (0, h))
    return pl.pallas_call(
        functools.partial(_flash_kernel, tk=tk),
        grid=(B_HEADS, SEQ // tq),
        in_specs=[pl.BlockSpec((tq, B_HEAD_PAD), lambda h, i: (i, h)), kv_spec, kv_spec],
        out_specs=pl.BlockSpec((tq, B_NOPE), lambda h, i: (i, h)),
        out_shape=jax.ShapeDtypeStruct((SEQ, B_HEADS * B_NOPE), BF16),
        scratch_shapes=[pltpu.VMEM((tq, LANES), F32), pltpu.VMEM((tq, B_HEAD_PAD), F32)],
        compiler_params=_params("parallel", "arbitrary"),
        name="b_flash_attention",
    )(q, k, v_ext)


def _out_proj_kernel(o_ref, w_ref, h_ref, out_ref):
    out_ref[...] = h_ref[...] + jnp.dot(o_ref[...], w_ref[...], preferred_element_type=F32)


def _out_projection(o, w, slot, h, name):
    tm, tn = 1024, 1024
    return pl.pallas_call(
        _out_proj_kernel,
        grid=(SEQ // tm, D_MODEL // tn),
        in_specs=[pl.BlockSpec((tm, D_MODEL), lambda i, j: (i, 0)),
                  pl.BlockSpec((None, D_MODEL, tn), lambda i, j: (slot, 0, j)),
                  pl.BlockSpec((tm, tn), lambda i, j: (i, j))],
        out_specs=pl.BlockSpec((tm, tn), lambda i, j: (i, j)),
        out_shape=jax.ShapeDtypeStruct((SEQ, D_MODEL), F32),
        compiler_params=_params("parallel", "parallel"),
        name=name,
    )(o, w, h)


def _merge_out_proj_kernel(o0_ref, o1_ref, o2_ref, l0_ref, l1_ref, l2_ref, w_ref, h_ref,
                           out_ref, merged_ref):
    @pl.when(pl.program_id(1) == 0)
    def _():
        l0, l1, l2 = l0_ref[...], l1_ref[...], l2_ref[...]
        m = jnp.maximum(jnp.maximum(l0, l1), l2)
        e0, e1, e2 = jnp.exp(l0 - m), jnp.exp(l1 - m), jnp.exp(l2 - m)
        inv = 1.0 / (e0 + e1 + e2)
        w0, w1, w2 = e0 * inv, e1 * inv, e2 * inv
        for hd in range(C_HEADS):
            cols = slice(hd * HEAD_DIM, (hd + 1) * HEAD_DIM)
            merged = (w0[:, hd:hd + 1] * o0_ref[:, cols].astype(F32)
                      + w1[:, hd:hd + 1] * o1_ref[:, cols].astype(F32)
                      + w2[:, hd:hd + 1] * o2_ref[:, cols].astype(F32))
            merged_ref[:, cols] = merged.astype(BF16)

    out_ref[...] = h_ref[...] + jnp.dot(merged_ref[...], w_ref[...], preferred_element_type=F32)


def _merge_out_projection(outs, lses, w, slot, h):
    tm, tn = 512, 1024
    o_spec = pl.BlockSpec((tm, D_MODEL), lambda i, j: (i, 0))
    l_spec = pl.BlockSpec((tm, LANES), lambda i, j: (i, 0))
    return pl.pallas_call(
        _merge_out_proj_kernel,
        grid=(SEQ // tm, D_MODEL // tn),
        in_specs=[o_spec] * 3 + [l_spec] * 3 + [
            pl.BlockSpec((None, D_MODEL, tn), lambda i, j: (slot, 0, j)),
            pl.BlockSpec((tm, tn), lambda i, j: (i, j))],
        out_specs=pl.BlockSpec((tm, tn), lambda i, j: (i, j)),
        out_shape=jax.ShapeDtypeStruct((SEQ, D_MODEL), F32),
        scratch_shapes=[pltpu.VMEM((tm, D_MODEL), BF16)],
        compiler_params=_params("parallel", "arbitrary"),
        name="c_merge_out_proj",
    )(*outs, *lses, w, h)


def _ffn_kernel(x_ref, g_ref, wg_ref, wu_ref, wd_ref, o_ref, xn_ref):
    @pl.when(pl.program_id(1) == 0)
    def _():
        x = x_ref[...]
        xn_ref[...] = (_rms_scale(x, D_MODEL) * g_ref[...]).astype(BF16)
        o_ref[...] = x

    xn = xn_ref[...]
    gate = jnp.dot(xn, wg_ref[...], preferred_element_type=F32)
    up = jnp.dot(xn, wu_ref[...], preferred_element_type=F32)
    act = (gate * jax.nn.sigmoid(gate) * up).astype(BF16)
    o_ref[...] += jnp.dot(act, wd_ref[...], preferred_element_type=F32)


def _ffn(h, g, wg, wu, wd, layer):
    tm, tf = 1024, 512
    return pl.pallas_call(
        _ffn_kernel,
        grid=(SEQ // tm, D_FF // tf),
        in_specs=[pl.BlockSpec((tm, D_MODEL), lambda i, f: (i, 0)),
                  pl.BlockSpec((1, D_MODEL), lambda i, f: (0, 0)),
                  pl.BlockSpec((None, D_MODEL, tf), lambda i, f: (layer, 0, f)),
                  pl.BlockSpec((None, D_MODEL, tf), lambda i, f: (layer, 0, f)),
                  pl.BlockSpec((None, tf, D_MODEL), lambda i, f: (layer, f, 0))],
        out_specs=pl.BlockSpec((tm, D_MODEL), lambda i, f: (i, 0)),
        out_shape=jax.ShapeDtypeStruct((SEQ, D_MODEL), F32),
        scratch_shapes=[pltpu.VMEM((tm, D_MODEL), BF16)],
        compiler_params=_params("parallel", "arbitrary"),
        name="ffn_swiglu",
    )(h, g, wg, wu, wd)


def _ple_epilogue(acc, rows, extra, outs):
    p_ref, wp_ref, h_ref = extra
    proj = jnp.dot(p_ref[rows, :].astype(BF16), wp_ref[...], preferred_element_type=F32)
    outs[0][rows, :] = h_ref[rows, :] + jax.nn.sigmoid(acc) * proj


def _ple(h, g, w_gate, p, w_proj, layer):
    tm, tn = 1024, 512
    return _norm_matmul(
        h, 0, D_MODEL, g, w_gate, layer, 0, D_MODEL // tn, tm=tm, tn=tn,
        extra=(p, w_proj, h),
        extra_specs=[pl.BlockSpec((None, tm, PLE_DIM), lambda i, j: (layer, i, 0)),
                     pl.BlockSpec((None, PLE_DIM, tn), lambda i, j: (layer, 0, j)),
                     pl.BlockSpec((tm, tn), lambda i, j: (i, j))],
        epilogue=_ple_epilogue,
        out_shape=jax.ShapeDtypeStruct((SEQ, D_MODEL), F32),
        out_specs=pl.BlockSpec((tm, tn), lambda i, j: (i, j)),
        name="ple_gate")


def _mixer_a(h, g_mix, w_in, slot, gq, gk, sink, w_o, tables):
    nq, nk = A_HEADS * HEAD_DIM, A_KV_HEADS * HEAD_DIM
    scale = 1.0 / math.sqrt(HEAD_DIM)
    head_gain = jnp.concatenate([jnp.tile(gq * scale, A_HEADS), jnp.tile(gk, A_KV_HEADS)])[None, :]
    qk, v = _qkv_projection(h, g_mix, w_in, slot, nq + nk, nk, head_gain, tables, "a")
    sink_rep = jnp.broadcast_to(jnp.repeat(sink, 128)[:, None], (A_HEADS * 128, LANES))
    o = _banded_attention(
        (qk, 0), (qk, nq // nk), (v, 0), dil=1, n_q=A_HEADS, n_kv=A_KV_HEADS, hw=A_HALF_WINDOW,
        sink_rep=sink_rep, has_lse=False, name="a_banded_attention")
    return _out_projection(o, w_o, slot, h, "a_out_proj")


def _mixer_b(h, g_mix, w_in, g_qlat, g_kvlat, w_q_up, w_kv_up, slot, gq, gk, w_o, tables):
    scale = math.log2(math.e) / math.sqrt(B_QK)
    half = B_ROPE // 2
    n_lat = B_Q_RANK + B_KV_RANK
    w_in = w_in[slot]
    w_in_pad = jnp.concatenate([w_in[:, :n_lat], _rotary_slab(w_in[:, n_lat:], half)], axis=1)[None]
    tm = 1024
    lat = _norm_matmul(
        h, 0, D_MODEL, g_mix, w_in_pad, 0, 0, 1, tm=tm, tn=B_IN_PAD, extra=(), extra_specs=[],
        epilogue=_plain_epilogue,
        out_shape=jax.ShapeDtypeStruct((SEQ, B_IN_PAD), F32),
        out_specs=pl.BlockSpec((tm, B_IN_PAD), lambda i, j: (i, j)),
        name="b_latent_proj")

    tab = pl.BlockSpec((tm, LANES), lambda i, j: (i, 0))
    tn = 1024
    n_slab = B_HEADS * B_HEAD_PAD

    def head_slab(x):
        return jnp.concatenate([x[..., :B_NOPE], _rotary_slab(x[..., B_NOPE:], half)], axis=-1)

    wq = head_slab(w_q_up[slot].reshape(B_Q_RANK, B_HEADS, B_QK)).reshape(B_Q_RANK, n_slab)[None]
    gq_slab = jnp.tile(head_slab(gq * scale), B_HEADS)[None, :]
    q = _norm_matmul(
        lat, 0, B_Q_RANK, g_qlat, wq, 0, 0, n_slab // tn, tm=tm, tn=tn,
        extra=(gq_slab,) + tuple(tables),
        extra_specs=[pl.BlockSpec((1, tn), lambda i, j: (0, j)), tab, tab],
        epilogue=_b_q_epilogue,
        out_shape=jax.ShapeDtypeStruct((SEQ, n_slab), BF16),
        out_specs=pl.BlockSpec((tm, tn), lambda i, j: (i, j)),
        name="b_q_proj")

    gk_slab = head_slab(gk)[None, :]
    slab_out = pl.BlockSpec((tm, tn), lambda i, j: (i, j))
    k, v_ext = _norm_matmul(
        lat, 1, B_KV_RANK, g_kvlat, w_kv_up, slot, 0, n_slab // tn, tm=tm, tn=tn,
        extra=(gk_slab, lat) + tuple(tables),
        extra_specs=[pl.BlockSpec((1, B_HEAD_PAD), lambda i, j: (0, 0)),
                     pl.BlockSpec((tm, LANES), lambda i, j: (i, n_lat // LANES)),
                     tab, tab],
        epilogue=_b_kv_epilogue,
        out_shape=[jax.ShapeDtypeStruct((SEQ, n_slab), BF16)] * 2,
        out_specs=[slab_out, slab_out],
        name="b_kv_proj")
    o = _dense_attention(q, k, v_ext)
    return _out_projection(o, w_o, slot, h, "b_out_proj")


def _mixer_c(h, g_mix, w_in, slot, gq, gk, w_o, tables):
    n_qh = C_GROUPS * C_HEADS
    nq = n_qh * HEAD_DIM
    nkv = C_HEADS * HEAD_DIM
    scale = 1.0 / math.sqrt(HEAD_DIM)
    head_gain = jnp.concatenate([jnp.tile(gq * scale, n_qh), jnp.tile(gk, C_HEADS)])[None, :]
    qk, v = _qkv_projection(h, g_mix, w_in, slot, nq + nkv, nkv, head_gain, tables, "c")
    outs, lses = [], []
    for g, (window, dil) in enumerate(C_PATTERNS):
        o, lse = _banded_attention(
            (qk, g), (qk, C_GROUPS), (v, 0), dil=dil, n_q=C_HEADS, n_kv=C_HEADS,
            hw=window // 2 // dil, sink_rep=None, has_lse=True,
            name=f"c_banded_attention_g{g}")
        outs.append(o)
        lses.append(lse)
    return _merge_out_projection(outs, lses, w_o, slot, h)


def kernel(x, p, positions, g_mix, g_ffn, g_ple, w_ple_gate, w_ple_proj,
           w_ffn_gate, w_ffn_up, w_ffn_down,
           a_w_in, a_q_norm, a_k_norm, a_sink, a_w_o,
           b_w_in, b_q_lat_norm, b_kv_lat_norm, b_w_q_up, b_w_kv_up, b_q_norm, b_k_norm, b_w_o,
           c_w_in, c_q_norm, c_k_norm, c_w_o):
    h = x.reshape(SEQ, D_MODEL)
    p = p.reshape(DEPTH, SEQ, PLE_DIM)
    pos_col = positions.reshape(SEQ, 1)
    tables_ac = _tables_adjacent(pos_col, PARTIAL_ROT)
    tables_b = _tables_split(pos_col, B_ROPE)
    (w_ple_gate, w_ple_proj, w_ffn_gate, w_ffn_up, w_ffn_down, a_w_in, a_w_o, b_w_in, b_w_q_up,
     b_w_kv_up, b_w_o, c_w_in, c_w_o) = [
        w.astype(BF16) for w in (w_ple_gate, w_ple_proj, w_ffn_gate, w_ffn_up, w_ffn_down, a_w_in,
                                 a_w_o, b_w_in, b_w_q_up, b_w_kv_up, b_w_o, c_w_in, c_w_o)]
    for i in range(DEPTH):
        kind, slot = i % 3, i // 3
        gm = g_mix[i][None, :]
        if kind == 0:
            h = _mixer_a(h, gm, a_w_in, slot, a_q_norm[slot], a_k_norm[slot], a_sink[slot],
                         a_w_o, tables_ac)
        elif kind == 1:
            h = _mixer_b(h, gm, b_w_in, b_q_lat_norm[slot][None, :], b_kv_lat_norm[slot][None, :],
                         b_w_q_up, b_w_kv_up, slot, b_q_norm[slot], b_k_norm[slot], b_w_o,
                         tables_b)
        else:
            h = _mixer_c(h, gm, c_w_in, slot, c_q_norm[slot], c_k_norm[slot], c_w_o, tables_ac)
        h = _ffn(h, g_ffn[i][None, :], w_ffn_gate, w_ffn_up, w_ffn_down, i)
        h = _ple(h, g_ple[i][None, :], w_ple_gate, p, w_ple_proj, i)
    return h.reshape(1, SEQ, D_MODEL)
```

```python
import functools
import math

import jax
import jax.numpy as jnp
from jax import lax
from jax.experimental import pallas as pl
from jax.experimental.pallas import tpu as pltpu

F32 = jnp.float32
BF16 = jnp.bfloat16

SEQ = 8192
D_MODEL = 2048
DEPTH = 4
HEAD_DIM = 128
ROPE_THETA = 500000.0
PARTIAL_ROT = HEAD_DIM // 4
NORM_EPS = 1e-6
NEG = -1e30
LANES = 128
HALF_LANES = LANES // 2

A_HEADS = 16
A_KV_HEADS = 4
A_HALF_WINDOW = 128
B_HEADS = 16
B_Q_RANK = 512
B_KV_RANK = 512
B_NOPE = 128
B_ROPE = 64
B_QK = B_NOPE + B_ROPE
B_HEAD_PAD = 256
B_IN_PAD = B_Q_RANK + B_KV_RANK + LANES
C_PATTERNS = ((128, 1), (512, 4), (2048, 16))
C_GROUPS = 3
C_HEADS = 16
C_HALF = 64
assert all(window // 2 // dil == C_HALF for window, dil in C_PATTERNS)
HOP = 4
assert tuple(dil for _, dil in C_PATTERNS) == (1, HOP, HOP * HOP)
D_FF = 5632
PLE_DIM = 256

VMEM_LIMIT = 56 * 1024 * 1024


def _params(*sem):
    return pltpu.CompilerParams(dimension_semantics=sem, vmem_limit_bytes=VMEM_LIMIT)


def _rms_scale(x, width):
    ss = jnp.sum(x * x, axis=-1, keepdims=True)
    return x * lax.rsqrt(ss * (1.0 / width) + NORM_EPS)


def _rope_adjacent(y, cos, sin_lo, sin_hi, half):
    return (y * cos + pltpu.roll(y, LANES - half, 1) * sin_lo
            + pltpu.roll(y, half, 1) * sin_hi)


def _rope_split(y, cos, sin):
    return y * cos + pltpu.roll(y, HALF_LANES, 1) * sin


def _rotary_slab(x, half):
    pad = jnp.zeros(x.shape[:-1] + (HALF_LANES - half,), x.dtype)
    return jnp.concatenate([x[..., :half], pad, x[..., half:], pad], axis=-1)


def _rope_table_kernel(pos_ref, inv_ref, sign_ref, cos_ref, *sin_refs):
    ang = pos_ref[...].astype(F32) * inv_ref[...]
    cos_ref[...] = jnp.cos(ang)
    s = jnp.sin(ang)
    for r, sin_ref in enumerate(sin_refs):
        sin_ref[...] = s * sign_ref[r:r + 1, :]


def _rope_tables(pos_col, inv_lanes, signs):
    tm = 1024
    n = signs.shape[0]
    tab = pl.BlockSpec((tm, LANES), lambda i: (i, 0))
    return pl.pallas_call(
        _rope_table_kernel,
        grid=(SEQ // tm,),
        in_specs=[pl.BlockSpec((tm, 1), lambda i: (i, 0)),
                  pl.BlockSpec((1, LANES), lambda i: (0, 0)),
                  pl.BlockSpec((n, LANES), lambda i: (0, 0))],
        out_specs=[tab] * (n + 1),
        out_shape=[jax.ShapeDtypeStruct((SEQ, LANES), F32)] * (n + 1),
        compiler_params=_params("parallel"),
        name="rope_tables",
    )(pos_col, inv_lanes, signs)


def _inv_freq(rot_dim):
    half = rot_dim // 2
    return ROPE_THETA ** (-jnp.arange(half, dtype=F32) * 2.0 / rot_dim)


def _tables_adjacent(pos_col, rot_dim):
    half = rot_dim // 2
    inv = _inv_freq(rot_dim)
    rest = jnp.zeros((LANES - rot_dim,), F32)
    zero, one = jnp.zeros((half,), F32), jnp.ones((half,), F32)
    inv_l = jnp.concatenate([inv, inv, rest])[None, :]
    signs = jnp.stack([jnp.concatenate([-one, zero, rest]), jnp.concatenate([zero, one, rest])])
    return _rope_tables(pos_col, inv_l, signs)


def _tables_split(pos_col, rot_dim):
    half = rot_dim // 2
    inv = _inv_freq(rot_dim)
    one = jnp.ones((half,), F32)
    inv_l = _rotary_slab(jnp.concatenate([inv, inv]), half)[None, :]
    signs = _rotary_slab(jnp.concatenate([-one, one]), half)[None, :]
    return _rope_tables(pos_col, inv_l, signs)


def _norm_matmul_kernel(x_ref, g_ref, w_ref, *rest, n_extra, epilogue, n_sub):
    extra = rest[:n_extra]
    outs = rest[n_extra:-1]
    xn_ref = rest[-1]

    @pl.when(pl.program_id(1) == 0)
    def _():
        x = x_ref[...]
        xn_ref[...] = (_rms_scale(x, x.shape[-1]) * g_ref[...]).astype(BF16)

    sub = xn_ref.shape[0] // n_sub
    for c in range(n_sub):
        rows = slice(c * sub, (c + 1) * sub)
        acc = jnp.dot(xn_ref[rows, :], w_ref[...], preferred_element_type=F32)
        epilogue(acc, rows, extra, outs)


def _norm_matmul(x, x_col, kin, gain, w, w_slot, w_col0, n_tiles, *, tm, tn, extra, extra_specs,
                 epilogue, out_shape, out_specs, name, n_sub=4):
    m = x.shape[0]
    kernel = functools.partial(_norm_matmul_kernel, n_extra=len(extra), epilogue=epilogue,
                               n_sub=n_sub)
    return pl.pallas_call(
        kernel,
        grid=(m // tm, n_tiles),
        in_specs=[pl.BlockSpec((tm, kin), lambda i, j: (i, x_col)),
                  pl.BlockSpec((1, kin), lambda i, j: (0, 0)),
                  pl.BlockSpec((None, kin, tn), lambda i, j: (w_slot, 0, w_col0 + j))]
        + list(extra_specs),
        out_specs=out_specs,
        out_shape=out_shape,
        scratch_shapes=[pltpu.VMEM((tm, kin), BF16)],
        compiler_params=_params("parallel", "arbitrary"),
        name=name,
    )(x, gain, w, *extra)


def _plain_epilogue(acc, rows, extra, outs):
    outs[0][rows, :] = acc.astype(outs[0].dtype)


def _head_norm_rope_epilogue(acc, rows, extra, outs):
    gain_ref, cos_ref, slo_ref, shi_ref = extra
    (o_ref,) = outs
    cos, slo, shi = cos_ref[rows, :], slo_ref[rows, :], shi_ref[rows, :]
    for c in range(acc.shape[1] // HEAD_DIM):
        cols = slice(c * HEAD_DIM, (c + 1) * HEAD_DIM)
        y = _rms_scale(acc[:, cols], HEAD_DIM) * gain_ref[:, cols]
        o_ref[rows, cols] = _rope_adjacent(y, cos, slo, shi, PARTIAL_ROT // 2).astype(o_ref.dtype)


def _qkv_projection(h, g_mix, w, slot, n_qk, n_v, head_gain, tables, name, dtype=BF16):
    tm, tn = 1024, 512
    tab = pl.BlockSpec((tm, LANES), lambda i, j: (i, 0))
    qk = _norm_matmul(
        h, 0, D_MODEL, g_mix, w, slot, 0, n_qk // tn, tm=tm, tn=tn,
        extra=(head_gain,) + tuple(tables),
        extra_specs=[pl.BlockSpec((1, tn), lambda i, j: (0, j)), tab, tab, tab],
        epilogue=_head_norm_rope_epilogue,
        out_shape=jax.ShapeDtypeStruct((SEQ, n_qk), dtype),
        out_specs=pl.BlockSpec((tm, tn), lambda i, j: (i, j)),
        name=name + "_qk_proj")
    v = _norm_matmul(
        h, 0, D_MODEL, g_mix, w, slot, n_qk // tn, n_v // tn, tm=tm, tn=tn,
        extra=(), extra_specs=[], epilogue=_plain_epilogue,
        out_shape=jax.ShapeDtypeStruct((SEQ, n_v), dtype),
        out_specs=pl.BlockSpec((tm, tn), lambda i, j: (i, j)),
        name=name + "_v_proj")
    return qk, v


def _b_q_epilogue(acc, rows, extra, outs):
    gain_ref, cos_ref, sin_ref = extra
    (o_ref,) = outs
    cos, sin = cos_ref[rows, :], sin_ref[rows, :]
    for c in range(acc.shape[1] // B_HEAD_PAD):
        c0 = c * B_HEAD_PAD
        y = _rms_scale(acc[:, c0:c0 + B_HEAD_PAD], B_QK) * gain_ref[:, c0:c0 + B_HEAD_PAD]
        o_ref[rows, c0:c0 + B_NOPE] = y[:, :B_NOPE].astype(BF16)
        o_ref[rows, c0 + B_NOPE:c0 + B_HEAD_PAD] = _rope_split(y[:, B_NOPE:], cos, sin).astype(BF16)


def _b_kv_epilogue(acc, rows, extra, outs):
    gain_ref, krope_ref, cos_ref, sin_ref = extra
    k_ref, v_ref = outs
    cos, sin = cos_ref[rows, :], sin_ref[rows, :]
    kr = krope_ref[rows, :]
    kr_ss = jnp.sum(kr * kr, axis=-1, keepdims=True)
    g_nope = gain_ref[:, :B_NOPE]
    g_rope = gain_ref[:, B_NOPE:]
    ones_blk = jnp.ones((acc.shape[0], LANES), BF16)
    for c in range(acc.shape[1] // B_HEAD_PAD):
        c0 = c * B_HEAD_PAD
        y = acc[:, c0:c0 + B_NOPE]
        ss = jnp.sum(y * y, axis=-1, keepdims=True) + kr_ss
        rinv = lax.rsqrt(ss * (1.0 / B_QK) + NORM_EPS)
        k_ref[rows, c0:c0 + B_NOPE] = (y * rinv * g_nope).astype(BF16)
        k_ref[rows, c0 + B_NOPE:c0 + B_HEAD_PAD] = _rope_split(
            kr * rinv * g_rope, cos, sin).astype(BF16)
        v_ref[rows, c0:c0 + B_NOPE] = acc[:, c0 + B_NOPE:c0 + B_HEAD_PAD].astype(BF16)
        v_ref[rows, c0 + B_NOPE:c0 + B_HEAD_PAD] = ones_blk


def _banded_kernel(*refs, tq, hw, n_kv, group, length, q_axis, has_sink, has_lse):
    q_ref, kp_ref, kc_ref, kn_ref, vp_ref, vc_ref, vn_ref, band_ref = refs[:8]
    nxt = 8
    sink_ref = None
    if has_sink:
        sink_ref = refs[nxt]
        nxt += 1
    o_ref = refs[nxt]
    lse_ref = refs[nxt + 1] if has_lse else None

    i = pl.program_id(q_axis)
    win = tq + 2 * hw
    n_col = win // LANES
    kpos = i * tq - hw + lax.broadcasted_iota(jnp.int32, (1, win), 1)
    edge = jnp.where((kpos >= 0) & (kpos < length), 0.0, NEG)
    bias = band_ref[...] + edge
    if group > 1:
        bias = jnp.concatenate([bias] * group, axis=0)
    ones_blk = jnp.ones((win, LANES), BF16)
    lane = lax.broadcasted_iota(jnp.int32, (tq, LANES), 1)
    lse_tile = jnp.zeros((tq, LANES), F32)

    for kv in range(n_kv):
        kcols = slice(kv * HEAD_DIM, (kv + 1) * HEAD_DIM)
        k_win = jnp.concatenate(
            [kp_ref[tq - hw:, kcols], kc_ref[:, kcols], kn_ref[:hw, kcols]], axis=0)
        v_win = jnp.concatenate(
            [vp_ref[tq - hw:, kcols], vc_ref[:, kcols], vn_ref[:hw, kcols]], axis=0)
        v_ext = jnp.concatenate([v_win, ones_blk], axis=1)
        heads = [kv * group + c for c in range(group)]
        q = jnp.concatenate(
            [q_ref[:, hd * HEAD_DIM:(hd + 1) * HEAD_DIM] for hd in heads], axis=0)
        s = lax.dot_general(q, k_win, (((1,), (1,)), ((), ())), preferred_element_type=F32) + bias
        mx = s[:, :LANES]
        for cb in range(1, n_col):
            mx = jnp.maximum(mx, s[:, cb * LANES:(cb + 1) * LANES])
        m = jnp.broadcast_to(jnp.max(mx, axis=-1, keepdims=True), mx.shape)
        if has_sink:
            sink = sink_ref[kv * group * tq:(kv + 1) * group * tq, :]
            m = jnp.maximum(m, sink)
        p = jnp.exp(s - jnp.concatenate([m] * n_col, axis=1)).astype(BF16)
        o_ext = jnp.dot(p, v_ext, preferred_element_type=F32)
        denom = o_ext[:, HEAD_DIM:]
        if has_sink:
            denom = denom + jnp.exp(sink - m)
        o = o_ext[:, :HEAD_DIM] / denom
        for c, hd in enumerate(heads):
            o_ref[:, hd * HEAD_DIM:(hd + 1) * HEAD_DIM] = o[c * tq:(c + 1) * tq].astype(o_ref.dtype)
        if has_lse:
            lse_tile = jnp.where(lane == kv, m + jnp.log(denom), lse_tile)
    if has_lse:
        lse_ref[...] = lse_tile


def _banded_attention(q_src, k_src, v_src, *, dil, n_q, n_kv, hw, sink_rep, has_lse, name):
    tq = 128
    length = SEQ // dil
    nb = length // tq
    group = n_q // n_kv
    qw, kw = n_q * HEAD_DIM, n_kv * HEAD_DIM

    def chain_view(src):
        arr, _ = src
        return arr.reshape(length, dil * arr.shape[1])

    def spec(src, width, shift):
        arr, col = src
        per_row = arr.shape[1] // width
        return pl.BlockSpec(
            (tq, width), lambda r, i: (jnp.clip(i + shift, 0, nb - 1), r * per_row + col))

    win = tq + 2 * hw
    rows = jnp.arange(tq, dtype=jnp.int32)[:, None]
    cols = jnp.arange(win, dtype=jnp.int32)[None, :]
    band = jnp.where(jnp.abs(rows + hw - cols) <= hw, 0.0, NEG).astype(F32)

    in_specs = [spec(q_src, qw, 0),
                spec(k_src, kw, -1), spec(k_src, kw, 0), spec(k_src, kw, 1),
                spec(v_src, kw, -1), spec(v_src, kw, 0), spec(v_src, kw, 1),
                pl.BlockSpec((tq, win), lambda r, i: (0, 0))]
    args = [chain_view(q_src)] + [chain_view(k_src)] * 3 + [chain_view(v_src)] * 3 + [band]
    if sink_rep is not None:
        in_specs.append(pl.BlockSpec(sink_rep.shape, lambda r, i: (0, 0)))
        args.append(sink_rep)
    out_shape = [jax.ShapeDtypeStruct((length, dil * qw), BF16)]
    out_specs = [pl.BlockSpec((tq, qw), lambda r, i: (i, r))]
    if has_lse:
        out_shape.append(jax.ShapeDtypeStruct((length, dil * LANES), F32))
        out_specs.append(pl.BlockSpec((tq, LANES), lambda r, i: (i, r)))
    kernel = functools.partial(
        _banded_kernel, tq=tq, hw=hw, n_kv=n_kv, group=group, length=length, q_axis=1,
        has_sink=sink_rep is not None, has_lse=has_lse)
    outs = pl.pallas_call(
        kernel, grid=(dil, nb), in_specs=in_specs, out_specs=out_specs, out_shape=out_shape,
        compiler_params=_params("parallel", "parallel"), name=name,
    )(*args)
    o = outs[0].reshape(SEQ, qw)
    if has_lse:
        return o, outs[1].reshape(SEQ, LANES)
    return o


def _flash_kernel(q_ref, k_ref, v_ref, o_ref, m_ref, acc_ref, *, tk):
    tq = q_ref.shape[0]
    n_chunks = k_ref.shape[0] // tk
    n_col = tk // LANES
    m_ref[...] = jnp.full((tq, LANES), NEG, F32)
    acc_ref[...] = jnp.zeros((tq, B_HEAD_PAD), F32)
    q = q_ref[...]

    for c in range(n_chunks):
        k = k_ref[c * tk:(c + 1) * tk, :]
        v = v_ref[c * tk:(c + 1) * tk, :]
        s = lax.dot_general(q, k, (((1,), (1,)), ((), ())), preferred_element_type=F32)
        mx = s[:, :LANES]
        for cb in range(1, n_col):
            mx = jnp.maximum(mx, s[:, cb * LANES:(cb + 1) * LANES])
        m_old = m_ref[...]
        m_new = jnp.maximum(m_old, jnp.max(mx, axis=-1, keepdims=True))
        alpha = jnp.exp2(m_old - m_new)
        p = jnp.exp2(s - jnp.concatenate([m_new] * n_col, axis=1)).astype(BF16)
        pv = jnp.dot(p, v, preferred_element_type=F32)
        acc_ref[...] = acc_ref[...] * jnp.concatenate([alpha, alpha], axis=1) + pv
        m_ref[...] = m_new

    acc = acc_ref[...]
    o_ref[...] = (acc[:, :B_NOPE] / acc[:, B_NOPE:]).astype(o_ref.dtype)


def _dense_attention(q, k, v_ext):
    tq, tk = 1024, 1024
    kv_spec = pl.BlockSpec((SEQ, B_HEAD_PAD), lambda h, i: (0, h))
    return pl.pallas_call(
        functools.partial(_flash_kernel, tk=tk),
        grid=(B_HEADS, SEQ // tq),
        in_specs=[pl.BlockSpec((tq, B_HEAD_PAD), lambda h, i: (i, h)), kv_spec, kv_spec],
        out_specs=pl.BlockSpec((tq, B_NOPE), lambda h, i: (i, h)),
        out_shape=jax.ShapeDtypeStruct((SEQ, B_HEADS * B_NOPE), BF16),
        scratch_shapes=[pltpu.VMEM((tq, LANES), F32), pltpu.VMEM((tq, B_HEAD_PAD), F32)],
        compiler_params=_params("parallel", "arbitrary"),
        name="b_flash_attention",
    )(q, k, v_ext)


def _out_proj_kernel(o_ref, w_ref, h_ref, out_ref):
    out_ref[...] = h_ref[...] + jnp.dot(o_ref[...], w_ref[...], preferred_element_type=F32)


def _out_projection(o, w, slot, h, name):
    tm, tn = 1024, 1024
    return pl.pallas_call(
        _out_proj_kernel,
        grid=(SEQ // tm, D_MODEL // tn),
        in_specs=[pl.BlockSpec((tm, D_MODEL), lambda i, j: (i, 0)),
                  pl.BlockSpec((None, D_MODEL, tn), lambda i, j: (slot, 0, j)),
                  pl.BlockSpec((tm, tn), lambda i, j: (i, j))],
        out_specs=pl.BlockSpec((tm, tn), lambda i, j: (i, j)),
        out_shape=jax.ShapeDtypeStruct((SEQ, D_MODEL), F32),
        compiler_params=_params("parallel", "parallel"),
        name=name,
    )(o, w, h)


def _ffn_kernel(x_ref, g_ref, wg_ref, wu_ref, wd_ref, o_ref, xn_ref):
    @pl.when(pl.program_id(1) == 0)
    def _():
        x = x_ref[...]
        xn_ref[...] = (_rms_scale(x, D_MODEL) * g_ref[...]).astype(BF16)
        o_ref[...] = x

    xn = xn_ref[...]
    gate = jnp.dot(xn, wg_ref[...], preferred_element_type=F32)
    up = jnp.dot(xn, wu_ref[...], preferred_element_type=F32)
    act = (gate * jax.nn.sigmoid(gate) * up).astype(BF16)
    o_ref[...] += jnp.dot(act, wd_ref[...], preferred_element_type=F32)


def _ffn(h, g, wg, wu, wd, layer):
    tm, tf = 1024, 512
    return pl.pallas_call(
        _ffn_kernel,
        grid=(SEQ // tm, D_FF // tf),
        in_specs=[pl.BlockSpec((tm, D_MODEL), lambda i, f: (i, 0)),
                  pl.BlockSpec((1, D_MODEL), lambda i, f: (0, 0)),
                  pl.BlockSpec((None, D_MODEL, tf), lambda i, f: (layer, 0, f)),
                  pl.BlockSpec((None, D_MODEL, tf), lambda i, f: (layer, 0, f)),
                  pl.BlockSpec((None, tf, D_MODEL), lambda i, f: (layer, f, 0))],
        out_specs=pl.BlockSpec((tm, D_MODEL), lambda i, f: (i, 0)),
        out_shape=jax.ShapeDtypeStruct((SEQ, D_MODEL), F32),
        scratch_shapes=[pltpu.VMEM((tm, D_MODEL), BF16)],
        compiler_params=_params("parallel", "arbitrary"),
        name="ffn_swiglu",
    )(h, g, wg, wu, wd)


def _ple_epilogue(acc, rows, extra, outs):
    p_ref, wp_ref, h_ref = extra
    proj = jnp.dot(p_ref[rows, :].astype(BF16), wp_ref[...], preferred_element_type=F32)
    outs[0][rows, :] = h_ref[rows, :] + jax.nn.sigmoid(acc) * proj


def _ple(h, g, w_gate, p, w_proj, layer):
    tm, tn = 1024, 512
    return _norm_matmul(
        h, 0, D_MODEL, g, w_gate, layer, 0, D_MODEL // tn, tm=tm, tn=tn,
        extra=(p, w_proj, h),
        extra_specs=[pl.BlockSpec((None, tm, PLE_DIM), lambda i, j: (layer, i, 0)),
                     pl.BlockSpec((None, PLE_DIM, tn), lambda i, j: (layer, 0, j)),
                     pl.BlockSpec((tm, tn), lambda i, j: (i, j))],
        epilogue=_ple_epilogue,
        out_shape=jax.ShapeDtypeStruct((SEQ, D_MODEL), F32),
        out_specs=pl.BlockSpec((tm, tn), lambda i, j: (i, j)),
        name="ple_gate")


def _mixer_a(h, g_mix, w_in, slot, gq, gk, sink, w_o, tables):
    nq, nk = A_HEADS * HEAD_DIM, A_KV_HEADS * HEAD_DIM
    scale = 1.0 / math.sqrt(HEAD_DIM)
    head_gain = jnp.concatenate([jnp.tile(gq * scale, A_HEADS), jnp.tile(gk, A_KV_HEADS)])[None, :]
    qk, v = _qkv_projection(h, g_mix, w_in, slot, nq + nk, nk, head_gain, tables, "a")
    sink_rep = jnp.broadcast_to(jnp.repeat(sink, 128)[:, None], (A_HEADS * 128, LANES))
    o = _banded_attention(
        (qk, 0), (qk, nq // nk), (v, 0), dil=1, n_q=A_HEADS, n_kv=A_KV_HEADS, hw=A_HALF_WINDOW,
        sink_rep=sink_rep, has_lse=False, name="a_banded_attention")
    return _out_projection(o, w_o, slot, h, "a_out_proj")


def _mixer_b(h, g_mix, w_in, g_qlat, g_kvlat, w_q_up, w_kv_up, slot, gq, gk, w_o, tables):
    scale = math.log2(math.e) / math.sqrt(B_QK)
    half = B_ROPE // 2
    n_lat = B_Q_RANK + B_KV_RANK
    w_in = w_in[slot]
    w_in_pad = jnp.concatenate([w_in[:, :n_lat], _rotary_slab(w_in[:, n_lat:], half)], axis=1)[None]
    tm = 1024
    lat = _norm_matmul(
        h, 0, D_MODEL, g_mix, w_in_pad, 0, 0, 1, tm=tm, tn=B_IN_PAD, extra=(), extra_specs=[],
        epilogue=_plain_epilogue,
        out_shape=jax.ShapeDtypeStruct((SEQ, B_IN_PAD), F32),
        out_specs=pl.BlockSpec((tm, B_IN_PAD), lambda i, j: (i, j)),
        name="b_latent_proj")

    tab = pl.BlockSpec((tm, LANES), lambda i, j: (i, 0))
    tn = 1024
    n_slab = B_HEADS * B_HEAD_PAD

    def head_slab(x):
        return jnp.concatenate([x[..., :B_NOPE], _rotary_slab(x[..., B_NOPE:], half)], axis=-1)

    wq = head_slab(w_q_up[slot].reshape(B_Q_RANK, B_HEADS, B_QK)).reshape(B_Q_RANK, n_slab)[None]
    gq_slab = jnp.tile(head_slab(gq * scale), B_HEADS)[None, :]
    q = _norm_matmul(
        lat, 0, B_Q_RANK, g_qlat, wq, 0, 0, n_slab // tn, tm=tm, tn=tn,
        extra=(gq_slab,) + tuple(tables),
        extra_specs=[pl.BlockSpec((1, tn), lambda i, j: (0, j)), tab, tab],
        epilogue=_b_q_epilogue,
        out_shape=jax.ShapeDtypeStruct((SEQ, n_slab), BF16),
        out_specs=pl.BlockSpec((tm, tn), lambda i, j: (i, j)),
        name="b_q_proj")

    gk_slab = head_slab(gk)[None, :]
    slab_out = pl.BlockSpec((tm, tn), lambda i, j: (i, j))
    k, v_ext = _norm_matmul(
        lat, 1, B_KV_RANK, g_kvlat, w_kv_up, slot, 0, n_slab // tn, tm=tm, tn=tn,
        extra=(gk_slab, lat) + tuple(tables),
        extra_specs=[pl.BlockSpec((1, B_HEAD_PAD), lambda i, j: (0, 0)),
                     pl.BlockSpec((tm, LANES), lambda i, j: (i, n_lat // LANES)),
                     tab, tab],
        epilogue=_b_kv_epilogue,
        out_shape=[jax.ShapeDtypeStruct((SEQ, n_slab), BF16)] * 2,
        out_specs=[slab_out, slab_out],
        name="b_kv_proj")
    o = _dense_attention(q, k, v_ext)
    return _out_projection(o, w_o, slot, h, "b_out_proj")


def _dilated_kernel(q0_ref, q1_ref, q2_ref, kp_ref, kc_ref, kn_ref, vp_ref, vc_ref, vn_ref,
                    band_ref, o_ref, *scratch):
    i = pl.program_id(0)
    blk = q0_ref.shape[0]
    band = band_ref[...]
    win = band.shape[1]
    ones_blk = jnp.ones((win, LANES), BF16)
    col = lax.broadcasted_iota(jnp.int32, (1, win), 1)
    n4, n16 = blk // HOP, blk // (HOP * HOP)
    scratch = list(scratch)
    take = lambda k: [scratch.pop(0) for _ in range(k)]
    o_s, l_s = take(C_GROUPS), take(C_GROUPS)
    hop1 = take(8)
    hop2 = take(7)
    o2_hop1, l2_hop1, merged_tok = take(3)

    for src, dst in zip((q1_ref, q2_ref, kp_ref, kc_ref, kn_ref, vp_ref, vc_ref, vn_ref), hop1):
        for r in range(HOP):
            dst[r * n4:(r + 1) * n4, :] = src[pl.ds(r, n4, stride=HOP), :]
    for src, dst in zip(hop1[1:], hop2):
        for r in range(HOP * HOP):
            dst[r * n16:(r + 1) * n16, :] = src[pl.ds((r % HOP) * n4 + r // HOP, n16, stride=HOP), :]
    operands = ((q0_ref, kp_ref, kc_ref, kn_ref, vp_ref, vc_ref, vn_ref),
                (hop1[0],) + tuple(hop1[2:]),
                tuple(hop2))

    def attend(q, k_win, v_win, bias):
        v_ext = jnp.concatenate([v_win, ones_blk], axis=1)
        s = lax.dot_general(q, k_win, (((1,), (1,)), ((), ())), preferred_element_type=F32) + bias
        mx = jnp.maximum(s[:, :LANES], s[:, LANES:])
        m = jnp.broadcast_to(jnp.max(mx, axis=-1, keepdims=True), mx.shape)
        p = jnp.exp(s - jnp.concatenate([m, m], axis=1)).astype(BF16)
        o_ext = jnp.dot(p, v_ext, preferred_element_type=F32)
        denom = o_ext[:, HEAD_DIM:]
        return o_ext[:, :HEAD_DIM] / denom, m + jnp.log(denom)

    for g, ((window, d), refs) in enumerate(zip(C_PATTERNS, operands)):
        q_ref, kprev, kcur, knext, vprev, vcur, vnext = refs
        n = blk // d
        rq = min(n, win - 2 * C_HALF)
        pad_rows = win - min(n + 2 * C_HALF, win)
        tiles = []
        for r in range(d):
            def chain_window(prev_ref, cur_ref, next_ref):
                parts = [prev_ref[(r + 1) * n - C_HALF:(r + 1) * n, :], cur_ref[r * n:(r + 1) * n, :],
                         next_ref[r * n:r * n + C_HALF, :]]
                if pad_rows:
                    parts.append(jnp.zeros((pad_rows, LANES), F32))
                return jnp.concatenate(parts, axis=0).astype(BF16)

            k_chain = chain_window(kprev, kcur, knext)
            v_chain = chain_window(vprev, vcur, vnext)
            for sb in range(n // rq):
                kpos = i * n + sb * rq - C_HALF + col
                edge = jnp.where((kpos >= 0) & (kpos < SEQ // d), 0.0, NEG)
                dst = slice(r * n + sb * rq, r * n + (sb + 1) * rq)
                tiles.append((dst, q_ref[dst, :].astype(BF16), k_chain[sb * rq:sb * rq + win],
                              v_chain[sb * rq:sb * rq + win], band[:rq] + edge))
        full = win - 2 * C_HALF
        batch = 1 if rq == full else 4 * full // rq
        for b0 in range(0, len(tiles), batch):
            group_tiles = tiles[b0:b0 + batch]
            scores = [lax.dot_general(q, k_win, (((1,), (1,)), ((), ())),
                                      preferred_element_type=F32) + bias
                      for _, q, k_win, _, bias in group_tiles]
            probs = []
            for s in scores:
                mx = jnp.maximum(s[:, :LANES], s[:, LANES:])
                m = jnp.broadcast_to(jnp.max(mx, axis=-1, keepdims=True), mx.shape)
                probs.append((m, jnp.exp(s - jnp.concatenate([m, m], axis=1)).astype(BF16)))
            for (dst, _, _, v_win, _), (m, p) in zip(group_tiles, probs):
                o_ext = jnp.dot(p, jnp.concatenate([v_win, ones_blk], axis=1),
                                preferred_element_type=F32)
                denom = o_ext[:, HEAD_DIM:]
                o_s[g][dst, :] = o_ext[:, :HEAD_DIM] / denom
                l_s[g][dst, :] = m + jnp.log(denom)

    for src, dst in ((o_s[2], o2_hop1), (l_s[2], l2_hop1)):
        for r in range(HOP * HOP):
            dst[pl.ds((r % HOP) * n4 + r // HOP, n16, stride=HOP), :] = src[r * n16:(r + 1) * n16, :]
    for r in range(HOP):
        rows = slice(r * n4, (r + 1) * n4)
        tok = pl.ds(r, n4, stride=HOP)
        l0, l1, l2 = l_s[0][tok, :], l_s[1][rows, :], l2_hop1[rows, :]
        m = jnp.maximum(jnp.maximum(l0, l1), l2)
        e0, e1, e2 = jnp.exp(l0 - m), jnp.exp(l1 - m), jnp.exp(l2 - m)
        merged = (e0 * o_s[0][tok, :] + e1 * o_s[1][rows, :] + e2 * o2_hop1[rows, :]) / (e0 + e1 + e2)
        merged_tok[tok, :] = merged
    o_ref[...] = merged_tok[...].astype(o_ref.dtype)


def _dilated_attention(qk, v):
    blk = 1024
    nb = SEQ // blk
    win = 2 * LANES
    rows = jnp.arange(LANES, dtype=jnp.int32)[:, None]
    cols = jnp.arange(win, dtype=jnp.int32)[None, :]
    band = jnp.where(jnp.abs(rows + C_HALF - cols) <= C_HALF, 0.0, NEG).astype(F32)

    def slab(col0, shift):
        return pl.BlockSpec((blk, HEAD_DIM),
                            lambda i, h: (jnp.clip(i + shift, 0, nb - 1), col0 + h))

    k_col = C_GROUPS * C_HEADS
    in_specs = [slab(g * C_HEADS, 0) for g in range(C_GROUPS)]
    in_specs += [slab(k_col, -1), slab(k_col, 0), slab(k_col, 1)]
    in_specs += [slab(0, -1), slab(0, 0), slab(0, 1)]
    in_specs.append(pl.BlockSpec((LANES, win), lambda i, h: (0, 0)))
    return pl.pallas_call(
        _dilated_kernel,
        grid=(nb, C_HEADS),
        in_specs=in_specs,
        out_specs=pl.BlockSpec((blk, HEAD_DIM), lambda i, h: (i, h)),
        out_shape=jax.ShapeDtypeStruct((SEQ, C_HEADS * HEAD_DIM), BF16),
        scratch_shapes=[pltpu.VMEM((blk, LANES), F32)] * (2 * C_GROUPS + 8 + 7 + 3),
        compiler_params=_params("parallel", "parallel"),
        name="c_dilated_attention",
    )(qk, qk, qk, qk, qk, qk, v, v, v, band)


def _mixer_c(h, g_mix, w_in, slot, gq, gk, w_o, tables):
    n_qh = C_GROUPS * C_HEADS
    nq = n_qh * HEAD_DIM
    nkv = C_HEADS * HEAD_DIM
    scale = 1.0 / math.sqrt(HEAD_DIM)
    head_gain = jnp.concatenate([jnp.tile(gq * scale, n_qh), jnp.tile(gk, C_HEADS)])[None, :]
    qk, v = _qkv_projection(h, g_mix, w_in, slot, nq + nkv, nkv, head_gain, tables, "c", dtype=F32)
    o = _dilated_attention(qk, v)
    return _out_projection(o, w_o, slot, h, "c_out_proj")


def kernel(x, p, positions, g_mix, g_ffn, g_ple, w_ple_gate, w_ple_proj,
           w_ffn_gate, w_ffn_up, w_ffn_down,
           a_w_in, a_q_norm, a_k_norm, a_sink, a_w_o,
           b_w_in, b_q_lat_norm, b_kv_lat_norm, b_w_q_up, b_w_kv_up, b_q_norm, b_k_norm, b_w_o,
           c_w_in, c_q_norm, c_k_norm, c_w_o):
    h = x.reshape(SEQ, D_MODEL)
    p = p.reshape(DEPTH, SEQ, PLE_DIM)
    pos_col = positions.reshape(SEQ, 1)
    tables_ac = _tables_adjacent(pos_col, PARTIAL_ROT)
    tables_b = _tables_split(pos_col, B_ROPE)
    (w_ple_gate, w_ple_proj, w_ffn_gate, w_ffn_up, w_ffn_down, a_w_in, a_w_o, b_w_in, b_w_q_up,
     b_w_kv_up, b_w_o, c_w_in, c_w_o) = [
        w.astype(BF16) for w in (w_ple_gate, w_ple_proj, w_ffn_gate, w_ffn_up, w_ffn_down, a_w_in,
                                 a_w_o, b_w_in, b_w_q_up, b_w_kv_up, b_w_o, c_w_in, c_w_o)]
    for i in range(DEPTH):
        kind, slot = i % 3, i // 3
        gm = g_mix[i][None, :]
        if kind == 0:
            h = _mixer_a(h, gm, a_w_in, slot, a_q_norm[slot], a_k_norm[slot], a_sink[slot],
                         a_w_o, tables_ac)
        elif kind == 1:
            h = _mixer_b(h, gm, b_w_in, b_q_lat_norm[slot][None, :], b_kv_lat_norm[slot][None, :],
                         b_w_q_up, b_w_kv_up, slot, b_q_norm[slot], b_k_norm[slot], b_w_o,
                         tables_b)
        else:
            h = _mixer_c(h, gm, c_w_in, slot, c_q_norm[slot], c_k_norm[slot], c_w_o, tables_ac)
        h = _ffn(h, g_ffn[i][None, :], w_ffn_gate, w_ffn_up, w_ffn_down, i)
        h = _ple(h, g_ple[i][None, :], w_ple_gate, p, w_ple_proj, i)
    return h.reshape(1, SEQ, D_MODEL)
```

```python
import functools
import math

import jax
import jax.numpy as jnp
from jax import lax
from jax.experimental import pallas as pl
from jax.experimental.pallas import tpu as pltpu

F32 = jnp.float32
BF16 = jnp.bfloat16

SEQ = 8192
D_MODEL = 2048
DEPTH = 4
HEAD_DIM = 128
ROPE_THETA = 500000.0
PARTIAL_ROT = HEAD_DIM // 4
NORM_EPS = 1e-6
NEG = -1e30
LANES = 128
HALF_LANES = LANES // 2

A_HEADS = 16
A_KV_HEADS = 4
A_HALF_WINDOW = 128
B_HEADS = 16
B_Q_RANK = 512
B_KV_RANK = 512
B_NOPE = 128
B_ROPE = 64
B_QK = B_NOPE + B_ROPE
B_HEAD_PAD = 256
B_IN_PAD = B_Q_RANK + B_KV_RANK + LANES
C_PATTERNS = ((128, 1), (512, 4), (2048, 16))
C_GROUPS = 3
C_HEADS = 16
C_HALF = 64
assert all(window // 2 // dil == C_HALF for window, dil in C_PATTERNS)
HOP = 4
assert tuple(dil for _, dil in C_PATTERNS) == (1, HOP, HOP * HOP)
D_FF = 5632
PLE_DIM = 256

VMEM_LIMIT = 56 * 1024 * 1024


def _params(*sem):
    return pltpu.CompilerParams(dimension_semantics=sem, vmem_limit_bytes=VMEM_LIMIT)


def _rms_scale(x, width):
    ss = jnp.sum(x * x, axis=-1, keepdims=True)
    return x * lax.rsqrt(ss * (1.0 / width) + NORM_EPS)


def _rope_adjacent(y, cos, sin_lo, sin_hi, half):
    return (y * cos + pltpu.roll(y, LANES - half, 1) * sin_lo
            + pltpu.roll(y, half, 1) * sin_hi)


def _rope_split(y, cos, sin):
    return y * cos + pltpu.roll(y, HALF_LANES, 1) * sin


def _rotary_slab(x, half):
    pad = jnp.zeros(x.shape[:-1] + (HALF_LANES - half,), x.dtype)
    return jnp.concatenate([x[..., :half], pad, x[..., half:], pad], axis=-1)


def _rope_table_kernel(pos_ref, inv_ref, sign_ref, cos_ref, *sin_refs):
    ang = pos_ref[...].astype(F32) * inv_ref[...]
    cos_ref[...] = jnp.cos(ang)
    s = jnp.sin(ang)
    for r, sin_ref in enumerate(sin_refs):
        sin_ref[...] = s * sign_ref[r:r + 1, :]


def _rope_tables(pos_col, inv_lanes, signs):
    tm = 1024
    n = signs.shape[0]
    tab = pl.BlockSpec((tm, LANES), lambda i: (i, 0))
    return pl.pallas_call(
        _rope_table_kernel,
        grid=(SEQ // tm,),
        in_specs=[pl.BlockSpec((tm, 1), lambda i: (i, 0)),
                  pl.BlockSpec((1, LANES), lambda i: (0, 0)),
                  pl.BlockSpec((n, LANES), lambda i: (0, 0))],
        out_specs=[tab] * (n + 1),
        out_shape=[jax.ShapeDtypeStruct((SEQ, LANES), F32)] * (n + 1),
        compiler_params=_params("parallel"),
        name="rope_tables",
    )(pos_col, inv_lanes, signs)


def _inv_freq(rot_dim):
    half = rot_dim // 2
    return ROPE_THETA ** (-jnp.arange(half, dtype=F32) * 2.0 / rot_dim)


def _tables_adjacent(pos_col, rot_dim):
    half = rot_dim // 2
    inv = _inv_freq(rot_dim)
    rest = jnp.zeros((LANES - rot_dim,), F32)
    zero, one = jnp.zeros((half,), F32), jnp.ones((half,), F32)
    inv_l = jnp.concatenate([inv, inv, rest])[None, :]
    signs = jnp.stack([jnp.concatenate([-one, zero, rest]), jnp.concatenate([zero, one, rest])])
    return _rope_tables(pos_col, inv_l, signs)


def _tables_split(pos_col, rot_dim):
    half = rot_dim // 2
    inv = _inv_freq(rot_dim)
    one = jnp.ones((half,), F32)
    inv_l = _rotary_slab(jnp.concatenate([inv, inv]), half)[None, :]
    signs = _rotary_slab(jnp.concatenate([-one, one]), half)[None, :]
    return _rope_tables(pos_col, inv_l, signs)


def _norm_matmul_kernel(x_ref, g_ref, w_ref, *rest, n_extra, epilogue, n_sub):
    extra = rest[:n_extra]
    outs = rest[n_extra:-1]
    xn_ref = rest[-1]

    sub = xn_ref.shape[0] // n_sub

    def column_tile(first):
        for c in range(n_sub):
            rows = slice(c * sub, (c + 1) * sub)
            if first:
                x = x_ref[rows, :]
                xn_ref[rows, :] = (_rms_scale(x, x.shape[-1]) * g_ref[...]).astype(BF16)
            acc = jnp.dot(xn_ref[rows, :], w_ref[...], preferred_element_type=F32)
            epilogue(acc, rows, extra, outs)

    @pl.when(pl.program_id(1) == 0)
    def _():
        column_tile(True)

    @pl.when(pl.program_id(1) > 0)
    def _():
        column_tile(False)


def _norm_matmul(x, x_col, kin, gain, w, w_slot, w_col0, n_tiles, *, tm, tn, extra, extra_specs,
                 epilogue, out_shape, out_specs, name, n_sub=4):
    m = x.shape[0]
    kernel = functools.partial(_norm_matmul_kernel, n_extra=len(extra), epilogue=epilogue,
                               n_sub=n_sub)
    return pl.pallas_call(
        kernel,
        grid=(m // tm, n_tiles),
        in_specs=[pl.BlockSpec((tm, kin), lambda i, j: (i, x_col)),
                  pl.BlockSpec((1, kin), lambda i, j: (0, 0)),
                  pl.BlockSpec((None, kin, tn), lambda i, j: (w_slot, 0, w_col0 + j))]
        + list(extra_specs),
        out_specs=out_specs,
        out_shape=out_shape,
        scratch_shapes=[pltpu.VMEM((tm, kin), BF16)],
        compiler_params=_params("parallel", "arbitrary"),
        name=name,
    )(x, gain, w, *extra)


def _plain_epilogue(acc, rows, extra, outs):
    outs[0][rows, :] = acc.astype(outs[0].dtype)


def _head_norm_rope_epilogue(acc, rows, extra, outs):
    gain_ref, cos_ref, slo_ref, shi_ref = extra
    (o_ref,) = outs
    cos, slo, shi = cos_ref[rows, :], slo_ref[rows, :], shi_ref[rows, :]
    for c in range(acc.shape[1] // HEAD_DIM):
        cols = slice(c * HEAD_DIM, (c + 1) * HEAD_DIM)
        y = _rms_scale(acc[:, cols], HEAD_DIM) * gain_ref[:, cols]
        o_ref[rows, cols] = _rope_adjacent(y, cos, slo, shi, PARTIAL_ROT // 2).astype(o_ref.dtype)


def _qkv_projection(h, g_mix, w, slot, n_qk, n_v, head_gain, tables, name, dtype=BF16):
    tm, tn = 1024, 512
    tab = pl.BlockSpec((tm, LANES), lambda i, j: (i, 0))
    qk = _norm_matmul(
        h, 0, D_MODEL, g_mix, w, slot, 0, n_qk // tn, tm=tm, tn=tn,
        extra=(head_gain,) + tuple(tables),
        extra_specs=[pl.BlockSpec((1, tn), lambda i, j: (0, j)), tab, tab, tab],
        epilogue=_head_norm_rope_epilogue,
        out_shape=jax.ShapeDtypeStruct((SEQ, n_qk), dtype),
        out_specs=pl.BlockSpec((tm, tn), lambda i, j: (i, j)),
        name=name + "_qk_proj")
    v = _norm_matmul(
        h, 0, D_MODEL, g_mix, w, slot, n_qk // tn, n_v // tn, tm=tm, tn=tn,
        extra=(), extra_specs=[], epilogue=_plain_epilogue,
        out_shape=jax.ShapeDtypeStruct((SEQ, n_v), dtype),
        out_specs=pl.BlockSpec((tm, tn), lambda i, j: (i, j)),
        name=name + "_v_proj")
    return qk, v


def _b_q_epilogue(acc, rows, extra, outs):
    gain_ref, cos_ref, sin_ref = extra
    (o_ref,) = outs
    cos, sin = cos_ref[rows, :], sin_ref[rows, :]
    for c in range(acc.shape[1] // B_HEAD_PAD):
        c0 = c * B_HEAD_PAD
        y = _rms_scale(acc[:, c0:c0 + B_HEAD_PAD], B_QK) * gain_ref[:, c0:c0 + B_HEAD_PAD]
        o_ref[rows, c0:c0 + B_NOPE] = y[:, :B_NOPE].astype(BF16)
        o_ref[rows, c0 + B_NOPE:c0 + B_HEAD_PAD] = _rope_split(y[:, B_NOPE:], cos, sin).astype(BF16)


def _b_kv_epilogue(acc, rows, extra, outs):
    gain_ref, krope_ref, cos_ref, sin_ref = extra
    k_ref, v_ref = outs
    cos, sin = cos_ref[rows, :], sin_ref[rows, :]
    kr = krope_ref[rows, :]
    kr_ss = jnp.sum(kr * kr, axis=-1, keepdims=True)
    g_nope = gain_ref[:, :B_NOPE]
    g_rope = gain_ref[:, B_NOPE:]
    ones_blk = jnp.ones((acc.shape[0], LANES), BF16)
    for c in range(acc.shape[1] // B_HEAD_PAD):
        c0 = c * B_HEAD_PAD
        y = acc[:, c0:c0 + B_NOPE]
        ss = jnp.sum(y * y, axis=-1, keepdims=True) + kr_ss
        rinv = lax.rsqrt(ss * (1.0 / B_QK) + NORM_EPS)
        k_ref[rows, c0:c0 + B_NOPE] = (y * rinv * g_nope).astype(BF16)
        k_ref[rows, c0 + B_NOPE:c0 + B_HEAD_PAD] = _rope_split(
            kr * rinv * g_rope, cos, sin).astype(BF16)
        v_ref[rows, c0:c0 + B_NOPE] = acc[:, c0 + B_NOPE:c0 + B_HEAD_PAD].astype(BF16)
        v_ref[rows, c0 + B_NOPE:c0 + B_HEAD_PAD] = ones_blk


def _banded_kernel(*refs, tq, hw, n_kv, group, length, q_axis, has_sink, has_lse):
    q_ref, kp_ref, kc_ref, kn_ref, vp_ref, vc_ref, vn_ref, band_ref = refs[:8]
    nxt = 8
    sink_ref = None
    if has_sink:
        sink_ref = refs[nxt]
        nxt += 1
    o_ref = refs[nxt]
    lse_ref = refs[nxt + 1] if has_lse else None

    i = pl.program_id(q_axis)
    win = tq + 2 * hw
    n_col = win // LANES
    kpos = i * tq - hw + lax.broadcasted_iota(jnp.int32, (1, win), 1)
    edge = jnp.where((kpos >= 0) & (kpos < length), 0.0, NEG)
    bias = band_ref[...] + edge
    if group > 1:
        bias = jnp.concatenate([bias] * group, axis=0)
    ones_blk = jnp.ones((win, LANES), BF16)
    lane = lax.broadcasted_iota(jnp.int32, (tq, LANES), 1)
    lse_tile = jnp.zeros((tq, LANES), F32)

    scores = []
    for kv in range(n_kv):
        kcols = slice(kv * HEAD_DIM, (kv + 1) * HEAD_DIM)
        k_win = jnp.concatenate(
            [kp_ref[tq - hw:, kcols], kc_ref[:, kcols], kn_ref[:hw, kcols]], axis=0)
        q = jnp.concatenate(
            [q_ref[:, hd * HEAD_DIM:(hd + 1) * HEAD_DIM]
             for hd in range(kv * group, (kv + 1) * group)], axis=0)
        scores.append(
            lax.dot_general(q, k_win, (((1,), (1,)), ((), ())), preferred_element_type=F32) + bias)
    probs = []
    for kv, s in enumerate(scores):
        mx = s[:, :LANES]
        for cb in range(1, n_col):
            mx = jnp.maximum(mx, s[:, cb * LANES:(cb + 1) * LANES])
        m = jnp.broadcast_to(jnp.max(mx, axis=-1, keepdims=True), mx.shape)
        if has_sink:
            m = jnp.maximum(m, sink_ref[kv * group * tq:(kv + 1) * group * tq, :])
        probs.append((m, jnp.exp(s - jnp.concatenate([m] * n_col, axis=1)).astype(BF16)))
    for kv, (m, p) in enumerate(probs):
        kcols = slice(kv * HEAD_DIM, (kv + 1) * HEAD_DIM)
        v_win = jnp.concatenate(
            [vp_ref[tq - hw:, kcols], vc_ref[:, kcols], vn_ref[:hw, kcols]], axis=0)
        o_ext = jnp.dot(p, jnp.concatenate([v_win, ones_blk], axis=1), preferred_element_type=F32)
        denom = o_ext[:, HEAD_DIM:]
        if has_sink:
            denom = denom + jnp.exp(sink_ref[kv * group * tq:(kv + 1) * group * tq, :] - m)
        o = o_ext[:, :HEAD_DIM] / denom
        for c in range(group):
            hd = kv * group + c
            o_ref[:, hd * HEAD_DIM:(hd + 1) * HEAD_DIM] = o[c * tq:(c + 1) * tq].astype(o_ref.dtype)
        if has_lse:
            lse_tile = jnp.where(lane == kv, m + jnp.log(denom), lse_tile)
    if has_lse:
        lse_ref[...] = lse_tile


def _banded_attention(q_src, k_src, v_src, *, dil, n_q, n_kv, hw, sink_rep, has_lse, name):
    tq = 128
    length = SEQ // dil
    nb = length // tq
    group = n_q // n_kv
    qw, kw = n_q * HEAD_DIM, n_kv * HEAD_DIM

    def chain_view(src):
        arr, _ = src
        return arr.reshape(length, dil * arr.shape[1])

    def spec(src, width, shift):
        arr, col = src
        per_row = arr.shape[1] // width
        return pl.BlockSpec(
            (tq, width), lambda r, i: (jnp.clip(i + shift, 0, nb - 1), r * per_row + col))

    win = tq + 2 * hw
    rows = jnp.arange(tq, dtype=jnp.int32)[:, None]
    cols = jnp.arange(win, dtype=jnp.int32)[None, :]
    band = jnp.where(jnp.abs(rows + hw - cols) <= hw, 0.0, NEG).astype(F32)

    in_specs = [spec(q_src, qw, 0),
                spec(k_src, kw, -1), spec(k_src, kw, 0), spec(k_src, kw, 1),
                spec(v_src, kw, -1), spec(v_src, kw, 0), spec(v_src, kw, 1),
                pl.BlockSpec((tq, win), lambda r, i: (0, 0))]
    args = [chain_view(q_src)] + [chain_view(k_src)] * 3 + [chain_view(v_src)] * 3 + [band]
    if sink_rep is not None:
        in_specs.append(pl.BlockSpec(sink_rep.shape, lambda r, i: (0, 0)))
        args.append(sink_rep)
    out_shape = [jax.ShapeDtypeStruct((length, dil * qw), BF16)]
    out_specs = [pl.BlockSpec((tq, qw), lambda r, i: (i, r))]
    if has_lse:
        out_shape.append(jax.ShapeDtypeStruct((length, dil * LANES), F32))
        out_specs.append(pl.BlockSpec((tq, LANES), lambda r, i: (i, r)))
    kernel = functools.partial(
        _banded_kernel, tq=tq, hw=hw, n_kv=n_kv, group=group, length=length, q_axis=1,
        has_sink=sink_rep is not None, has_lse=has_lse)
    outs = pl.pallas_call(
        kernel, grid=(dil, nb), in_specs=in_specs, out_specs=out_specs, out_shape=out_shape,
        compiler_params=_params("parallel", "parallel"), name=name,
    )(*args)
    o = outs[0].reshape(SEQ, qw)
    if has_lse:
        return o, outs[1].reshape(SEQ, LANES)
    return o


def _flash_kernel(q_ref, k_ref, v_ref, o_ref, m_ref, acc_ref, *, tk):
    tq = q_ref.shape[0]
    n_chunks = k_ref.shape[0] // tk
    n_col = tk // LANES
    m_ref[...] = jnp.full((tq, LANES), NEG, F32)
    acc_ref[...] = jnp.zeros((tq, B_HEAD_PAD), F32)
    q = q_ref[...]

    for c in range(n_chunks):
        k = k_ref[c * tk:(c + 1) * tk, :]
        v = v_ref[c * tk:(c + 1) * tk, :]
        s = lax.dot_general(q, k, (((1,), (1,)), ((), ())), preferred_element_type=F32)
        mx = s[:, :LANES]
        for cb in range(1, n_col):
            mx = jnp.maximum(mx, s[:, cb * LANES:(cb + 1) * LANES])
        m_old = m_ref[...]
        m_new = jnp.maximum(m_old, jnp.max(mx, axis=-1, keepdims=True))
        alpha = jnp.exp2(m_old - m_new)
        p = jnp.exp2(s - jnp.concatenate([m_new] * n_col, axis=1)).astype(BF16)
        pv = jnp.dot(p, v, preferred_element_type=F32)
        acc_ref[...] = acc_ref[...] * jnp.concatenate([alpha, alpha], axis=1) + pv
        m_ref[...] = m_new

    acc = acc_ref[...]
    o_ref[...] = (acc[:, :B_NOPE] / acc[:, B_NOPE:]).astype(o_ref.dtype)


def _dense_attention(q, k, v_ext):
    tq, tk = 1024, 256
    kv_spec = pl.BlockSpec((SEQ, B_HEAD_PAD), lambda h, i: (0, h))
    return pl.pallas_call(
        functools.partial(_flash_kernel, tk=tk),
        grid=(B_HEADS, SEQ // tq),
        in_specs=[pl.BlockSpec((tq, B_HEAD_PAD), lambda h, i: (i, h)), kv_spec, kv_spec],
        out_specs=pl.BlockSpec((tq, B_NOPE), lambda h, i: (i, h)),
        out_shape=jax.ShapeDtypeStruct((SEQ, B_HEADS * B_NOPE), BF16),
        scratch_shapes=[pltpu.VMEM((tq, LANES), F32), pltpu.VMEM((tq, B_HEAD_PAD), F32)],
        compiler_params=_params("parallel", "arbitrary"),
        name="b_flash_attention",
    )(q, k, v_ext)


def _out_proj_kernel(o_ref, w_ref, h_ref, out_ref, *, n_sub):
    sub = o_ref.shape[0] // n_sub
    for c in range(n_sub):
        rows = slice(c * sub, (c + 1) * sub)
        out_ref[rows, :] = h_ref[rows, :] + jnp.dot(o_ref[rows, :], w_ref[...],
                                                    preferred_element_type=F32)


def _out_projection(o, w, slot, h, name):
    tm = 1024
    row = pl.BlockSpec((tm, D_MODEL), lambda i: (i, 0))
    return pl.pallas_call(
        functools.partial(_out_proj_kernel, n_sub=4),
        grid=(SEQ // tm,),
        in_specs=[row,
                  pl.BlockSpec((None, D_MODEL, D_MODEL), lambda i: (slot, 0, 0),
                               pipeline_mode=pl.Buffered(1)),
                  row],
        out_specs=row,
        out_shape=jax.ShapeDtypeStruct((SEQ, D_MODEL), F32),
        compiler_params=_params("parallel"),
        name=name,
    )(o, w, h)


def _ffn_kernel(x_ref, g_ref, wg_ref, wu_ref, wd_ref, o_ref, xn_ref):
    @pl.when(pl.program_id(1) == 0)
    def _():
        x = x_ref[...]
        xn_ref[...] = (_rms_scale(x, D_MODEL) * g_ref[...]).astype(BF16)
        o_ref[...] = x

    xn = xn_ref[...]
    gate = jnp.dot(xn, wg_ref[...], preferred_element_type=F32)
    up = jnp.dot(xn, wu_ref[...], preferred_element_type=F32)
    act = (gate * jax.nn.sigmoid(gate) * up).astype(BF16)
    o_ref[...] += jnp.dot(act, wd_ref[...], preferred_element_type=F32)


def _ffn(h, g, wg, wu, wd, layer):
    tm, tf = 1024, 512
    return pl.pallas_call(
        _ffn_kernel,
        grid=(SEQ // tm, D_FF // tf),
        in_specs=[pl.BlockSpec((tm, D_MODEL), lambda i, f: (i, 0)),
                  pl.BlockSpec((1, D_MODEL), lambda i, f: (0, 0)),
                  pl.BlockSpec((None, D_MODEL, tf), lambda i, f: (layer, 0, f)),
                  pl.BlockSpec((None, D_MODEL, tf), lambda i, f: (layer, 0, f)),
                  pl.BlockSpec((None, tf, D_MODEL), lambda i, f: (layer, f, 0))],
        out_specs=pl.BlockSpec((tm, D_MODEL), lambda i, f: (i, 0)),
        out_shape=jax.ShapeDtypeStruct((SEQ, D_MODEL), F32),
        scratch_shapes=[pltpu.VMEM((tm, D_MODEL), BF16)],
        compiler_params=_params("parallel", "arbitrary"),
        name="ffn_swiglu",
    )(h, g, wg, wu, wd)


def _ple_kernel(x_ref, g_ref, wg_ref, p_ref, wp_ref, o_ref, *, n_sub):
    sub = x_ref.shape[0] // n_sub
    for c in range(n_sub):
        rows = slice(c * sub, (c + 1) * sub)
        x = x_ref[rows, :]
        xn = (_rms_scale(x, D_MODEL) * g_ref[...]).astype(BF16)
        gate = jnp.dot(xn, wg_ref[...], preferred_element_type=F32)
        proj = jnp.dot(p_ref[rows, :].astype(BF16), wp_ref[...], preferred_element_type=F32)
        o_ref[rows, :] = x + jax.nn.sigmoid(gate) * proj


def _ple(h, g, w_gate, p, w_proj, layer):
    tm = 1024
    resident = pl.Buffered(1)
    return pl.pallas_call(
        functools.partial(_ple_kernel, n_sub=4),
        grid=(SEQ // tm,),
        in_specs=[pl.BlockSpec((tm, D_MODEL), lambda i: (i, 0)),
                  pl.BlockSpec((1, D_MODEL), lambda i: (0, 0)),
                  pl.BlockSpec((None, D_MODEL, D_MODEL), lambda i: (layer, 0, 0),
                               pipeline_mode=resident),
                  pl.BlockSpec((None, tm, PLE_DIM), lambda i: (layer, i, 0)),
                  pl.BlockSpec((None, PLE_DIM, D_MODEL), lambda i: (layer, 0, 0),
                               pipeline_mode=resident)],
        out_specs=pl.BlockSpec((tm, D_MODEL), lambda i: (i, 0)),
        out_shape=jax.ShapeDtypeStruct((SEQ, D_MODEL), F32),
        compiler_params=_params("parallel"),
        name="ple_gate",
    )(h, g, w_gate, p, w_proj)


def _mixer_a(h, g_mix, w_in, slot, gq, gk, sink, w_o, tables):
    nq, nk = A_HEADS * HEAD_DIM, A_KV_HEADS * HEAD_DIM
    scale = 1.0 / math.sqrt(HEAD_DIM)
    head_gain = jnp.concatenate([jnp.tile(gq * scale, A_HEADS), jnp.tile(gk, A_KV_HEADS)])[None, :]
    qk, v = _qkv_projection(h, g_mix, w_in, slot, nq + nk, nk, head_gain, tables, "a")
    sink_rep = jnp.broadcast_to(jnp.repeat(sink, 128)[:, None], (A_HEADS * 128, LANES))
    o = _banded_attention(
        (qk, 0), (qk, nq // nk), (v, 0), dil=1, n_q=A_HEADS, n_kv=A_KV_HEADS, hw=A_HALF_WINDOW,
        sink_rep=sink_rep, has_lse=False, name="a_banded_attention")
    return _out_projection(o, w_o, slot, h, "a_out_proj")


def _mixer_b(h, g_mix, w_in, g_qlat, g_kvlat, w_q_up, w_kv_up, slot, gq, gk, w_o, tables):
    scale = math.log2(math.e) / math.sqrt(B_QK)
    half = B_ROPE // 2
    n_lat = B_Q_RANK + B_KV_RANK
    w_in = w_in[slot]
    w_in_pad = jnp.concatenate([w_in[:, :n_lat], _rotary_slab(w_in[:, n_lat:], half)], axis=1)[None]
    tm = 1024
    lat = _norm_matmul(
        h, 0, D_MODEL, g_mix, w_in_pad, 0, 0, 1, tm=tm, tn=B_IN_PAD, extra=(), extra_specs=[],
        epilogue=_plain_epilogue,
        out_shape=jax.ShapeDtypeStruct((SEQ, B_IN_PAD), F32),
        out_specs=pl.BlockSpec((tm, B_IN_PAD), lambda i, j: (i, j)),
        name="b_latent_proj")

    tab = pl.BlockSpec((tm, LANES), lambda i, j: (i, 0))
    tn = 1024
    n_slab = B_HEADS * B_HEAD_PAD

    def head_slab(x):
        return jnp.concatenate([x[..., :B_NOPE], _rotary_slab(x[..., B_NOPE:], half)], axis=-1)

    wq = head_slab(w_q_up[slot].reshape(B_Q_RANK, B_HEADS, B_QK)).reshape(B_Q_RANK, n_slab)[None]
    gq_slab = jnp.tile(head_slab(gq * scale), B_HEADS)[None, :]
    q = _norm_matmul(
        lat, 0, B_Q_RANK, g_qlat, wq, 0, 0, n_slab // tn, tm=tm, tn=tn,
        extra=(gq_slab,) + tuple(tables),
        extra_specs=[pl.BlockSpec((1, tn), lambda i, j: (0, j)), tab, tab],
        epilogue=_b_q_epilogue,
        out_shape=jax.ShapeDtypeStruct((SEQ, n_slab), BF16),
        out_specs=pl.BlockSpec((tm, tn), lambda i, j: (i, j)),
        name="b_q_proj")

    gk_slab = head_slab(gk)[None, :]
    slab_out = pl.BlockSpec((tm, tn), lambda i, j: (i, j))
    k, v_ext = _norm_matmul(
        lat, 1, B_KV_RANK, g_kvlat, w_kv_up, slot, 0, n_slab // tn, tm=tm, tn=tn,
        extra=(gk_slab, lat) + tuple(tables),
        extra_specs=[pl.BlockSpec((1, B_HEAD_PAD), lambda i, j: (0, 0)),
                     pl.BlockSpec((tm, LANES), lambda i, j: (i, n_lat // LANES)),
                     tab, tab],
        epilogue=_b_kv_epilogue,
        out_shape=[jax.ShapeDtypeStruct((SEQ, n_slab), BF16)] * 2,
        out_specs=[slab_out, slab_out],
        name="b_kv_proj")
    o = _dense_attention(q, k, v_ext)
    return _out_projection(o, w_o, slot, h, "b_out_proj")


def _dilated_kernel(q0_ref, q1_ref, q2_ref, kp_ref, kc_ref, kn_ref, vp_ref, vc_ref, vn_ref,
                    band_ref, o_ref, *scratch):
    i = pl.program_id(0)
    blk = q0_ref.shape[0]
    band = band_ref[...]
    win = band.shape[1]
    ones_blk = jnp.ones((win, LANES), BF16)
    col = lax.broadcasted_iota(jnp.int32, (1, win), 1)
    n4, n16 = blk // HOP, blk // (HOP * HOP)
    scratch = list(scratch)
    take = lambda k: [scratch.pop(0) for _ in range(k)]
    o_s, l_s = take(C_GROUPS), take(C_GROUPS)
    hop1 = take(8)
    hop2 = take(7)
    o2_hop1, l2_hop1, merged_tok = take(3)

    for src, dst in zip((q1_ref, q2_ref, kp_ref, kc_ref, kn_ref, vp_ref, vc_ref, vn_ref), hop1):
        for r in range(HOP):
            dst[r * n4:(r + 1) * n4, :] = src[pl.ds(r, n4, stride=HOP), :]
    for src, dst in zip(hop1[1:], hop2):
        for r in range(HOP * HOP):
            dst[r * n16:(r + 1) * n16, :] = src[pl.ds((r % HOP) * n4 + r // HOP, n16, stride=HOP), :]
    operands = ((q0_ref, kp_ref, kc_ref, kn_ref, vp_ref, vc_ref, vn_ref),
                (hop1[0],) + tuple(hop1[2:]),
                tuple(hop2))

    def attend(q, k_win, v_win, bias):
        v_ext = jnp.concatenate([v_win, ones_blk], axis=1)
        s = lax.dot_general(q, k_win, (((1,), (1,)), ((), ())), preferred_element_type=F32) + bias
        mx = jnp.maximum(s[:, :LANES], s[:, LANES:])
        m = jnp.broadcast_to(jnp.max(mx, axis=-1, keepdims=True), mx.shape)
        p = jnp.exp(s - jnp.concatenate([m, m], axis=1)).astype(BF16)
        o_ext = jnp.dot(p, v_ext, preferred_element_type=F32)
        denom = o_ext[:, HEAD_DIM:]
        return o_ext[:, :HEAD_DIM] / denom, m + jnp.log(denom)

    for g, ((window, d), refs) in enumerate(zip(C_PATTERNS, operands)):
        q_ref, kprev, kcur, knext, vprev, vcur, vnext = refs
        n = blk // d
        rq = min(n, win - 2 * C_HALF)
        pad_rows = win - min(n + 2 * C_HALF, win)
        tiles = []
        for r in range(d):
            def chain_window(prev_ref, cur_ref, next_ref):
                parts = [prev_ref[(r + 1) * n - C_HALF:(r + 1) * n, :], cur_ref[r * n:(r + 1) * n, :],
                         next_ref[r * n:r * n + C_HALF, :]]
                if pad_rows:
                    parts.append(jnp.zeros((pad_rows, LANES), F32))
                return jnp.concatenate(parts, axis=0).astype(BF16)

            k_chain = chain_window(kprev, kcur, knext)
            v_chain = chain_window(vprev, vcur, vnext)
            for sb in range(n // rq):
                kpos = i * n + sb * rq - C_HALF + col
                edge = jnp.where((kpos >= 0) & (kpos < SEQ // d), 0.0, NEG)
                dst = slice(r * n + sb * rq, r * n + (sb + 1) * rq)
                tiles.append((dst, q_ref[dst, :].astype(BF16), k_chain[sb * rq:sb * rq + win],
                              v_chain[sb * rq:sb * rq + win], band[:rq] + edge))
        full = win - 2 * C_HALF
        batch = 1 if rq == full else 4 * full // rq
        for b0 in range(0, len(tiles), batch):
            group_tiles = tiles[b0:b0 + batch]
            scores = [lax.dot_general(q, k_win, (((1,), (1,)), ((), ())),
                                      preferred_element_type=F32) + bias
                      for _, q, k_win, _, bias in group_tiles]
            probs = []
            for s in scores:
                mx = jnp.maximum(s[:, :LANES], s[:, LANES:])
                m = jnp.broadcast_to(jnp.max(mx, axis=-1, keepdims=True), mx.shape)
                probs.append((m, jnp.exp(s - jnp.concatenate([m, m], axis=1)).astype(BF16)))
            for (dst, _, _, v_win, _), (m, p) in zip(group_tiles, probs):
                o_ext = jnp.dot(p, jnp.concatenate([v_win, ones_blk], axis=1),
                                preferred_element_type=F32)
                denom = o_ext[:, HEAD_DIM:]
                o_s[g][dst, :] = o_ext[:, :HEAD_DIM] / denom
                l_s[g][dst, :] = m + jnp.log(denom)

    for src, dst in ((o_s[2], o2_hop1), (l_s[2], l2_hop1)):
        for r in range(HOP * HOP):
            dst[pl.ds((r % HOP) * n4 + r // HOP, n16, stride=HOP), :] = src[r * n16:(r + 1) * n16, :]
    for r in range(HOP):
        rows = slice(r * n4, (r + 1) * n4)
        tok = pl.ds(r, n4, stride=HOP)
        l0, l1, l2 = l_s[0][tok, :], l_s[1][rows, :], l2_hop1[rows, :]
        m = jnp.maximum(jnp.maximum(l0, l1), l2)
        e0, e1, e2 = jnp.exp(l0 - m), jnp.exp(l1 - m), jnp.exp(l2 - m)
        merged = (e0 * o_s[0][tok, :] + e1 * o_s[1][rows, :] + e2 * o2_hop1[rows, :]) / (e0 + e1 + e2)
        merged_tok[tok, :] = merged
    o_ref[...] = merged_tok[...].astype(o_ref.dtype)


def _dilated_attention(qk, v):
    blk = 1024
    nb = SEQ // blk
    win = 2 * LANES
    rows = jnp.arange(LANES, dtype=jnp.int32)[:, None]
    cols = jnp.arange(win, dtype=jnp.int32)[None, :]
    band = jnp.where(jnp.abs(rows + C_HALF - cols) <= C_HALF, 0.0, NEG).astype(F32)

    def slab(col0, shift):
        return pl.BlockSpec((blk, HEAD_DIM),
                            lambda i, h: (jnp.clip(i + shift, 0, nb - 1), col0 + h))

    k_col = C_GROUPS * C_HEADS
    in_specs = [slab(g * C_HEADS, 0) for g in range(C_GROUPS)]
    in_specs += [slab(k_col, -1), slab(k_col, 0), slab(k_col, 1)]
    in_specs += [slab(0, -1), slab(0, 0), slab(0, 1)]
    in_specs.append(pl.BlockSpec((LANES, win), lambda i, h: (0, 0)))
    return pl.pallas_call(
        _dilated_kernel,
        grid=(nb, C_HEADS),
        in_specs=in_specs,
        out_specs=pl.BlockSpec((blk, HEAD_DIM), lambda i, h: (i, h)),
        out_shape=jax.ShapeDtypeStruct((SEQ, C_HEADS * HEAD_DIM), BF16),
        scratch_shapes=[pltpu.VMEM((blk, LANES), F32)] * (2 * C_GROUPS + 8 + 7 + 3),
        compiler_params=_params("parallel", "parallel"),
        name="c_dilated_attention",
    )(qk, qk, qk, qk, qk, qk, v, v, v, band)


def _mixer_c(h, g_mix, w_in, slot, gq, gk, w_o, tables):
    n_qh = C_GROUPS * C_HEADS
    nq = n_qh * HEAD_DIM
    nkv = C_HEADS * HEAD_DIM
    scale = 1.0 / math.sqrt(HEAD_DIM)
    head_gain = jnp.concatenate([jnp.tile(gq * scale, n_qh), jnp.tile(gk, C_HEADS)])[None, :]
    qk, v = _qkv_projection(h, g_mix, w_in, slot, nq + nkv, nkv, head_gain, tables, "c", dtype=F32)
    o = _dilated_attention(qk, v)
    return _out_projection(o, w_o, slot, h, "c_out_proj")


def kernel(x, p, positions, g_mix, g_ffn, g_ple, w_ple_gate, w_ple_proj,
           w_ffn_gate, w_ffn_up, w_ffn_down,
           a_w_in, a_q_norm, a_k_norm, a_sink, a_w_o,
           b_w_in, b_q_lat_norm, b_kv_lat_norm, b_w_q_up, b_w_kv_up, b_q_norm, b_k_norm, b_w_o,
           c_w_in, c_q_norm, c_k_norm, c_w_o):
    h = x.reshape(SEQ, D_MODEL)
    p = p.reshape(DEPTH, SEQ, PLE_DIM)
    pos_col = positions.reshape(SEQ, 1)
    tables_ac = _tables_adjacent(pos_col, PARTIAL_ROT)
    tables_b = _tables_split(pos_col, B_ROPE)
    (w_ple_gate, w_ple_proj, w_ffn_gate, w_ffn_up, w_ffn_down, a_w_in, a_w_o, b_w_in, b_w_q_up,
     b_w_kv_up, b_w_o, c_w_in, c_w_o) = [
        w.astype(BF16) for w in (w_ple_gate, w_ple_proj, w_ffn_gate, w_ffn_up, w_ffn_down, a_w_in,
                                 a_w_o, b_w_in, b_w_q_up, b_w_kv_up, b_w_o, c_w_in, c_w_o)]
    for i in range(DEPTH):
        kind, slot = i % 3, i // 3
        gm = g_mix[i][None, :]
        if kind == 0:
            h = _mixer_a(h, gm, a_w_in, slot, a_q_norm[slot], a_k_norm[slot], a_sink[slot],
                         a_w_o, tables_ac)
        elif kind == 1:
            h = _mixer_b(h, gm, b_w_in, b_q_lat_norm[slot][None, :], b_kv_lat_norm[slot][None, :],
                         b_w_q_up, b_w_kv_up, slot, b_q_norm[slot], b_k_norm[slot], b_w_o,
                         tables_b)
        else:
            h = _mixer_c(h, gm, c_w_in, slot, c_q_norm[slot], c_k_norm[slot], c_w_o, tables_ac)
        h = _ffn(h, g_ffn[i][None, :], w_ffn_gate, w_ffn_up, w_ffn_down, i)
        h = _ple(h, g_ple[i][None, :], w_ple_gate, p, w_ple_proj, i)
    return h.reshape(1, SEQ, D_MODEL)
```

```python
import functools
import math

import jax
import jax.numpy as jnp
from jax import lax
from jax.experimental import pallas as pl
from jax.experimental.pallas import tpu as pltpu

F32 = jnp.float32
BF16 = jnp.bfloat16

SEQ = 8192
D_MODEL = 2048
DEPTH = 4
HEAD_DIM = 128
ROPE_THETA = 500000.0
PARTIAL_ROT = HEAD_DIM // 4
NORM_EPS = 1e-6
NEG = -1e30
LANES = 128
HALF_LANES = LANES // 2

A_HEADS = 16
A_KV_HEADS = 4
A_HALF_WINDOW = 128
B_HEADS = 16
B_Q_RANK = 512
B_KV_RANK = 512
B_NOPE = 128
B_ROPE = 64
B_QK = B_NOPE + B_ROPE
B_HEAD_PAD = 256
B_IN_PAD = B_Q_RANK + B_KV_RANK + LANES
C_PATTERNS = ((128, 1), (512, 4), (2048, 16))
C_GROUPS = 3
C_HEADS = 16
C_HALF = 64
assert all(window // 2 // dil == C_HALF for window, dil in C_PATTERNS)
HOP = 4
assert tuple(dil for _, dil in C_PATTERNS) == (1, HOP, HOP * HOP)
D_FF = 5632
PLE_DIM = 256

VMEM_LIMIT = 56 * 1024 * 1024


def _params(*sem):
    return pltpu.CompilerParams(dimension_semantics=sem, vmem_limit_bytes=VMEM_LIMIT)


def _rms_scale(x, width):
    ss = jnp.sum(x * x, axis=-1, keepdims=True)
    return x * lax.rsqrt(ss * (1.0 / width) + NORM_EPS)


def _rope_adjacent(y, cos, sin_lo, sin_hi, half):
    return (y * cos + pltpu.roll(y, LANES - half, 1) * sin_lo
            + pltpu.roll(y, half, 1) * sin_hi)


def _rope_split(y, cos, sin):
    return y * cos + pltpu.roll(y, HALF_LANES, 1) * sin


def _rotary_slab(x, half):
    pad = jnp.zeros(x.shape[:-1] + (HALF_LANES - half,), x.dtype)
    return jnp.concatenate([x[..., :half], pad, x[..., half:], pad], axis=-1)


def _rope_table_kernel(pos_ref, inv_ref, sign_ref, cos_ref, *sin_refs):
    ang = pos_ref[...].astype(F32) * inv_ref[...]
    cos_ref[...] = jnp.cos(ang)
    s = jnp.sin(ang)
    for r, sin_ref in enumerate(sin_refs):
        sin_ref[...] = s * sign_ref[r:r + 1, :]


def _rope_tables(pos_col, inv_lanes, signs):
    tm = 1024
    n = signs.shape[0]
    tab = pl.BlockSpec((tm, LANES), lambda i: (i, 0))
    return pl.pallas_call(
        _rope_table_kernel,
        grid=(SEQ // tm,),
        in_specs=[pl.BlockSpec((tm, 1), lambda i: (i, 0)),
                  pl.BlockSpec((1, LANES), lambda i: (0, 0)),
                  pl.BlockSpec((n, LANES), lambda i: (0, 0))],
        out_specs=[tab] * (n + 1),
        out_shape=[jax.ShapeDtypeStruct((SEQ, LANES), F32)] * (n + 1),
        compiler_params=_params("parallel"),
        name="rope_tables",
    )(pos_col, inv_lanes, signs)


def _inv_freq(rot_dim):
    half = rot_dim // 2
    return ROPE_THETA ** (-jnp.arange(half, dtype=F32) * 2.0 / rot_dim)


def _tables_adjacent(pos_col, rot_dim):
    half = rot_dim // 2
    inv = _inv_freq(rot_dim)
    rest = jnp.zeros((LANES - rot_dim,), F32)
    zero, one = jnp.zeros((half,), F32), jnp.ones((half,), F32)
    inv_l = jnp.concatenate([inv, inv, rest])[None, :]
    signs = jnp.stack([jnp.concatenate([-one, zero, rest]), jnp.concatenate([zero, one, rest])])
    return _rope_tables(pos_col, inv_l, signs)


def _tables_split(pos_col, rot_dim):
    half = rot_dim // 2
    inv = _inv_freq(rot_dim)
    one = jnp.ones((half,), F32)
    inv_l = _rotary_slab(jnp.concatenate([inv, inv]), half)[None, :]
    signs = _rotary_slab(jnp.concatenate([-one, one]), half)[None, :]
    return _rope_tables(pos_col, inv_l, signs)


def _norm_matmul_kernel(x_ref, g_ref, w_ref, *rest, n_extra, epilogue, n_sub):
    extra = rest[:n_extra]
    outs = rest[n_extra:-1]
    xn_ref = rest[-1]

    sub = xn_ref.shape[0] // n_sub

    def column_tile(first):
        for c in range(n_sub):
            rows = slice(c * sub, (c + 1) * sub)
            if first:
                x = x_ref[rows, :]
                xn_ref[rows, :] = (_rms_scale(x, x.shape[-1]) * g_ref[...]).astype(BF16)
            acc = jnp.dot(xn_ref[rows, :], w_ref[...], preferred_element_type=F32)
            epilogue(acc, rows, extra, outs)

    @pl.when(pl.program_id(1) == 0)
    def _():
        column_tile(True)

    @pl.when(pl.program_id(1) > 0)
    def _():
        column_tile(False)


def _norm_matmul(x, x_col, kin, gain, w, w_slot, w_col0, n_tiles, *, tm, tn, extra, extra_specs,
                 epilogue, out_shape, out_specs, name, n_sub=4):
    m = x.shape[0]
    kernel = functools.partial(_norm_matmul_kernel, n_extra=len(extra), epilogue=epilogue,
                               n_sub=n_sub)
    return pl.pallas_call(
        kernel,
        grid=(m // tm, n_tiles),
        in_specs=[pl.BlockSpec((tm, kin), lambda i, j: (i, x_col)),
                  pl.BlockSpec((1, kin), lambda i, j: (0, 0)),
                  pl.BlockSpec((None, kin, tn), lambda i, j: (w_slot, 0, w_col0 + j))]
        + list(extra_specs),
        out_specs=out_specs,
        out_shape=out_shape,
        scratch_shapes=[pltpu.VMEM((tm, kin), BF16)],
        compiler_params=_params("parallel", "arbitrary"),
        name=name,
    )(x, gain, w, *extra)


def _plain_epilogue(acc, rows, extra, outs):
    outs[0][rows, :] = acc.astype(outs[0].dtype)


def _head_norm_rope_epilogue(acc, rows, extra, outs):
    gain_ref, cos_ref, slo_ref, shi_ref = extra
    (o_ref,) = outs
    cos, slo, shi = cos_ref[rows, :], slo_ref[rows, :], shi_ref[rows, :]
    for c in range(acc.shape[1] // HEAD_DIM):
        cols = slice(c * HEAD_DIM, (c + 1) * HEAD_DIM)
        y = _rms_scale(acc[:, cols], HEAD_DIM) * gain_ref[:, cols]
        o_ref[rows, cols] = _rope_adjacent(y, cos, slo, shi, PARTIAL_ROT // 2).astype(o_ref.dtype)


def _qkv_projection(h, g_mix, w, slot, n_qk, n_v, head_gain, tables, name, dtype=BF16):
    tm, tn = 1024, 512
    tab = pl.BlockSpec((tm, LANES), lambda i, j: (i, 0))
    qk = _norm_matmul(
        h, 0, D_MODEL, g_mix, w, slot, 0, n_qk // tn, tm=tm, tn=tn,
        extra=(head_gain,) + tuple(tables),
        extra_specs=[pl.BlockSpec((1, tn), lambda i, j: (0, j)), tab, tab, tab],
        epilogue=_head_norm_rope_epilogue,
        out_shape=jax.ShapeDtypeStruct((SEQ, n_qk), dtype),
        out_specs=pl.BlockSpec((tm, tn), lambda i, j: (i, j)),
        name=name + "_qk_proj")
    v = _norm_matmul(
        h, 0, D_MODEL, g_mix, w, slot, n_qk // tn, n_v // tn, tm=tm, tn=tn,
        extra=(), extra_specs=[], epilogue=_plain_epilogue,
        out_shape=jax.ShapeDtypeStruct((SEQ, n_v), dtype),
        out_specs=pl.BlockSpec((tm, tn), lambda i, j: (i, j)),
        name=name + "_v_proj")
    return qk, v


def _b_q_epilogue(acc, rows, extra, outs):
    gain_ref, cos_ref, sin_ref = extra
    (o_ref,) = outs
    cos, sin = cos_ref[rows, :], sin_ref[rows, :]
    for c in range(acc.shape[1] // B_HEAD_PAD):
        c0 = c * B_HEAD_PAD
        y = _rms_scale(acc[:, c0:c0 + B_HEAD_PAD], B_QK) * gain_ref[:, c0:c0 + B_HEAD_PAD]
        o_ref[rows, c0:c0 + B_NOPE] = y[:, :B_NOPE].astype(BF16)
        o_ref[rows, c0 + B_NOPE:c0 + B_HEAD_PAD] = _rope_split(y[:, B_NOPE:], cos, sin).astype(BF16)


def _b_kv_epilogue(acc, rows, extra, outs):
    gain_ref, krope_ref, cos_ref, sin_ref = extra
    k_ref, v_ref = outs
    cos, sin = cos_ref[rows, :], sin_ref[rows, :]
    kr = krope_ref[rows, :]
    kr_ss = jnp.sum(kr * kr, axis=-1, keepdims=True)
    g_nope = gain_ref[:, :B_NOPE]
    g_rope = gain_ref[:, B_NOPE:]
    ones_blk = jnp.ones((acc.shape[0], LANES), BF16)
    for c in range(acc.shape[1] // B_HEAD_PAD):
        c0 = c * B_HEAD_PAD
        y = acc[:, c0:c0 + B_NOPE]
        ss = jnp.sum(y * y, axis=-1, keepdims=True) + kr_ss
        rinv = lax.rsqrt(ss * (1.0 / B_QK) + NORM_EPS)
        k_ref[rows, c0:c0 + B_NOPE] = (y * rinv * g_nope).astype(BF16)
        k_ref[rows, c0 + B_NOPE:c0 + B_HEAD_PAD] = _rope_split(
            kr * rinv * g_rope, cos, sin).astype(BF16)
        v_ref[rows, c0:c0 + B_NOPE] = acc[:, c0 + B_NOPE:c0 + B_HEAD_PAD].astype(BF16)
        v_ref[rows, c0 + B_NOPE:c0 + B_HEAD_PAD] = ones_blk


def _banded_kernel(*refs, tq, hw, n_kv, group, length, q_axis, has_sink, has_lse):
    q_ref, kp_ref, kc_ref, kn_ref, vp_ref, vc_ref, vn_ref, band_ref = refs[:8]
    nxt = 8
    sink_ref = None
    if has_sink:
        sink_ref = refs[nxt]
        nxt += 1
    o_ref = refs[nxt]
    lse_ref = refs[nxt + 1] if has_lse else None

    i = pl.program_id(q_axis)
    win = tq + 2 * hw
    n_col = win // LANES
    kpos = i * tq - hw + lax.broadcasted_iota(jnp.int32, (1, win), 1)
    edge = jnp.where((kpos >= 0) & (kpos < length), 0.0, NEG)
    bias = band_ref[...] + edge
    if group > 1:
        bias = jnp.concatenate([bias] * group, axis=0)
    ones_blk = jnp.ones((win, LANES), BF16)
    lane = lax.broadcasted_iota(jnp.int32, (tq, LANES), 1)
    lse_tile = jnp.zeros((tq, LANES), F32)

    scores = []
    for kv in range(n_kv):
        kcols = slice(kv * HEAD_DIM, (kv + 1) * HEAD_DIM)
        k_win = jnp.concatenate(
            [kp_ref[tq - hw:, kcols], kc_ref[:, kcols], kn_ref[:hw, kcols]], axis=0)
        q = jnp.concatenate(
            [q_ref[:, hd * HEAD_DIM:(hd + 1) * HEAD_DIM]
             for hd in range(kv * group, (kv + 1) * group)], axis=0)
        scores.append(
            lax.dot_general(q, k_win, (((1,), (1,)), ((), ())), preferred_element_type=F32) + bias)
    probs = []
    for kv, s in enumerate(scores):
        mx = s[:, :LANES]
        for cb in range(1, n_col):
            mx = jnp.maximum(mx, s[:, cb * LANES:(cb + 1) * LANES])
        m = jnp.broadcast_to(jnp.max(mx, axis=-1, keepdims=True), mx.shape)
        if has_sink:
            m = jnp.maximum(m, sink_ref[kv * group * tq:(kv + 1) * group * tq, :])
        probs.append((m, jnp.exp(s - jnp.concatenate([m] * n_col, axis=1)).astype(BF16)))
    for kv, (m, p) in enumerate(probs):
        kcols = slice(kv * HEAD_DIM, (kv + 1) * HEAD_DIM)
        v_win = jnp.concatenate(
            [vp_ref[tq - hw:, kcols], vc_ref[:, kcols], vn_ref[:hw, kcols]], axis=0)
        o_ext = jnp.dot(p, jnp.concatenate([v_win, ones_blk], axis=1), preferred_element_type=F32)
        denom = o_ext[:, HEAD_DIM:]
        if has_sink:
            denom = denom + jnp.exp(sink_ref[kv * group * tq:(kv + 1) * group * tq, :] - m)
        o = o_ext[:, :HEAD_DIM] / denom
        for c in range(group):
            hd = kv * group + c
            o_ref[:, hd * HEAD_DIM:(hd + 1) * HEAD_DIM] = o[c * tq:(c + 1) * tq].astype(o_ref.dtype)
        if has_lse:
            lse_tile = jnp.where(lane == kv, m + jnp.log(denom), lse_tile)
    if has_lse:
        lse_ref[...] = lse_tile


def _banded_attention(q_src, k_src, v_src, *, dil, n_q, n_kv, hw, sink_rep, has_lse, name):
    tq = 128
    length = SEQ // dil
    nb = length // tq
    group = n_q // n_kv
    qw, kw = n_q * HEAD_DIM, n_kv * HEAD_DIM

    def chain_view(src):
        arr, _ = src
        return arr.reshape(length, dil * arr.shape[1])

    def spec(src, width, shift):
        arr, col = src
        per_row = arr.shape[1] // width
        return pl.BlockSpec(
            (tq, width), lambda r, i: (jnp.clip(i + shift, 0, nb - 1), r * per_row + col))

    win = tq + 2 * hw
    rows = jnp.arange(tq, dtype=jnp.int32)[:, None]
    cols = jnp.arange(win, dtype=jnp.int32)[None, :]
    band = jnp.where(jnp.abs(rows + hw - cols) <= hw, 0.0, NEG).astype(F32)

    in_specs = [spec(q_src, qw, 0),
                spec(k_src, kw, -1), spec(k_src, kw, 0), spec(k_src, kw, 1),
                spec(v_src, kw, -1), spec(v_src, kw, 0), spec(v_src, kw, 1),
                pl.BlockSpec((tq, win), lambda r, i: (0, 0))]
    args = [chain_view(q_src)] + [chain_view(k_src)] * 3 + [chain_view(v_src)] * 3 + [band]
    if sink_rep is not None:
        in_specs.append(pl.BlockSpec(sink_rep.shape, lambda r, i: (0, 0)))
        args.append(sink_rep)
    out_shape = [jax.ShapeDtypeStruct((length, dil * qw), BF16)]
    out_specs = [pl.BlockSpec((tq, qw), lambda r, i: (i, r))]
    if has_lse:
        out_shape.append(jax.ShapeDtypeStruct((length, dil * LANES), F32))
        out_specs.append(pl.BlockSpec((tq, LANES), lambda r, i: (i, r)))
    kernel = functools.partial(
        _banded_kernel, tq=tq, hw=hw, n_kv=n_kv, group=group, length=length, q_axis=1,
        has_sink=sink_rep is not None, has_lse=has_lse)
    outs = pl.pallas_call(
        kernel, grid=(dil, nb), in_specs=in_specs, out_specs=out_specs, out_shape=out_shape,
        compiler_params=_params("parallel", "parallel"), name=name,
    )(*args)
    o = outs[0].reshape(SEQ, qw)
    if has_lse:
        return o, outs[1].reshape(SEQ, LANES)
    return o


def _flash_kernel(*refs, tk, n_cast):
    q_ref, k_ref, v_ref = refs[:3]
    w_refs = refs[3:3 + n_cast]
    o_ref = refs[3 + n_cast]
    wb_refs = refs[4 + n_cast:4 + 2 * n_cast]
    m_ref, acc_ref = refs[4 + 2 * n_cast:]
    tq = q_ref.shape[0]
    n_chunks = k_ref.shape[0] // tk
    n_col = tk // LANES
    m_ref[...] = jnp.full((tq, LANES), NEG, F32)
    acc_ref[...] = jnp.zeros((tq, B_HEAD_PAD), F32)
    q = q_ref[...]
    for w_ref, wb_ref in zip(w_refs, wb_refs):
        wb_ref[...] = w_ref[...].astype(BF16)

    for c in range(n_chunks):
        k = k_ref[c * tk:(c + 1) * tk, :]
        v = v_ref[c * tk:(c + 1) * tk, :]
        s = lax.dot_general(q, k, (((1,), (1,)), ((), ())), preferred_element_type=F32)
        mx = s[:, :LANES]
        for cb in range(1, n_col):
            mx = jnp.maximum(mx, s[:, cb * LANES:(cb + 1) * LANES])
        m_old = m_ref[...]
        m_new = jnp.maximum(m_old, jnp.max(mx, axis=-1, keepdims=True))
        alpha = jnp.exp2(m_old - m_new)
        p = jnp.exp2(s - jnp.concatenate([m_new] * n_col, axis=1)).astype(BF16)
        pv = jnp.dot(p, v, preferred_element_type=F32)
        acc_ref[...] = acc_ref[...] * jnp.concatenate([alpha, alpha], axis=1) + pv
        m_ref[...] = m_new

    acc = acc_ref[...]
    o_ref[...] = (acc[:, :B_NOPE] / acc[:, B_NOPE:]).astype(o_ref.dtype)


def _dense_attention(q, k, v_ext, casts):
    tq, tk = 1024, 256
    nq = SEQ // tq
    kv_spec = pl.BlockSpec((SEQ, B_HEAD_PAD), lambda h, i: (0, h))
    w_specs, wb_specs, wb_shapes = [], [], []
    for w, first, n_layers, rows in casts:
        _, kdim, ndim = w.shape
        per_layer = kdim // rows
        last = n_layers * per_layer - 1
        assert last < B_HEADS * nq and kdim % rows == 0

        def slab(h, i, per_layer=per_layer, last=last):
            s = jnp.minimum(h * nq + i, last)
            return s // per_layer, s % per_layer

        w_specs.append(pl.BlockSpec(
            (None, rows, ndim),
            lambda h, i, slab=slab, first=first: (first + slab(h, i)[0], slab(h, i)[1], 0)))
        wb_specs.append(pl.BlockSpec(
            (None, rows, ndim), lambda h, i, slab=slab: (slab(h, i)[0], slab(h, i)[1], 0)))
        wb_shapes.append(jax.ShapeDtypeStruct((n_layers, kdim, ndim), BF16))
    outs = pl.pallas_call(
        functools.partial(_flash_kernel, tk=tk, n_cast=len(casts)),
        grid=(B_HEADS, nq),
        in_specs=[pl.BlockSpec((tq, B_HEAD_PAD), lambda h, i: (i, h)), kv_spec, kv_spec] + w_specs,
        out_specs=[pl.BlockSpec((tq, B_NOPE), lambda h, i: (i, h))] + wb_specs,
        out_shape=[jax.ShapeDtypeStruct((SEQ, B_HEADS * B_NOPE), BF16)] + wb_shapes,
        scratch_shapes=[pltpu.VMEM((tq, LANES), F32), pltpu.VMEM((tq, B_HEAD_PAD), F32)],
        compiler_params=_params("arbitrary", "arbitrary"),
        name="b_flash_attention",
    )(q, k, v_ext, *[c[0] for c in casts])
    return outs[0], outs[1:]


def _out_proj_kernel(o_ref, w_ref, h_ref, out_ref, *, n_sub):
    sub = o_ref.shape[0] // n_sub
    for c in range(n_sub):
        rows = slice(c * sub, (c + 1) * sub)
        out_ref[rows, :] = h_ref[rows, :] + jnp.dot(o_ref[rows, :], w_ref[...],
                                                    preferred_element_type=F32)


def _out_projection(o, w, slot, h, name):
    tm = 1024
    row = pl.BlockSpec((tm, D_MODEL), lambda i: (i, 0))
    return pl.pallas_call(
        functools.partial(_out_proj_kernel, n_sub=4),
        grid=(SEQ // tm,),
        in_specs=[row,
                  pl.BlockSpec((None, D_MODEL, D_MODEL), lambda i: (slot, 0, 0),
                               pipeline_mode=pl.Buffered(1)),
                  row],
        out_specs=row,
        out_shape=jax.ShapeDtypeStruct((SEQ, D_MODEL), F32),
        compiler_params=_params("parallel"),
        name=name,
    )(o, w, h)


def _ffn_kernel(x_ref, g_ref, wg_ref, wu_ref, wd_ref, o_ref, xn_ref):
    @pl.when(pl.program_id(1) == 0)
    def _():
        x = x_ref[...]
        xn_ref[...] = (_rms_scale(x, D_MODEL) * g_ref[...]).astype(BF16)
        o_ref[...] = x

    xn = xn_ref[...]
    gate = jnp.dot(xn, wg_ref[...], preferred_element_type=F32)
    up = jnp.dot(xn, wu_ref[...], preferred_element_type=F32)
    act = (gate * jax.nn.sigmoid(gate) * up).astype(BF16)
    o_ref[...] += jnp.dot(act, wd_ref[...], preferred_element_type=F32)


def _ffn(h, g, wg, wu, wd, layer):
    tm, tf = 1024, 512
    return pl.pallas_call(
        _ffn_kernel,
        grid=(SEQ // tm, D_FF // tf),
        in_specs=[pl.BlockSpec((tm, D_MODEL), lambda i, f: (i, 0)),
                  pl.BlockSpec((1, D_MODEL), lambda i, f: (0, 0)),
                  pl.BlockSpec((None, D_MODEL, tf), lambda i, f: (layer, 0, f)),
                  pl.BlockSpec((None, D_MODEL, tf), lambda i, f: (layer, 0, f)),
                  pl.BlockSpec((None, tf, D_MODEL), lambda i, f: (layer, f, 0))],
        out_specs=pl.BlockSpec((tm, D_MODEL), lambda i, f: (i, 0)),
        out_shape=jax.ShapeDtypeStruct((SEQ, D_MODEL), F32),
        scratch_shapes=[pltpu.VMEM((tm, D_MODEL), BF16)],
        compiler_params=_params("parallel", "arbitrary"),
        name="ffn_swiglu",
    )(h, g, wg, wu, wd)


def _ple_kernel(x_ref, g_ref, wg_ref, p_ref, wp_ref, o_ref, *, n_sub):
    sub = x_ref.shape[0] // n_sub
    for c in range(n_sub):
        rows = slice(c * sub, (c + 1) * sub)
        x = x_ref[rows, :]
        xn = (_rms_scale(x, D_MODEL) * g_ref[...]).astype(BF16)
        gate = jnp.dot(xn, wg_ref[...], preferred_element_type=F32)
        proj = jnp.dot(p_ref[rows, :].astype(BF16), wp_ref[...], preferred_element_type=F32)
        o_ref[rows, :] = x + jax.nn.sigmoid(gate) * proj


def _ple(h, g, w_gate, gate_slot, p, w_proj, layer):
    tm = 1024
    resident = pl.Buffered(1)
    return pl.pallas_call(
        functools.partial(_ple_kernel, n_sub=4),
        grid=(SEQ // tm,),
        in_specs=[pl.BlockSpec((tm, D_MODEL), lambda i: (i, 0)),
                  pl.BlockSpec((1, D_MODEL), lambda i: (0, 0)),
                  pl.BlockSpec((None, D_MODEL, D_MODEL), lambda i: (gate_slot, 0, 0),
                               pipeline_mode=resident),
                  pl.BlockSpec((None, tm, PLE_DIM), lambda i: (layer, i, 0)),
                  pl.BlockSpec((None, PLE_DIM, D_MODEL), lambda i: (layer, 0, 0),
                               pipeline_mode=resident)],
        out_specs=pl.BlockSpec((tm, D_MODEL), lambda i: (i, 0)),
        out_shape=jax.ShapeDtypeStruct((SEQ, D_MODEL), F32),
        compiler_params=_params("parallel"),
        name="ple_gate",
    )(h, g, w_gate, p, w_proj)


def _mixer_a(h, g_mix, w_in, slot, gq, gk, sink, w_o, tables):
    nq, nk = A_HEADS * HEAD_DIM, A_KV_HEADS * HEAD_DIM
    scale = 1.0 / math.sqrt(HEAD_DIM)
    head_gain = jnp.concatenate([jnp.tile(gq * scale, A_HEADS), jnp.tile(gk, A_KV_HEADS)])[None, :]
    qk, v = _qkv_projection(h, g_mix, w_in, slot, nq + nk, nk, head_gain, tables, "a")
    sink_rep = jnp.broadcast_to(jnp.repeat(sink, 128)[:, None], (A_HEADS * 128, LANES))
    o = _banded_attention(
        (qk, 0), (qk, nq // nk), (v, 0), dil=1, n_q=A_HEADS, n_kv=A_KV_HEADS, hw=A_HALF_WINDOW,
        sink_rep=sink_rep, has_lse=False, name="a_banded_attention")
    return _out_projection(o, w_o, slot, h, "a_out_proj")


def _mixer_b(h, g_mix, w_in, g_qlat, g_kvlat, w_q_up, w_kv_up, slot, gq, gk, w_o, tables, casts):
    scale = math.log2(math.e) / math.sqrt(B_QK)
    half = B_ROPE // 2
    n_lat = B_Q_RANK + B_KV_RANK
    w_in = w_in[slot]
    w_in_pad = jnp.concatenate([w_in[:, :n_lat], _rotary_slab(w_in[:, n_lat:], half)], axis=1)[None]
    tm = 1024
    lat = _norm_matmul(
        h, 0, D_MODEL, g_mix, w_in_pad, 0, 0, 1, tm=tm, tn=B_IN_PAD, extra=(), extra_specs=[],
        epilogue=_plain_epilogue,
        out_shape=jax.ShapeDtypeStruct((SEQ, B_IN_PAD), F32),
        out_specs=pl.BlockSpec((tm, B_IN_PAD), lambda i, j: (i, j)),
        name="b_latent_proj")

    tab = pl.BlockSpec((tm, LANES), lambda i, j: (i, 0))
    tn = 1024
    n_slab = B_HEADS * B_HEAD_PAD

    def head_slab(x):
        return jnp.concatenate([x[..., :B_NOPE], _rotary_slab(x[..., B_NOPE:], half)], axis=-1)

    wq = head_slab(w_q_up[slot].reshape(B_Q_RANK, B_HEADS, B_QK)).reshape(B_Q_RANK, n_slab)[None]
    gq_slab = jnp.tile(head_slab(gq * scale), B_HEADS)[None, :]
    q = _norm_matmul(
        lat, 0, B_Q_RANK, g_qlat, wq, 0, 0, n_slab // tn, tm=tm, tn=tn,
        extra=(gq_slab,) + tuple(tables),
        extra_specs=[pl.BlockSpec((1, tn), lambda i, j: (0, j)), tab, tab],
        epilogue=_b_q_epilogue,
        out_shape=jax.ShapeDtypeStruct((SEQ, n_slab), BF16),
        out_specs=pl.BlockSpec((tm, tn), lambda i, j: (i, j)),
        name="b_q_proj")

    gk_slab = head_slab(gk)[None, :]
    slab_out = pl.BlockSpec((tm, tn), lambda i, j: (i, j))
    k, v_ext = _norm_matmul(
        lat, 1, B_KV_RANK, g_kvlat, w_kv_up, slot, 0, n_slab // tn, tm=tm, tn=tn,
        extra=(gk_slab, lat) + tuple(tables),
        extra_specs=[pl.BlockSpec((1, B_HEAD_PAD), lambda i, j: (0, 0)),
                     pl.BlockSpec((tm, LANES), lambda i, j: (i, n_lat // LANES)),
                     tab, tab],
        epilogue=_b_kv_epilogue,
        out_shape=[jax.ShapeDtypeStruct((SEQ, n_slab), BF16)] * 2,
        out_specs=[slab_out, slab_out],
        name="b_kv_proj")
    o, cast_weights = _dense_attention(q, k, v_ext, casts)
    return _out_projection(o, w_o, slot, h, "b_out_proj"), cast_weights


def _dilated_kernel(q0_ref, q1_ref, q2_ref, kp_ref, kc_ref, kn_ref, vp_ref, vc_ref, vn_ref,
                    band_ref, o_ref, *scratch):
    i = pl.program_id(0)
    blk = q0_ref.shape[0]
    band = band_ref[...]
    win = band.shape[1]
    ones_blk = jnp.ones((win, LANES), BF16)
    col = lax.broadcasted_iota(jnp.int32, (1, win), 1)
    n4, n16 = blk // HOP, blk // (HOP * HOP)
    scratch = list(scratch)
    take = lambda k: [scratch.pop(0) for _ in range(k)]
    o_s, l_s = take(C_GROUPS), take(C_GROUPS)
    hop1 = take(8)
    hop2 = take(7)
    o2_hop1, l2_hop1, merged_tok = take(3)

    for src, dst in zip((q1_ref, q2_ref, kp_ref, kc_ref, kn_ref, vp_ref, vc_ref, vn_ref), hop1):
        for r in range(HOP):
            dst[r * n4:(r + 1) * n4, :] = src[pl.ds(r, n4, stride=HOP), :]
    for src, dst in zip(hop1[1:], hop2):
        for r in range(HOP * HOP):
            dst[r * n16:(r + 1) * n16, :] = src[pl.ds((r % HOP) * n4 + r // HOP, n16, stride=HOP), :]
    operands = ((q0_ref, kp_ref, kc_ref, kn_ref, vp_ref, vc_ref, vn_ref),
                (hop1[0],) + tuple(hop1[2:]),
                tuple(hop2))

    def attend(q, k_win, v_win, bias):
        v_ext = jnp.concatenate([v_win, ones_blk], axis=1)
        s = lax.dot_general(q, k_win, (((1,), (1,)), ((), ())), preferred_element_type=F32) + bias
        mx = jnp.maximum(s[:, :LANES], s[:, LANES:])
        m = jnp.broadcast_to(jnp.max(mx, axis=-1, keepdims=True), mx.shape)
        p = jnp.exp(s - jnp.concatenate([m, m], axis=1)).astype(BF16)
        o_ext = jnp.dot(p, v_ext, preferred_element_type=F32)
        denom = o_ext[:, HEAD_DIM:]
        return o_ext[:, :HEAD_DIM] / denom, m + jnp.log(denom)

    for g, ((window, d), refs) in enumerate(zip(C_PATTERNS, operands)):
        q_ref, kprev, kcur, knext, vprev, vcur, vnext = refs
        n = blk // d
        rq = min(n, win - 2 * C_HALF)
        pad_rows = win - min(n + 2 * C_HALF, win)
        tiles = []
        for r in range(d):
            def chain_window(prev_ref, cur_ref, next_ref):
                parts = [prev_ref[(r + 1) * n - C_HALF:(r + 1) * n, :], cur_ref[r * n:(r + 1) * n, :],
                         next_ref[r * n:r * n + C_HALF, :]]
                if pad_rows:
                    parts.append(jnp.zeros((pad_rows, LANES), F32))
                return jnp.concatenate(parts, axis=0).astype(BF16)

            k_chain = chain_window(kprev, kcur, knext)
            v_chain = chain_window(vprev, vcur, vnext)
            for sb in range(n // rq):
                kpos = i * n + sb * rq - C_HALF + col
                edge = jnp.where((kpos >= 0) & (kpos < SEQ // d), 0.0, NEG)
                dst = slice(r * n + sb * rq, r * n + (sb + 1) * rq)
                tiles.append((dst, q_ref[dst, :].astype(BF16), k_chain[sb * rq:sb * rq + win],
                              v_chain[sb * rq:sb * rq + win], band[:rq] + edge))
        full = win - 2 * C_HALF
        batch = 1 if rq == full else 4 * full // rq
        for b0 in range(0, len(tiles), batch):
            group_tiles = tiles[b0:b0 + batch]
            scores = [lax.dot_general(q, k_win, (((1,), (1,)), ((), ())),
                                      preferred_element_type=F32) + bias
                      for _, q, k_win, _, bias in group_tiles]
            probs = []
            for s in scores:
                mx = jnp.maximum(s[:, :LANES], s[:, LANES:])
                m = jnp.broadcast_to(jnp.max(mx, axis=-1, keepdims=True), mx.shape)
                probs.append((m, jnp.exp(s - jnp.concatenate([m, m], axis=1)).astype(BF16)))
            for (dst, _, _, v_win, _), (m, p) in zip(group_tiles, probs):
                o_ext = jnp.dot(p, jnp.concatenate([v_win, ones_blk], axis=1),
                                preferred_element_type=F32)
                denom = o_ext[:, HEAD_DIM:]
                o_s[g][dst, :] = o_ext[:, :HEAD_DIM] / denom
                l_s[g][dst, :] = m + jnp.log(denom)

    for src, dst in ((o_s[2], o2_hop1), (l_s[2], l2_hop1)):
        for r in range(HOP * HOP):
            dst[pl.ds((r % HOP) * n4 + r // HOP, n16, stride=HOP), :] = src[r * n16:(r + 1) * n16, :]
    for r in range(HOP):
        rows = slice(r * n4, (r + 1) * n4)
        tok = pl.ds(r, n4, stride=HOP)
        l0, l1, l2 = l_s[0][tok, :], l_s[1][rows, :], l2_hop1[rows, :]
        m = jnp.maximum(jnp.maximum(l0, l1), l2)
        e0, e1, e2 = jnp.exp(l0 - m), jnp.exp(l1 - m), jnp.exp(l2 - m)
        merged = (e0 * o_s[0][tok, :] + e1 * o_s[1][rows, :] + e2 * o2_hop1[rows, :]) / (e0 + e1 + e2)
        merged_tok[tok, :] = merged
    o_ref[...] = merged_tok[...].astype(o_ref.dtype)


def _dilated_attention(qk, v):
    blk = 1024
    nb = SEQ // blk
    win = 2 * LANES
    rows = jnp.arange(LANES, dtype=jnp.int32)[:, None]
    cols = jnp.arange(win, dtype=jnp.int32)[None, :]
    band = jnp.where(jnp.abs(rows + C_HALF - cols) <= C_HALF, 0.0, NEG).astype(F32)

    def slab(col0, shift):
        return pl.BlockSpec((blk, HEAD_DIM),
                            lambda i, h: (jnp.clip(i + shift, 0, nb - 1), col0 + h))

    k_col = C_GROUPS * C_HEADS
    in_specs = [slab(g * C_HEADS, 0) for g in range(C_GROUPS)]
    in_specs += [slab(k_col, -1), slab(k_col, 0), slab(k_col, 1)]
    in_specs += [slab(0, -1), slab(0, 0), slab(0, 1)]
    in_specs.append(pl.BlockSpec((LANES, win), lambda i, h: (0, 0)))
    return pl.pallas_call(
        _dilated_kernel,
        grid=(nb, C_HEADS),
        in_specs=in_specs,
        out_specs=pl.BlockSpec((blk, HEAD_DIM), lambda i, h: (i, h)),
        out_shape=jax.ShapeDtypeStruct((SEQ, C_HEADS * HEAD_DIM), BF16),
        scratch_shapes=[pltpu.VMEM((blk, LANES), F32)] * (2 * C_GROUPS + 8 + 7 + 3),
        compiler_params=_params("parallel", "parallel"),
        name="c_dilated_attention",
    )(qk, qk, qk, qk, qk, qk, v, v, v, band)


def _mixer_c(h, g_mix, w_in, slot, gq, gk, w_o, tables):
    n_qh = C_GROUPS * C_HEADS
    nq = n_qh * HEAD_DIM
    nkv = C_HEADS * HEAD_DIM
    scale = 1.0 / math.sqrt(HEAD_DIM)
    head_gain = jnp.concatenate([jnp.tile(gq * scale, n_qh), jnp.tile(gk, C_HEADS)])[None, :]
    qk, v = _qkv_projection(h, g_mix, w_in, slot, nq + nkv, nkv, head_gain, tables, "c", dtype=F32)
    o = _dilated_attention(qk, v)
    return _out_projection(o, w_o, slot, h, "c_out_proj")


def kernel(x, p, positions, g_mix, g_ffn, g_ple, w_ple_gate, w_ple_proj,
           w_ffn_gate, w_ffn_up, w_ffn_down,
           a_w_in, a_q_norm, a_k_norm, a_sink, a_w_o,
           b_w_in, b_q_lat_norm, b_kv_lat_norm, b_w_q_up, b_w_kv_up, b_q_norm, b_k_norm, b_w_o,
           c_w_in, c_q_norm, c_k_norm, c_w_o):
    h = x.reshape(SEQ, D_MODEL)
    p = p.reshape(DEPTH, SEQ, PLE_DIM)
    pos_col = positions.reshape(SEQ, 1)
    tables_ac = _tables_adjacent(pos_col, PARTIAL_ROT)
    tables_b = _tables_split(pos_col, B_ROPE)
    (w_ple_proj, a_w_in, a_w_o, b_w_in, b_w_q_up, b_w_kv_up, b_w_o, c_w_o) = [
        w.astype(BF16) for w in (w_ple_proj, a_w_in, a_w_o, b_w_in, b_w_q_up, b_w_kv_up, b_w_o, c_w_o)]
    late = {"ffn_gate": (w_ffn_gate, 1, DEPTH - 1, 64), "ffn_up": (w_ffn_up, 1, DEPTH - 1, 64),
            "ffn_down": (w_ffn_down, 1, DEPTH - 1, 176), "ple_gate": (w_ple_gate, 1, DEPTH - 1, 64),
            "c_in": (c_w_in, 0, 1, 16)}
    early = {name: w[:1].astype(BF16) for name, (w, first, _, _) in late.items() if first > 0}
    cast = {}

    def weight(name, layer):
        first = late[name][1]
        if layer < first:
            return early[name], layer
        return cast[name], layer - first

    for i in range(DEPTH):
        kind, slot = i % 3, i // 3
        gm = g_mix[i][None, :]
        if kind == 0:
            h = _mixer_a(h, gm, a_w_in, slot, a_q_norm[slot], a_k_norm[slot], a_sink[slot],
                         a_w_o, tables_ac)
        elif kind == 1:
            h, cast_weights = _mixer_b(
                h, gm, b_w_in, b_q_lat_norm[slot][None, :], b_kv_lat_norm[slot][None, :],
                b_w_q_up, b_w_kv_up, slot, b_q_norm[slot], b_k_norm[slot], b_w_o, tables_b,
                list(late.values()))
            cast = dict(zip(late, cast_weights))
        else:
            h = _mixer_c(h, gm, *weight("c_in", slot), c_q_norm[slot], c_k_norm[slot], c_w_o,
                         tables_ac)
        (wg, l), (wu, _), (wd, _) = weight("ffn_gate", i), weight("ffn_up", i), weight("ffn_down", i)
        h = _ffn(h, g_ffn[i][None, :], wg, wu, wd, l)
        wpg, l = weight("ple_gate", i)
        h = _ple(h, g_ple[i][None, :], wpg, l, p, w_ple_proj, i)
    return h.reshape(1, SEQ, D_MODEL)
```

```python
import functools
import math

import jax
import jax.numpy as jnp
from jax import lax
from jax.experimental import pallas as pl
from jax.experimental.pallas import tpu as pltpu

F32 = jnp.float32
BF16 = jnp.bfloat16

SEQ = 8192
D_MODEL = 2048
DEPTH = 4
HEAD_DIM = 128
ROPE_THETA = 500000.0
PARTIAL_ROT = HEAD_DIM // 4
NORM_EPS = 1e-6
NEG = -1e30
LANES = 128
HALF_LANES = LANES // 2

A_HEADS = 16
A_KV_HEADS = 4
A_HALF_WINDOW = 128
A_BLOCK = 128
B_HEADS = 16
B_Q_RANK = 512
B_KV_RANK = 512
B_NOPE = 128
B_ROPE = 64
B_QK = B_NOPE + B_ROPE
B_HEAD_PAD = 256
B_IN_PAD = B_Q_RANK + B_KV_RANK + LANES
C_PATTERNS = ((128, 1), (512, 4), (2048, 16))
C_GROUPS = 3
C_HEADS = 16
C_HALF = 64
assert all(window // 2 // dil == C_HALF for window, dil in C_PATTERNS)
HOP = 4
assert tuple(dil for _, dil in C_PATTERNS) == (1, HOP, HOP * HOP)
D_FF = 5632
PLE_DIM = 256

VMEM_LIMIT = 56 * 1024 * 1024


def _params(*sem):
    return pltpu.CompilerParams(dimension_semantics=sem, vmem_limit_bytes=VMEM_LIMIT)


def _rms_scale(x, width):
    ss = jnp.sum(x * x, axis=-1, keepdims=True)
    return x * lax.rsqrt(ss * (1.0 / width) + NORM_EPS)


def _rope_adjacent(y, cos, sin_lo, sin_hi, half):
    return (y * cos + pltpu.roll(y, LANES - half, 1) * sin_lo
            + pltpu.roll(y, half, 1) * sin_hi)


def _rope_split(y, cos, sin):
    return y * cos + pltpu.roll(y, HALF_LANES, 1) * sin


def _rotary_slab(x, half):
    pad = jnp.zeros(x.shape[:-1] + (HALF_LANES - half,), x.dtype)
    return jnp.concatenate([x[..., :half], pad, x[..., half:], pad], axis=-1)


def _cast_kernel(w_ref, o_ref):
    o_ref[...] = w_ref[...].astype(o_ref.dtype)


def _cast_layers(w, n_layers, rows):
    _, kdim, ndim = w.shape
    assert kdim % rows == 0
    spec = pl.BlockSpec((None, rows, ndim), lambda l, r: (l, r, 0))
    return pl.pallas_call(
        _cast_kernel,
        grid=(n_layers, kdim // rows),
        in_specs=[spec],
        out_specs=spec,
        out_shape=jax.ShapeDtypeStruct((n_layers, kdim, ndim), BF16),
        compiler_params=_params("parallel", "parallel"),
        name="cast_early_weights",
    )(w)


def _rope_table_kernel(pos_ref, inv_ref, sign_ref, cos_ref, *sin_refs):
    ang = pos_ref[...].astype(F32) * inv_ref[...]
    cos_ref[...] = jnp.cos(ang)
    s = jnp.sin(ang)
    for r, sin_ref in enumerate(sin_refs):
        sin_ref[...] = s * sign_ref[r:r + 1, :]


def _rope_tables(pos_col, inv_lanes, signs):
    tm = 1024
    n = signs.shape[0]
    tab = pl.BlockSpec((tm, LANES), lambda i: (i, 0))
    return pl.pallas_call(
        _rope_table_kernel,
        grid=(SEQ // tm,),
        in_specs=[pl.BlockSpec((tm, 1), lambda i: (i, 0)),
                  pl.BlockSpec((1, LANES), lambda i: (0, 0)),
                  pl.BlockSpec((n, LANES), lambda i: (0, 0))],
        out_specs=[tab] * (n + 1),
        out_shape=[jax.ShapeDtypeStruct((SEQ, LANES), F32)] * (n + 1),
        compiler_params=_params("parallel"),
        name="rope_tables",
    )(pos_col, inv_lanes, signs)


def _inv_freq(rot_dim):
    half = rot_dim // 2
    return ROPE_THETA ** (-jnp.arange(half, dtype=F32) * 2.0 / rot_dim)


def _tables_adjacent(pos_col, rot_dim):
    half = rot_dim // 2
    inv = _inv_freq(rot_dim)
    rest = jnp.zeros((LANES - rot_dim,), F32)
    zero, one = jnp.zeros((half,), F32), jnp.ones((half,), F32)
    inv_l = jnp.concatenate([inv, inv, rest])[None, :]
    signs = jnp.stack([jnp.concatenate([-one, zero, rest]), jnp.concatenate([zero, one, rest])])
    return _rope_tables(pos_col, inv_l, signs)


def _tables_split(pos_col, rot_dim):
    half = rot_dim // 2
    inv = _inv_freq(rot_dim)
    one = jnp.ones((half,), F32)
    inv_l = _rotary_slab(jnp.concatenate([inv, inv]), half)[None, :]
    signs = _rotary_slab(jnp.concatenate([-one, one]), half)[None, :]
    return _rope_tables(pos_col, inv_l, signs)


def _norm_matmul_kernel(x_ref, g_ref, w_ref, *rest, n_extra, epilogue, n_sub):
    extra = rest[:n_extra]
    outs = rest[n_extra:-1]
    xn_ref = rest[-1]

    sub = xn_ref.shape[0] // n_sub

    def column_tile(first):
        for c in range(n_sub):
            rows = slice(c * sub, (c + 1) * sub)
            if first:
                x = x_ref[rows, :]
                xn_ref[rows, :] = (_rms_scale(x, x.shape[-1]) * g_ref[...]).astype(BF16)
            acc = jnp.dot(xn_ref[rows, :], w_ref[...], preferred_element_type=F32)
            epilogue(acc, rows, extra, outs)

    @pl.when(pl.program_id(1) == 0)
    def _():
        column_tile(True)

    @pl.when(pl.program_id(1) > 0)
    def _():
        column_tile(False)


def _norm_matmul(x, x_col, kin, gain, w, w_slot, w_col0, n_tiles, *, tm, tn, extra, extra_specs,
                 epilogue, out_shape, out_specs, name, n_sub=4):
    m = x.shape[0]
    kernel = functools.partial(_norm_matmul_kernel, n_extra=len(extra), epilogue=epilogue,
                               n_sub=n_sub)
    return pl.pallas_call(
        kernel,
        grid=(m // tm, n_tiles),
        in_specs=[pl.BlockSpec((tm, kin), lambda i, j: (i, x_col)),
                  pl.BlockSpec((1, kin), lambda i, j: (0, 0)),
                  pl.BlockSpec((None, kin, tn), lambda i, j: (w_slot, 0, w_col0 + j))]
        + list(extra_specs),
        out_specs=out_specs,
        out_shape=out_shape,
        scratch_shapes=[pltpu.VMEM((tm, kin), BF16)],
        compiler_params=_params("parallel", "arbitrary"),
        name=name,
    )(x, gain, w, *extra)


def _plain_epilogue(acc, rows, extra, outs):
    outs[0][rows, :] = acc.astype(outs[0].dtype)


def _head_norm_rope_epilogue(acc, rows, extra, outs):
    gain_ref, cos_ref, slo_ref, shi_ref = extra
    (o_ref,) = outs
    cos, slo, shi = cos_ref[rows, :], slo_ref[rows, :], shi_ref[rows, :]
    for c in range(acc.shape[1] // HEAD_DIM):
        cols = slice(c * HEAD_DIM, (c + 1) * HEAD_DIM)
        y = _rms_scale(acc[:, cols], HEAD_DIM) * gain_ref[:, cols]
        o_ref[rows, cols] = _rope_adjacent(y, cos, slo, shi, PARTIAL_ROT // 2).astype(o_ref.dtype)


def _qkv_projection(h, g_mix, w, slot, n_qk, n_v, head_gain, tables, name, dtype=BF16):
    tm, tn = 1024, 512
    tab = pl.BlockSpec((tm, LANES), lambda i, j: (i, 0))
    qk = _norm_matmul(
        h, 0, D_MODEL, g_mix, w, slot, 0, n_qk // tn, tm=tm, tn=tn,
        extra=(head_gain,) + tuple(tables),
        extra_specs=[pl.BlockSpec((1, tn), lambda i, j: (0, j)), tab, tab, tab],
        epilogue=_head_norm_rope_epilogue,
        out_shape=jax.ShapeDtypeStruct((SEQ, n_qk), dtype),
        out_specs=pl.BlockSpec((tm, tn), lambda i, j: (i, j)),
        name=name + "_qk_proj")
    v = _norm_matmul(
        h, 0, D_MODEL, g_mix, w, slot, n_qk // tn, n_v // tn, tm=tm, tn=tn,
        extra=(), extra_specs=[], epilogue=_plain_epilogue,
        out_shape=jax.ShapeDtypeStruct((SEQ, n_v), dtype),
        out_specs=pl.BlockSpec((tm, tn), lambda i, j: (i, j)),
        name=name + "_v_proj")
    return qk, v


def _b_q_epilogue(acc, rows, extra, outs):
    gain_ref, cos_ref, sin_ref = extra
    (o_ref,) = outs
    cos, sin = cos_ref[rows, :], sin_ref[rows, :]
    for c in range(acc.shape[1] // B_HEAD_PAD):
        c0 = c * B_HEAD_PAD
        y = _rms_scale(acc[:, c0:c0 + B_HEAD_PAD], B_QK) * gain_ref[:, c0:c0 + B_HEAD_PAD]
        o_ref[rows, c0:c0 + B_NOPE] = y[:, :B_NOPE].astype(BF16)
        o_ref[rows, c0 + B_NOPE:c0 + B_HEAD_PAD] = _rope_split(y[:, B_NOPE:], cos, sin).astype(BF16)


def _b_kv_epilogue(acc, rows, extra, outs):
    gain_ref, krope_ref, cos_ref, sin_ref = extra
    k_ref, v_ref = outs
    cos, sin = cos_ref[rows, :], sin_ref[rows, :]
    kr = krope_ref[rows, :]
    kr_ss = jnp.sum(kr * kr, axis=-1, keepdims=True)
    g_nope = gain_ref[:, :B_NOPE]
    kr_rot = _rope_split(kr * gain_ref[:, B_NOPE:], cos, sin)
    ones_blk = jnp.ones((acc.shape[0], LANES), BF16)
    for c in range(acc.shape[1] // B_HEAD_PAD):
        c0 = c * B_HEAD_PAD
        y = acc[:, c0:c0 + B_NOPE]
        ss = jnp.sum(y * y, axis=-1, keepdims=True) + kr_ss
        rinv = lax.rsqrt(ss * (1.0 / B_QK) + NORM_EPS)
        k_ref[rows, c0:c0 + B_NOPE] = (y * rinv * g_nope).astype(BF16)
        k_ref[rows, c0 + B_NOPE:c0 + B_HEAD_PAD] = (kr_rot * rinv).astype(BF16)
        v_ref[rows, c0:c0 + B_NOPE] = acc[:, c0 + B_NOPE:c0 + B_HEAD_PAD].astype(BF16)
        v_ref[rows, c0 + B_NOPE:c0 + B_HEAD_PAD] = ones_blk


def _banded_kernel(q_ref, kp_ref, kc_ref, kn_ref, vp_ref, vc_ref, vn_ref, band_ref, sink_ref,
                   o_ref, *, hw, n_kv, group):
    i = pl.program_id(0)
    tq = q_ref.shape[0]
    win = tq + 2 * hw
    n_col = win // LANES
    kpos = i * tq - hw + lax.broadcasted_iota(jnp.int32, (1, win), 1)
    edge = jnp.where((kpos >= 0) & (kpos < SEQ), 0.0, NEG)
    bias = jnp.concatenate([band_ref[...] + edge] * group, axis=0)
    ones_blk = jnp.ones((win, LANES), BF16)

    scores = []
    for kv in range(n_kv):
        kcols = slice(kv * HEAD_DIM, (kv + 1) * HEAD_DIM)
        k_win = jnp.concatenate(
            [kp_ref[tq - hw:, kcols], kc_ref[:, kcols], kn_ref[:hw, kcols]], axis=0)
        q = jnp.concatenate(
            [q_ref[:, hd * HEAD_DIM:(hd + 1) * HEAD_DIM]
             for hd in range(kv * group, (kv + 1) * group)], axis=0)
        scores.append(
            lax.dot_general(q, k_win, (((1,), (1,)), ((), ())), preferred_element_type=F32) + bias)
    probs = []
    for kv, s in enumerate(scores):
        mx = s[:, :LANES]
        for cb in range(1, n_col):
            mx = jnp.maximum(mx, s[:, cb * LANES:(cb + 1) * LANES])
        sink = sink_ref[kv * group * tq:(kv + 1) * group * tq, :]
        m = jnp.maximum(jnp.broadcast_to(jnp.max(mx, axis=-1, keepdims=True), mx.shape), sink)
        probs.append((m, jnp.exp(s - jnp.concatenate([m] * n_col, axis=1)).astype(BF16)))
    for kv, (m, p) in enumerate(probs):
        kcols = slice(kv * HEAD_DIM, (kv + 1) * HEAD_DIM)
        v_win = jnp.concatenate(
            [vp_ref[tq - hw:, kcols], vc_ref[:, kcols], vn_ref[:hw, kcols]], axis=0)
        o_ext = jnp.dot(p, jnp.concatenate([v_win, ones_blk], axis=1), preferred_element_type=F32)
        sink = sink_ref[kv * group * tq:(kv + 1) * group * tq, :]
        o = o_ext[:, :HEAD_DIM] / (o_ext[:, HEAD_DIM:] + jnp.exp(sink - m))
        for c in range(group):
            hd = kv * group + c
            o_ref[:, hd * HEAD_DIM:(hd + 1) * HEAD_DIM] = o[c * tq:(c + 1) * tq].astype(o_ref.dtype)


def _banded_attention(qk, v, sink, *, n_q, n_kv, hw, name):
    tq = A_BLOCK
    nb = SEQ // tq
    group = n_q // n_kv
    qw, kw = n_q * HEAD_DIM, n_kv * HEAD_DIM
    win = tq + 2 * hw
    rows = jnp.arange(tq, dtype=jnp.int32)[:, None]
    cols = jnp.arange(win, dtype=jnp.int32)[None, :]
    band = jnp.where(jnp.abs(rows + hw - cols) <= hw, 0.0, NEG).astype(F32)
    sink_rep = jnp.broadcast_to(jnp.repeat(sink, tq)[:, None], (n_q * tq, LANES))

    def kv_spec(col, shift):
        return pl.BlockSpec((tq, kw), lambda i: (jnp.clip(i + shift, 0, nb - 1), col))

    k_col = qw // kw
    return pl.pallas_call(
        functools.partial(_banded_kernel, hw=hw, n_kv=n_kv, group=group),
        grid=(nb,),
        in_specs=[pl.BlockSpec((tq, qw), lambda i: (i, 0)),
                  kv_spec(k_col, -1), kv_spec(k_col, 0), kv_spec(k_col, 1),
                  kv_spec(0, -1), kv_spec(0, 0), kv_spec(0, 1),
                  pl.BlockSpec((tq, win), lambda i: (0, 0)),
                  pl.BlockSpec((n_q * tq, LANES), lambda i: (0, 0))],
        out_specs=pl.BlockSpec((tq, qw), lambda i: (i, 0)),
        out_shape=jax.ShapeDtypeStruct((SEQ, qw), BF16),
        compiler_params=_params("parallel"),
        name=name,
    )(qk, qk, qk, qk, v, v, v, band, sink_rep)


def _flash_kernel(*refs, tk, n_cast):
    q_ref, k_ref, v_ref = refs[:3]
    w_refs = refs[3:3 + n_cast]
    o_ref = refs[3 + n_cast]
    wb_refs = refs[4 + n_cast:4 + 2 * n_cast]
    m_ref, acc_ref = refs[4 + 2 * n_cast:]
    tq = q_ref.shape[0]
    n_chunks = k_ref.shape[0] // tk
    n_col = tk // LANES
    m_ref[...] = jnp.full((tq, LANES), NEG, F32)
    acc_ref[...] = jnp.zeros((tq, B_HEAD_PAD), F32)
    q = q_ref[...]
    for w_ref, wb_ref in zip(w_refs, wb_refs):
        wb_ref[...] = w_ref[...].astype(BF16)

    for c in range(n_chunks):
        k = k_ref[c * tk:(c + 1) * tk, :]
        v = v_ref[c * tk:(c + 1) * tk, :]
        s = lax.dot_general(q, k, (((1,), (1,)), ((), ())), preferred_element_type=F32)
        mx = s[:, :LANES]
        for cb in range(1, n_col):
            mx = jnp.maximum(mx, s[:, cb * LANES:(cb + 1) * LANES])
        m_old = m_ref[...]
        m_new = jnp.maximum(m_old, jnp.max(mx, axis=-1, keepdims=True))
        alpha = jnp.exp2(m_old - m_new)
        p = jnp.exp2(s - jnp.concatenate([m_new] * n_col, axis=1)).astype(BF16)
        pv = jnp.dot(p, v, preferred_element_type=F32)
        acc_ref[...] = acc_ref[...] * jnp.concatenate([alpha, alpha], axis=1) + pv
        m_ref[...] = m_new

    acc = acc_ref[...]
    o_ref[...] = (acc[:, :B_NOPE] / acc[:, B_NOPE:]).astype(o_ref.dtype)


def _dense_attention(q, k, v_ext, casts):
    tq, tk = 1024, 256
    nq = SEQ // tq
    kv_spec = pl.BlockSpec((SEQ, B_HEAD_PAD), lambda h, i: (0, h))
    w_specs, wb_specs, wb_shapes = [], [], []
    for w, first, n_layers, rows in casts:
        _, kdim, ndim = w.shape
        per_layer = kdim // rows
        last = n_layers * per_layer - 1
        assert last < B_HEADS * nq and kdim % rows == 0

        def slab(h, i, per_layer=per_layer, last=last):
            s = jnp.minimum(h * nq + i, last)
            return s // per_layer, s % per_layer

        w_specs.append(pl.BlockSpec(
            (None, rows, ndim),
            lambda h, i, slab=slab, first=first: (first + slab(h, i)[0], slab(h, i)[1], 0)))
        wb_specs.append(pl.BlockSpec(
            (None, rows, ndim), lambda h, i, slab=slab: (slab(h, i)[0], slab(h, i)[1], 0)))
        wb_shapes.append(jax.ShapeDtypeStruct((n_layers, kdim, ndim), BF16))
    outs = pl.pallas_call(
        functools.partial(_flash_kernel, tk=tk, n_cast=len(casts)),
        grid=(B_HEADS, nq),
        in_specs=[pl.BlockSpec((tq, B_HEAD_PAD), lambda h, i: (i, h)), kv_spec, kv_spec] + w_specs,
        out_specs=[pl.BlockSpec((tq, B_NOPE), lambda h, i: (i, h))] + wb_specs,
        out_shape=[jax.ShapeDtypeStruct((SEQ, B_HEADS * B_NOPE), BF16)] + wb_shapes,
        scratch_shapes=[pltpu.VMEM((tq, LANES), F32), pltpu.VMEM((tq, B_HEAD_PAD), F32)],
        compiler_params=_params("arbitrary", "arbitrary"),
        name="b_flash_attention",
    )(q, k, v_ext, *[c[0] for c in casts])
    return outs[0], outs[1:]


def _out_proj_kernel(o_ref, w_ref, h_ref, out_ref, *, n_sub):
    sub = o_ref.shape[0] // n_sub
    for c in range(n_sub):
        rows = slice(c * sub, (c + 1) * sub)
        out_ref[rows, :] = h_ref[rows, :] + jnp.dot(o_ref[rows, :], w_ref[...],
                                                    preferred_element_type=F32)


def _out_projection(o, w, slot, h, name):
    tm = 1024
    row = pl.BlockSpec((tm, D_MODEL), lambda i: (i, 0))
    return pl.pallas_call(
        functools.partial(_out_proj_kernel, n_sub=4),
        grid=(SEQ // tm,),
        in_specs=[row,
                  pl.BlockSpec((None, D_MODEL, D_MODEL), lambda i: (slot, 0, 0),
                               pipeline_mode=pl.Buffered(1)),
                  row],
        out_specs=row,
        out_shape=jax.ShapeDtypeStruct((SEQ, D_MODEL), F32),
        compiler_params=_params("parallel"),
        name=name,
    )(o, w, h)


def _ffn_kernel(x_ref, g_ref, wg_ref, wu_ref, wd_ref, o_ref, xn_ref):
    @pl.when(pl.program_id(1) == 0)
    def _():
        x = x_ref[...]
        xn_ref[...] = (_rms_scale(x, D_MODEL) * g_ref[...]).astype(BF16)
        o_ref[...] = x

    xn = xn_ref[...]
    gate = jnp.dot(xn, wg_ref[...], preferred_element_type=F32)
    up = jnp.dot(xn, wu_ref[...], preferred_element_type=F32)
    act = (gate * jax.nn.sigmoid(gate) * up).astype(BF16)
    o_ref[...] += jnp.dot(act, wd_ref[...], preferred_element_type=F32)


def _ffn(h, g, wg, wu, wd, layer):
    tm, tf = 1024, 512
    return pl.pallas_call(
        _ffn_kernel,
        grid=(SEQ // tm, D_FF // tf),
        in_specs=[pl.BlockSpec((tm, D_MODEL), lambda i, f: (i, 0)),
                  pl.BlockSpec((1, D_MODEL), lambda i, f: (0, 0)),
                  pl.BlockSpec((None, D_MODEL, tf), lambda i, f: (layer, 0, f)),
                  pl.BlockSpec((None, D_MODEL, tf), lambda i, f: (layer, 0, f)),
                  pl.BlockSpec((None, tf, D_MODEL), lambda i, f: (layer, f, 0))],
        out_specs=pl.BlockSpec((tm, D_MODEL), lambda i, f: (i, 0)),
        out_shape=jax.ShapeDtypeStruct((SEQ, D_MODEL), F32),
        scratch_shapes=[pltpu.VMEM((tm, D_MODEL), BF16)],
        compiler_params=_params("parallel", "arbitrary"),
        name="ffn_swiglu",
    )(h, g, wg, wu, wd)


def _ple_kernel(x_ref, g_ref, wg_ref, p_ref, wp_ref, o_ref, *, n_sub):
    sub = x_ref.shape[0] // n_sub
    for c in range(n_sub):
        rows = slice(c * sub, (c + 1) * sub)
        x = x_ref[rows, :]
        xn = (_rms_scale(x, D_MODEL) * g_ref[...]).astype(BF16)
        gate = jnp.dot(xn, wg_ref[...], preferred_element_type=F32)
        proj = jnp.dot(p_ref[rows, :].astype(BF16), wp_ref[...], preferred_element_type=F32)
        o_ref[rows, :] = x + jax.nn.sigmoid(gate) * proj


def _ple(h, g, w_gate, gate_slot, p, w_proj, layer):
    tm = 1024
    resident = pl.Buffered(1)
    return pl.pallas_call(
        functools.partial(_ple_kernel, n_sub=4),
        grid=(SEQ // tm,),
        in_specs=[pl.BlockSpec((tm, D_MODEL), lambda i: (i, 0)),
                  pl.BlockSpec((1, D_MODEL), lambda i: (0, 0)),
                  pl.BlockSpec((None, D_MODEL, D_MODEL), lambda i: (gate_slot, 0, 0),
                               pipeline_mode=resident),
                  pl.BlockSpec((None, tm, PLE_DIM), lambda i: (layer, i, 0)),
                  pl.BlockSpec((None, PLE_DIM, D_MODEL), lambda i: (layer, 0, 0),
                               pipeline_mode=resident)],
        out_specs=pl.BlockSpec((tm, D_MODEL), lambda i: (i, 0)),
        out_shape=jax.ShapeDtypeStruct((SEQ, D_MODEL), F32),
        compiler_params=_params("parallel"),
        name="ple_gate",
    )(h, g, w_gate, p, w_proj)


def _mixer_a(h, g_mix, w_in, slot, gq, gk, sink, w_o, tables):
    nq, nk = A_HEADS * HEAD_DIM, A_KV_HEADS * HEAD_DIM
    scale = 1.0 / math.sqrt(HEAD_DIM)
    head_gain = jnp.concatenate([jnp.tile(gq * scale, A_HEADS), jnp.tile(gk, A_KV_HEADS)])[None, :]
    qk, v = _qkv_projection(h, g_mix, w_in, slot, nq + nk, nk, head_gain, tables, "a")
    o = _banded_attention(qk, v, sink, n_q=A_HEADS, n_kv=A_KV_HEADS, hw=A_HALF_WINDOW,
                          name="a_banded_attention")
    return _out_projection(o, w_o, slot, h, "a_out_proj")


def _mixer_b(h, g_mix, w_in, g_qlat, g_kvlat, w_q_up, w_kv_up, slot, gq, gk, w_o, tables, casts):
    scale = math.log2(math.e) / math.sqrt(B_QK)
    half = B_ROPE // 2
    n_lat = B_Q_RANK + B_KV_RANK
    w_in = w_in[slot]
    w_in_pad = jnp.concatenate([w_in[:, :n_lat], _rotary_slab(w_in[:, n_lat:], half)], axis=1)[None]
    tm = 1024
    lat = _norm_matmul(
        h, 0, D_MODEL, g_mix, w_in_pad, 0, 0, 1, tm=tm, tn=B_IN_PAD, extra=(), extra_specs=[],
        epilogue=_plain_epilogue,
        out_shape=jax.ShapeDtypeStruct((SEQ, B_IN_PAD), F32),
        out_specs=pl.BlockSpec((tm, B_IN_PAD), lambda i, j: (i, j)),
        name="b_latent_proj")

    tab = pl.BlockSpec((tm, LANES), lambda i, j: (i, 0))
    tn = 1024
    n_slab = B_HEADS * B_HEAD_PAD

    def head_slab(x):
        return jnp.concatenate([x[..., :B_NOPE], _rotary_slab(x[..., B_NOPE:], half)], axis=-1)

    wq = head_slab(w_q_up[slot].reshape(B_Q_RANK, B_HEADS, B_QK)).reshape(B_Q_RANK, n_slab)[None]
    gq_slab = jnp.tile(head_slab(gq * scale), B_HEADS)[None, :]
    q = _norm_matmul(
        lat, 0, B_Q_RANK, g_qlat, wq, 0, 0, n_slab // tn, tm=tm, tn=tn,
        extra=(gq_slab,) + tuple(tables),
        extra_specs=[pl.BlockSpec((1, tn), lambda i, j: (0, j)), tab, tab],
        epilogue=_b_q_epilogue,
        out_shape=jax.ShapeDtypeStruct((SEQ, n_slab), BF16),
        out_specs=pl.BlockSpec((tm, tn), lambda i, j: (i, j)),
        name="b_q_proj")

    gk_slab = head_slab(gk)[None, :]
    slab_out = pl.BlockSpec((tm, tn), lambda i, j: (i, j))
    k, v_ext = _norm_matmul(
        lat, 1, B_KV_RANK, g_kvlat, w_kv_up, slot, 0, n_slab // tn, tm=tm, tn=tn,
        extra=(gk_slab, lat) + tuple(tables),
        extra_specs=[pl.BlockSpec((1, B_HEAD_PAD), lambda i, j: (0, 0)),
                     pl.BlockSpec((tm, LANES), lambda i, j: (i, n_lat // LANES)),
                     tab, tab],
        epilogue=_b_kv_epilogue,
        out_shape=[jax.ShapeDtypeStruct((SEQ, n_slab), BF16)] * 2,
        out_specs=[slab_out, slab_out],
        name="b_kv_proj")
    o, cast_weights = _dense_attention(q, k, v_ext, casts)
    return _out_projection(o, w_o, slot, h, "b_out_proj"), cast_weights


def _dilated_kernel(q0_ref, q1_ref, q2_ref, kp_ref, kc_ref, kn_ref, vp_ref, vc_ref, vn_ref,
                    band_ref, o_ref, *scratch):
    i = pl.program_id(0)
    blk = q0_ref.shape[0]
    band = band_ref[...]
    win = band.shape[1]
    ones_blk = jnp.ones((win, LANES), BF16)
    col = lax.broadcasted_iota(jnp.int32, (1, win), 1)
    n4, n16 = blk // HOP, blk // (HOP * HOP)
    scratch = list(scratch)
    take = lambda k: [scratch.pop(0) for _ in range(k)]
    o_s, l_s = take(C_GROUPS), take(C_GROUPS)
    hop1 = take(8)
    hop2 = take(7)
    o2_hop1, l2_hop1, merged_tok = take(3)

    for src, dst in zip((q1_ref, q2_ref, kp_ref, kc_ref, kn_ref, vp_ref, vc_ref, vn_ref), hop1):
        for r in range(HOP):
            dst[r * n4:(r + 1) * n4, :] = src[pl.ds(r, n4, stride=HOP), :]
    for src, dst in zip(hop1[1:], hop2):
        for r in range(HOP * HOP):
            dst[r * n16:(r + 1) * n16, :] = src[pl.ds((r % HOP) * n4 + r // HOP, n16, stride=HOP), :]
    operands = ((q0_ref, kp_ref, kc_ref, kn_ref, vp_ref, vc_ref, vn_ref),
                (hop1[0],) + tuple(hop1[2:]),
                tuple(hop2))

    def attend(q, k_win, v_win, bias):
        v_ext = jnp.concatenate([v_win, ones_blk], axis=1)
        s = lax.dot_general(q, k_win, (((1,), (1,)), ((), ())), preferred_element_type=F32) + bias
        mx = jnp.maximum(s[:, :LANES], s[:, LANES:])
        m = jnp.broadcast_to(jnp.max(mx, axis=-1, keepdims=True), mx.shape)
        p = jnp.exp(s - jnp.concatenate([m, m], axis=1)).astype(BF16)
        o_ext = jnp.dot(p, v_ext, preferred_element_type=F32)
        denom = o_ext[:, HEAD_DIM:]
        return o_ext[:, :HEAD_DIM] / denom, m + jnp.log(denom)

    for g, ((window, d), refs) in enumerate(zip(C_PATTERNS, operands)):
        q_ref, kprev, kcur, knext, vprev, vcur, vnext = refs
        n = blk // d
        rq = min(n, win - 2 * C_HALF)
        pad_rows = win - min(n + 2 * C_HALF, win)
        tiles = []
        for r in range(d):
            def chain_window(prev_ref, cur_ref, next_ref):
                parts = [prev_ref[(r + 1) * n - C_HALF:(r + 1) * n, :], cur_ref[r * n:(r + 1) * n, :],
                         next_ref[r * n:r * n + C_HALF, :]]
                if pad_rows:
                    parts.append(jnp.zeros((pad_rows, LANES), F32))
                return jnp.concatenate(parts, axis=0).astype(BF16)

            k_chain = chain_window(kprev, kcur, knext)
            v_chain = chain_window(vprev, vcur, vnext)
            for sb in range(n // rq):
                kpos = i * n + sb * rq - C_HALF + col
                edge = jnp.where((kpos >= 0) & (kpos < SEQ // d), 0.0, NEG)
                dst = slice(r * n + sb * rq, r * n + (sb + 1) * rq)
                tiles.append((dst, q_ref[dst, :].astype(BF16), k_chain[sb * rq:sb * rq + win],
                              v_chain[sb * rq:sb * rq + win], band[:rq] + edge))
        full = win - 2 * C_HALF
        batch = 1 if rq == full else 4 * full // rq
        for b0 in range(0, len(tiles), batch):
            group_tiles = tiles[b0:b0 + batch]
            scores = [lax.dot_general(q, k_win, (((1,), (1,)), ((), ())),
                                      preferred_element_type=F32) + bias
                      for _, q, k_win, _, bias in group_tiles]
            probs = []
            for s in scores:
                mx = jnp.maximum(s[:, :LANES], s[:, LANES:])
                m = jnp.broadcast_to(jnp.max(mx, axis=-1, keepdims=True), mx.shape)
                probs.append((m, jnp.exp(s - jnp.concatenate([m, m], axis=1)).astype(BF16)))
            for (dst, _, _, v_win, _), (m, p) in zip(group_tiles, probs):
                o_ext = jnp.dot(p, jnp.concatenate([v_win, ones_blk], axis=1),
                                preferred_element_type=F32)
                denom = o_ext[:, HEAD_DIM:]
                o_s[g][dst, :] = o_ext[:, :HEAD_DIM] / denom
                l_s[g][dst, :] = m + jnp.log(denom)

    for src, dst in ((o_s[2], o2_hop1), (l_s[2], l2_hop1)):
        for r in range(HOP * HOP):
            dst[pl.ds((r % HOP) * n4 + r // HOP, n16, stride=HOP), :] = src[r * n16:(r + 1) * n16, :]
    for r in range(HOP):
        rows = slice(r * n4, (r + 1) * n4)
        tok = pl.ds(r, n4, stride=HOP)
        l0, l1, l2 = l_s[0][tok, :], l_s[1][rows, :], l2_hop1[rows, :]
        m = jnp.maximum(jnp.maximum(l0, l1), l2)
        e0, e1, e2 = jnp.exp(l0 - m), jnp.exp(l1 - m), jnp.exp(l2 - m)
        merged = (e0 * o_s[0][tok, :] + e1 * o_s[1][rows, :] + e2 * o2_hop1[rows, :]) / (e0 + e1 + e2)
        merged_tok[tok, :] = merged
    o_ref[...] = merged_tok[...].astype(o_ref.dtype)


def _dilated_attention(qk, v):
    blk = 1024
    nb = SEQ // blk
    win = 2 * LANES
    rows = jnp.arange(LANES, dtype=jnp.int32)[:, None]
    cols = jnp.arange(win, dtype=jnp.int32)[None, :]
    band = jnp.where(jnp.abs(rows + C_HALF - cols) <= C_HALF, 0.0, NEG).astype(F32)

    def slab(col0, shift):
        return pl.BlockSpec((blk, HEAD_DIM),
                            lambda i, h: (jnp.clip(i + shift, 0, nb - 1), col0 + h))

    k_col = C_GROUPS * C_HEADS
    in_specs = [slab(g * C_HEADS, 0) for g in range(C_GROUPS)]
    in_specs += [slab(k_col, -1), slab(k_col, 0), slab(k_col, 1)]
    in_specs += [slab(0, -1), slab(0, 0), slab(0, 1)]
    in_specs.append(pl.BlockSpec((LANES, win), lambda i, h: (0, 0)))
    return pl.pallas_call(
        _dilated_kernel,
        grid=(nb, C_HEADS),
        in_specs=in_specs,
        out_specs=pl.BlockSpec((blk, HEAD_DIM), lambda i, h: (i, h)),
        out_shape=jax.ShapeDtypeStruct((SEQ, C_HEADS * HEAD_DIM), BF16),
        scratch_shapes=[pltpu.VMEM((blk, LANES), F32)] * (2 * C_GROUPS + 8 + 7 + 3),
        compiler_params=_params("parallel", "parallel"),
        name="c_dilated_attention",
    )(qk, qk, qk, qk, qk, qk, v, v, v, band)


def _mixer_c(h, g_mix, w_in, slot, gq, gk, w_o, tables):
    n_qh = C_GROUPS * C_HEADS
    nq = n_qh * HEAD_DIM
    nkv = C_HEADS * HEAD_DIM
    scale = 1.0 / math.sqrt(HEAD_DIM)
    head_gain = jnp.concatenate([jnp.tile(gq * scale, n_qh), jnp.tile(gk, C_HEADS)])[None, :]
    qk, v = _qkv_projection(h, g_mix, w_in, slot, nq + nkv, nkv, head_gain, tables, "c", dtype=F32)
    o = _dilated_attention(qk, v)
    return _out_projection(o, w_o, slot, h, "c_out_proj")


def kernel(x, p, positions, g_mix, g_ffn, g_ple, w_ple_gate, w_ple_proj,
           w_ffn_gate, w_ffn_up, w_ffn_down,
           a_w_in, a_q_norm, a_k_norm, a_sink, a_w_o,
           b_w_in, b_q_lat_norm, b_kv_lat_norm, b_w_q_up, b_w_kv_up, b_q_norm, b_k_norm, b_w_o,
           c_w_in, c_q_norm, c_k_norm, c_w_o):
    h = x.reshape(SEQ, D_MODEL)
    p = p.reshape(DEPTH, SEQ, PLE_DIM)
    pos_col = positions.reshape(SEQ, 1)
    tables_ac = _tables_adjacent(pos_col, PARTIAL_ROT)
    tables_b = _tables_split(pos_col, B_ROPE)
    (w_ple_proj, a_w_in, a_w_o, b_w_in, b_w_q_up, b_w_kv_up, b_w_o, c_w_o) = [
        w.astype(BF16) for w in (w_ple_proj, a_w_in, a_w_o, b_w_in, b_w_q_up, b_w_kv_up, b_w_o, c_w_o)]
    late = {"ffn_gate": (w_ffn_gate, 1, DEPTH - 1, 64), "ffn_up": (w_ffn_up, 1, DEPTH - 1, 64),
            "ffn_down": (w_ffn_down, 1, DEPTH - 1, 176), "ple_gate": (w_ple_gate, 1, DEPTH - 1, 64),
            "c_in": (c_w_in, 0, 1, 16)}
    early = {name: _cast_layers(w, first, 4 * rows)
             for name, (w, first, _, rows) in late.items() if first > 0}
    cast = {}

    def weight(name, layer):
        first = late[name][1]
        if layer < first:
            return early[name], layer
        return cast[name], layer - first

    for i in range(DEPTH):
        kind, slot = i % 3, i // 3
        gm = g_mix[i][None, :]
        if kind == 0:
            h = _mixer_a(h, gm, a_w_in, slot, a_q_norm[slot], a_k_norm[slot], a_sink[slot],
                         a_w_o, tables_ac)
        elif kind == 1:
            h, cast_weights = _mixer_b(
                h, gm, b_w_in, b_q_lat_norm[slot][None, :], b_kv_lat_norm[slot][None, :],
                b_w_q_up, b_w_kv_up, slot, b_q_norm[slot], b_k_norm[slot], b_w_o, tables_b,
                list(late.values()))
            cast = dict(zip(late, cast_weights))
        else:
            h = _mixer_c(h, gm, *weight("c_in", slot), c_q_norm[slot], c_k_norm[slot], c_w_o,
                         tables_ac)
        (wg, l), (wu, _), (wd, _) = weight("ffn_gate", i), weight("ffn_up", i), weight("ffn_down", i)
        h = _ffn(h, g_ffn[i][None, :], wg, wu, wd, l)
        wpg, l = weight("ple_gate", i)
        h = _ple(h, g_ple[i][None, :], wpg, l, p, w_ple_proj, i)
    return h.reshape(1, SEQ, D_MODEL)
```

```python
import functools
import math

import jax
import jax.numpy as jnp
from jax import lax
from jax.experimental import pallas as pl
from jax.experimental.pallas import tpu as pltpu

F32 = jnp.float32
BF16 = jnp.bfloat16

SEQ = 8192
D_MODEL = 2048
DEPTH = 4
HEAD_DIM = 128
ROPE_THETA = 500000.0
PARTIAL_ROT = HEAD_DIM // 4
NORM_EPS = 1e-6
NEG = -1e30
LOG2_E = math.log2(math.e)
LANES = 128
HALF_LANES = LANES // 2

A_HEADS = 16
A_KV_HEADS = 4
A_HALF_WINDOW = 128
A_BLOCK = 128
B_HEADS = 16
B_Q_RANK = 512
B_KV_RANK = 512
B_NOPE = 128
B_ROPE = 64
B_QK = B_NOPE + B_ROPE
B_HEAD_PAD = 256
B_IN_PAD = B_Q_RANK + B_KV_RANK + LANES
C_PATTERNS = ((128, 1), (512, 4), (2048, 16))
C_GROUPS = 3
C_HEADS = 16
C_HALF = 64
assert all(window // 2 // dil == C_HALF for window, dil in C_PATTERNS)
HOP = 4
assert tuple(dil for _, dil in C_PATTERNS) == (1, HOP, HOP * HOP)
D_FF = 5632
PLE_DIM = 256

VMEM_LIMIT = 56 * 1024 * 1024


def _params(*sem):
    return pltpu.CompilerParams(dimension_semantics=sem, vmem_limit_bytes=VMEM_LIMIT)


def _rms_scale(x, width):
    ss = jnp.sum(x * x, axis=-1, keepdims=True)
    return x * lax.rsqrt(ss * (1.0 / width) + NORM_EPS)


def _rope_adjacent(y, cos, sin_lo, sin_hi, half):
    return (y * cos + pltpu.roll(y, LANES - half, 1) * sin_lo
            + pltpu.roll(y, half, 1) * sin_hi)


def _rope_split(y, cos, sin):
    return y * cos + pltpu.roll(y, HALF_LANES, 1) * sin


def _rotary_slab(x, half):
    pad = jnp.zeros(x.shape[:-1] + (HALF_LANES - half,), x.dtype)
    return jnp.concatenate([x[..., :half], pad, x[..., half:], pad], axis=-1)


def _cast_kernel(w_ref, o_ref):
    o_ref[...] = w_ref[...].astype(o_ref.dtype)


def _cast_layers(w, n_layers, rows):
    _, kdim, ndim = w.shape
    assert kdim % rows == 0
    spec = pl.BlockSpec((None, rows, ndim), lambda l, r: (l, r, 0))
    return pl.pallas_call(
        _cast_kernel,
        grid=(n_layers, kdim // rows),
        in_specs=[spec],
        out_specs=spec,
        out_shape=jax.ShapeDtypeStruct((n_layers, kdim, ndim), BF16),
        compiler_params=_params("parallel", "parallel"),
        name="cast_early_weights",
    )(w)


def _rope_table_kernel(pos_ref, inv_ref, sign_ref, cos_ref, *sin_refs):
    ang = pos_ref[...].astype(F32) * inv_ref[...]
    cos_ref[...] = jnp.cos(ang)
    s = jnp.sin(ang)
    for r, sin_ref in enumerate(sin_refs):
        sin_ref[...] = s * sign_ref[r:r + 1, :]


def _rope_tables(pos_col, inv_lanes, signs):
    tm = 1024
    n = signs.shape[0]
    tab = pl.BlockSpec((tm, LANES), lambda i: (i, 0))
    return pl.pallas_call(
        _rope_table_kernel,
        grid=(SEQ // tm,),
        in_specs=[pl.BlockSpec((tm, 1), lambda i: (i, 0)),
                  pl.BlockSpec((1, LANES), lambda i: (0, 0)),
                  pl.BlockSpec((n, LANES), lambda i: (0, 0))],
        out_specs=[tab] * (n + 1),
        out_shape=[jax.ShapeDtypeStruct((SEQ, LANES), F32)] * (n + 1),
        compiler_params=_params("parallel"),
        name="rope_tables",
    )(pos_col, inv_lanes, signs)


def _inv_freq(rot_dim):
    half = rot_dim // 2
    return ROPE_THETA ** (-jnp.arange(half, dtype=F32) * 2.0 / rot_dim)


def _tables_adjacent(pos_col, rot_dim):
    half = rot_dim // 2
    inv = _inv_freq(rot_dim)
    rest = jnp.zeros((LANES - rot_dim,), F32)
    zero, one = jnp.zeros((half,), F32), jnp.ones((half,), F32)
    inv_l = jnp.concatenate([inv, inv, rest])[None, :]
    signs = jnp.stack([jnp.concatenate([-one, zero, rest]), jnp.concatenate([zero, one, rest])])
    return _rope_tables(pos_col, inv_l, signs)


def _tables_split(pos_col, rot_dim):
    half = rot_dim // 2
    inv = _inv_freq(rot_dim)
    one = jnp.ones((half,), F32)
    inv_l = _rotary_slab(jnp.concatenate([inv, inv]), half)[None, :]
    signs = _rotary_slab(jnp.concatenate([-one, one]), half)[None, :]
    return _rope_tables(pos_col, inv_l, signs)


def _norm_matmul_kernel(x_ref, g_ref, w_ref, *rest, n_extra, epilogue, n_sub):
    extra = rest[:n_extra]
    outs = rest[n_extra:-1]
    xn_ref = rest[-1]

    sub = xn_ref.shape[0] // n_sub

    def column_tile(first):
        for c in range(n_sub):
            rows = slice(c * sub, (c + 1) * sub)
            if first:
                x = x_ref[rows, :]
                xn_ref[rows, :] = (_rms_scale(x, x.shape[-1]) * g_ref[...]).astype(BF16)
            acc = jnp.dot(xn_ref[rows, :], w_ref[...], preferred_element_type=F32)
            epilogue(acc, rows, extra, outs)

    @pl.when(pl.program_id(1) == 0)
    def _():
        column_tile(True)

    @pl.when(pl.program_id(1) > 0)
    def _():
        column_tile(False)


def _norm_matmul(x, x_col, kin, gain, w, w_slot, w_col0, n_tiles, *, tm, tn, extra, extra_specs,
                 epilogue, out_shape, out_specs, name, n_sub=4):
    m = x.shape[0]
    kernel = functools.partial(_norm_matmul_kernel, n_extra=len(extra), epilogue=epilogue,
                               n_sub=n_sub)
    return pl.pallas_call(
        kernel,
        grid=(m // tm, n_tiles),
        in_specs=[pl.BlockSpec((tm, kin), lambda i, j: (i, x_col)),
                  pl.BlockSpec((1, kin), lambda i, j: (0, 0)),
                  pl.BlockSpec((None, kin, tn), lambda i, j: (w_slot, 0, w_col0 + j))]
        + list(extra_specs),
        out_specs=out_specs,
        out_shape=out_shape,
        scratch_shapes=[pltpu.VMEM((tm, kin), BF16)],
        compiler_params=_params("parallel", "arbitrary"),
        name=name,
    )(x, gain, w, *extra)


def _plain_epilogue(acc, rows, extra, outs):
    outs[0][rows, :] = acc.astype(outs[0].dtype)


def _head_norm_rope_epilogue(acc, rows, extra, outs):
    gain_ref, cos_ref, slo_ref, shi_ref = extra
    (o_ref,) = outs
    cos, slo, shi = cos_ref[rows, :], slo_ref[rows, :], shi_ref[rows, :]
    for c in range(acc.shape[1] // HEAD_DIM):
        cols = slice(c * HEAD_DIM, (c + 1) * HEAD_DIM)
        y = _rms_scale(acc[:, cols], HEAD_DIM) * gain_ref[:, cols]
        o_ref[rows, cols] = _rope_adjacent(y, cos, slo, shi, PARTIAL_ROT // 2).astype(o_ref.dtype)


def _qkv_projection(h, g_mix, w, slot, n_qk, n_v, head_gain, tables, name, dtype=BF16):
    tm = 1024
    tn = 1024 if n_qk % 1024 == 0 and n_v % 1024 == 0 else 512
    tab = pl.BlockSpec((tm, LANES), lambda i, j: (i, 0))
    qk = _norm_matmul(
        h, 0, D_MODEL, g_mix, w, slot, 0, n_qk // tn, tm=tm, tn=tn,
        extra=(head_gain,) + tuple(tables),
        extra_specs=[pl.BlockSpec((1, tn), lambda i, j: (0, j)), tab, tab, tab],
        epilogue=_head_norm_rope_epilogue,
        out_shape=jax.ShapeDtypeStruct((SEQ, n_qk), dtype),
        out_specs=pl.BlockSpec((tm, tn), lambda i, j: (i, j)),
        name=name + "_qk_proj")
    v = _norm_matmul(
        h, 0, D_MODEL, g_mix, w, slot, n_qk // tn, n_v // tn, tm=tm, tn=tn,
        extra=(), extra_specs=[], epilogue=_plain_epilogue,
        out_shape=jax.ShapeDtypeStruct((SEQ, n_v), dtype),
        out_specs=pl.BlockSpec((tm, tn), lambda i, j: (i, j)),
        name=name + "_v_proj")
    return qk, v


def _b_q_epilogue(acc, rows, extra, outs):
    gain_ref, cos_ref, sin_ref = extra
    (o_ref,) = outs
    cos, sin = cos_ref[rows, :], sin_ref[rows, :]
    for c in range(acc.shape[1] // B_HEAD_PAD):
        c0 = c * B_HEAD_PAD
        y = _rms_scale(acc[:, c0:c0 + B_HEAD_PAD], B_QK) * gain_ref[:, c0:c0 + B_HEAD_PAD]
        o_ref[rows, c0:c0 + B_NOPE] = y[:, :B_NOPE].astype(BF16)
        o_ref[rows, c0 + B_NOPE:c0 + B_HEAD_PAD] = _rope_split(y[:, B_NOPE:], cos, sin).astype(BF16)


def _b_kv_epilogue(acc, rows, extra, outs):
    gain_ref, krope_ref, cos_ref, sin_ref = extra
    k_ref, v_ref = outs
    cos, sin = cos_ref[rows, :], sin_ref[rows, :]
    kr = krope_ref[rows, :]
    kr_ss = jnp.sum(kr * kr, axis=-1, keepdims=True)
    g_nope = gain_ref[:, :B_NOPE]
    kr_rot = _rope_split(kr * gain_ref[:, B_NOPE:], cos, sin)
    ones_blk = jnp.ones((acc.shape[0], LANES), BF16)
    for c in range(acc.shape[1] // B_HEAD_PAD):
        c0 = c * B_HEAD_PAD
        y = acc[:, c0:c0 + B_NOPE]
        ss = jnp.sum(y * y, axis=-1, keepdims=True) + kr_ss
        rinv = lax.rsqrt(ss * (1.0 / B_QK) + NORM_EPS)
        k_ref[rows, c0:c0 + B_NOPE] = (y * rinv * g_nope).astype(BF16)
        k_ref[rows, c0 + B_NOPE:c0 + B_HEAD_PAD] = (kr_rot * rinv).astype(BF16)
        v_ref[rows, c0:c0 + B_NOPE] = acc[:, c0 + B_NOPE:c0 + B_HEAD_PAD].astype(BF16)
        v_ref[rows, c0 + B_NOPE:c0 + B_HEAD_PAD] = ones_blk


def _banded_kernel(q_ref, kp_ref, kc_ref, kn_ref, vp_ref, vc_ref, vn_ref, band_ref, sink_ref,
                   o_ref, *, hw, n_kv, group):
    i = pl.program_id(0)
    tq = q_ref.shape[0]
    win = tq + 2 * hw
    n_col = win // LANES
    kpos = i * tq - hw + lax.broadcasted_iota(jnp.int32, (1, win), 1)
    edge = jnp.where((kpos >= 0) & (kpos < SEQ), 0.0, NEG)
    bias = jnp.concatenate([band_ref[...] + edge] * group, axis=0)
    ones_blk = jnp.ones((win, LANES), BF16)

    scores = []
    for kv in range(n_kv):
        kcols = slice(kv * HEAD_DIM, (kv + 1) * HEAD_DIM)
        k_win = jnp.concatenate(
            [kp_ref[tq - hw:, kcols], kc_ref[:, kcols], kn_ref[:hw, kcols]], axis=0)
        q = jnp.concatenate(
            [q_ref[:, hd * HEAD_DIM:(hd + 1) * HEAD_DIM]
             for hd in range(kv * group, (kv + 1) * group)], axis=0)
        scores.append(
            lax.dot_general(q, k_win, (((1,), (1,)), ((), ())), preferred_element_type=F32) + bias)
    probs = []
    for kv, s in enumerate(scores):
        mx = s[:, :LANES]
        for cb in range(1, n_col):
            mx = jnp.maximum(mx, s[:, cb * LANES:(cb + 1) * LANES])
        sink = sink_ref[kv * group * tq:(kv + 1) * group * tq, :]
        m = jnp.maximum(jnp.broadcast_to(jnp.max(mx, axis=-1, keepdims=True), mx.shape), sink)
        probs.append((m, jnp.exp2(s - jnp.concatenate([m] * n_col, axis=1)).astype(BF16)))
    for kv, (m, p) in enumerate(probs):
        kcols = slice(kv * HEAD_DIM, (kv + 1) * HEAD_DIM)
        v_win = jnp.concatenate(
            [vp_ref[tq - hw:, kcols], vc_ref[:, kcols], vn_ref[:hw, kcols]], axis=0)
        o_ext = jnp.dot(p, jnp.concatenate([v_win, ones_blk], axis=1), preferred_element_type=F32)
        sink = sink_ref[kv * group * tq:(kv + 1) * group * tq, :]
        o = o_ext[:, :HEAD_DIM] / (o_ext[:, HEAD_DIM:] + jnp.exp2(sink - m))
        for c in range(group):
            hd = kv * group + c
            o_ref[:, hd * HEAD_DIM:(hd + 1) * HEAD_DIM] = o[c * tq:(c + 1) * tq].astype(o_ref.dtype)


def _banded_attention(qk, v, sink, *, n_q, n_kv, hw, name):
    tq = A_BLOCK
    nb = SEQ // tq
    group = n_q // n_kv
    qw, kw = n_q * HEAD_DIM, n_kv * HEAD_DIM
    win = tq + 2 * hw
    rows = jnp.arange(tq, dtype=jnp.int32)[:, None]
    cols = jnp.arange(win, dtype=jnp.int32)[None, :]
    band = jnp.where(jnp.abs(rows + hw - cols) <= hw, 0.0, NEG).astype(F32)
    sink_rep = jnp.broadcast_to(jnp.repeat(sink, tq)[:, None], (n_q * tq, LANES))

    def kv_spec(col, shift):
        return pl.BlockSpec((tq, kw), lambda i: (jnp.clip(i + shift, 0, nb - 1), col))

    k_col = qw // kw
    return pl.pallas_call(
        functools.partial(_banded_kernel, hw=hw, n_kv=n_kv, group=group),
        grid=(nb,),
        in_specs=[pl.BlockSpec((tq, qw), lambda i: (i, 0)),
                  kv_spec(k_col, -1), kv_spec(k_col, 0), kv_spec(k_col, 1),
                  kv_spec(0, -1), kv_spec(0, 0), kv_spec(0, 1),
                  pl.BlockSpec((tq, win), lambda i: (0, 0)),
                  pl.BlockSpec((n_q * tq, LANES), lambda i: (0, 0))],
        out_specs=pl.BlockSpec((tq, qw), lambda i: (i, 0)),
        out_shape=jax.ShapeDtypeStruct((SEQ, qw), BF16),
        compiler_params=_params("parallel"),
        name=name,
    )(qk, qk, qk, qk, v, v, v, band, sink_rep)


def _flash_kernel(*refs, tk, n_cast):
    q_ref, k_ref, v_ref = refs[:3]
    w_refs = refs[3:3 + n_cast]
    o_ref = refs[3 + n_cast]
    wb_refs = refs[4 + n_cast:4 + 2 * n_cast]
    m_ref, acc_ref = refs[4 + 2 * n_cast:]
    tq = q_ref.shape[0]
    n_chunks = k_ref.shape[0] // tk
    n_col = tk // LANES
    m_ref[...] = jnp.full((tq, LANES), NEG, F32)
    acc_ref[...] = jnp.zeros((tq, B_HEAD_PAD), F32)
    q = q_ref[...]
    for w_ref, wb_ref in zip(w_refs, wb_refs):
        wb_ref[...] = w_ref[...].astype(BF16)

    for c in range(n_chunks):
        k = k_ref[c * tk:(c + 1) * tk, :]
        v = v_ref[c * tk:(c + 1) * tk, :]
        s = lax.dot_general(q, k, (((1,), (1,)), ((), ())), preferred_element_type=F32)
        mx = s[:, :LANES]
        for cb in range(1, n_col):
            mx = jnp.maximum(mx, s[:, cb * LANES:(cb + 1) * LANES])
        m_old = m_ref[...]
        m_new = jnp.maximum(m_old, jnp.max(mx, axis=-1, keepdims=True))
        alpha = jnp.exp2(m_old - m_new)
        p = jnp.exp2(s - jnp.concatenate([m_new] * n_col, axis=1)).astype(BF16)
        pv = jnp.dot(p, v, preferred_element_type=F32)
        acc_ref[...] = acc_ref[...] * jnp.concatenate([alpha, alpha], axis=1) + pv
        m_ref[...] = m_new

    acc = acc_ref[...]
    o_ref[...] = (acc[:, :B_NOPE] / acc[:, B_NOPE:]).astype(o_ref.dtype)


def _dense_attention(q, k, v_ext, casts):
    tq, tk = 1024, 256
    nq = SEQ // tq
    kv_spec = pl.BlockSpec((SEQ, B_HEAD_PAD), lambda h, i: (0, h))
    w_specs, wb_specs, wb_shapes = [], [], []
    for w, first, n_layers, rows in casts:
        _, kdim, ndim = w.shape
        per_layer = kdim // rows
        last = n_layers * per_layer - 1
        assert last < B_HEADS * nq and kdim % rows == 0

        def slab(h, i, per_layer=per_layer, last=last):
            s = jnp.minimum(h * nq + i, last)
            return s // per_layer, s % per_layer

        w_specs.append(pl.BlockSpec(
            (None, rows, ndim),
            lambda h, i, slab=slab, first=first: (first + slab(h, i)[0], slab(h, i)[1], 0)))
        wb_specs.append(pl.BlockSpec(
            (None, rows, ndim), lambda h, i, slab=slab: (slab(h, i)[0], slab(h, i)[1], 0)))
        wb_shapes.append(jax.ShapeDtypeStruct((n_layers, kdim, ndim), BF16))
    outs = pl.pallas_call(
        functools.partial(_flash_kernel, tk=tk, n_cast=len(casts)),
        grid=(B_HEADS, nq),
        in_specs=[pl.BlockSpec((tq, B_HEAD_PAD), lambda h, i: (i, h)), kv_spec, kv_spec] + w_specs,
        out_specs=[pl.BlockSpec((tq, B_NOPE), lambda h, i: (i, h))] + wb_specs,
        out_shape=[jax.ShapeDtypeStruct((SEQ, B_HEADS * B_NOPE), BF16)] + wb_shapes,
        scratch_shapes=[pltpu.VMEM((tq, LANES), F32), pltpu.VMEM((tq, B_HEAD_PAD), F32)],
        compiler_params=_params("arbitrary", "arbitrary"),
        name="b_flash_attention",
    )(q, k, v_ext, *[c[0] for c in casts])
    return outs[0], outs[1:]


def _out_proj_kernel(o_ref, w_ref, h_ref, out_ref, *, n_sub):
    sub = o_ref.shape[0] // n_sub
    for c in range(n_sub):
        rows = slice(c * sub, (c + 1) * sub)
        out_ref[rows, :] = h_ref[rows, :] + jnp.dot(o_ref[rows, :], w_ref[...],
                                                    preferred_element_type=F32)


def _out_projection(o, w, slot, h, name):
    tm = 1024
    row = pl.BlockSpec((tm, D_MODEL), lambda i: (i, 0))
    return pl.pallas_call(
        functools.partial(_out_proj_kernel, n_sub=4),
        grid=(SEQ // tm,),
        in_specs=[row,
                  pl.BlockSpec((None, D_MODEL, D_MODEL), lambda i: (slot, 0, 0),
                               pipeline_mode=pl.Buffered(1)),
                  row],
        out_specs=row,
        out_shape=jax.ShapeDtypeStruct((SEQ, D_MODEL), F32),
        compiler_params=_params("parallel"),
        name=name,
    )(o, w, h)


def _ffn_kernel(x_ref, g_ref, wg_ref, wu_ref, wd_ref, o_ref, xn_ref):
    @pl.when(pl.program_id(1) == 0)
    def _():
        x = x_ref[...]
        xn_ref[...] = (_rms_scale(x, D_MODEL) * g_ref[...]).astype(BF16)
        o_ref[...] = x

    xn = xn_ref[...]
    gate = jnp.dot(xn, wg_ref[...], preferred_element_type=F32)
    up = jnp.dot(xn, wu_ref[...], preferred_element_type=F32)
    act = (gate * jax.nn.sigmoid(gate) * up).astype(BF16)
    o_ref[...] += jnp.dot(act, wd_ref[...], preferred_element_type=F32)


def _ffn(h, g, wg, wu, wd, layer):
    tm, tf = 1024, 512
    return pl.pallas_call(
        _ffn_kernel,
        grid=(SEQ // tm, D_FF // tf),
        in_specs=[pl.BlockSpec((tm, D_MODEL), lambda i, f: (i, 0)),
                  pl.BlockSpec((1, D_MODEL), lambda i, f: (0, 0)),
                  pl.BlockSpec((None, D_MODEL, tf), lambda i, f: (layer, 0, f)),
                  pl.BlockSpec((None, D_MODEL, tf), lambda i, f: (layer, 0, f)),
                  pl.BlockSpec((None, tf, D_MODEL), lambda i, f: (layer, f, 0))],
        out_specs=pl.BlockSpec((tm, D_MODEL), lambda i, f: (i, 0)),
        out_shape=jax.ShapeDtypeStruct((SEQ, D_MODEL), F32),
        scratch_shapes=[pltpu.VMEM((tm, D_MODEL), BF16)],
        compiler_params=_params("parallel", "arbitrary"),
        name="ffn_swiglu",
    )(h, g, wg, wu, wd)


def _ple_kernel(x_ref, g_ref, wg_ref, p_ref, wp_ref, o_ref, *, n_sub):
    sub = x_ref.shape[0] // n_sub
    for c in range(n_sub):
        rows = slice(c * sub, (c + 1) * sub)
        x = x_ref[rows, :]
        xn = (_rms_scale(x, D_MODEL) * g_ref[...]).astype(BF16)
        gate = jnp.dot(xn, wg_ref[...], preferred_element_type=F32)
        proj = jnp.dot(p_ref[rows, :].astype(BF16), wp_ref[...], preferred_element_type=F32)
        o_ref[rows, :] = x + jax.nn.sigmoid(gate) * proj


def _ple(h, g, w_gate, gate_slot, p, w_proj, layer):
    tm = 1024
    resident = pl.Buffered(1)
    return pl.pallas_call(
        functools.partial(_ple_kernel, n_sub=4),
        grid=(SEQ // tm,),
        in_specs=[pl.BlockSpec((tm, D_MODEL), lambda i: (i, 0)),
                  pl.BlockSpec((1, D_MODEL), lambda i: (0, 0)),
                  pl.BlockSpec((None, D_MODEL, D_MODEL), lambda i: (gate_slot, 0, 0),
                               pipeline_mode=resident),
                  pl.BlockSpec((None, tm, PLE_DIM), lambda i: (layer, i, 0)),
                  pl.BlockSpec((None, PLE_DIM, D_MODEL), lambda i: (layer, 0, 0),
                               pipeline_mode=resident)],
        out_specs=pl.BlockSpec((tm, D_MODEL), lambda i: (i, 0)),
        out_shape=jax.ShapeDtypeStruct((SEQ, D_MODEL), F32),
        compiler_params=_params("parallel"),
        name="ple_gate",
    )(h, g, w_gate, p, w_proj)


def _mixer_a(h, g_mix, w_in, slot, gq, gk, sink, w_o, tables):
    nq, nk = A_HEADS * HEAD_DIM, A_KV_HEADS * HEAD_DIM
    scale = LOG2_E / math.sqrt(HEAD_DIM)
    head_gain = jnp.concatenate([jnp.tile(gq * scale, A_HEADS), jnp.tile(gk, A_KV_HEADS)])[None, :]
    qk, v = _qkv_projection(h, g_mix, w_in, slot, nq + nk, nk, head_gain, tables, "a")
    o = _banded_attention(qk, v, sink * LOG2_E, n_q=A_HEADS, n_kv=A_KV_HEADS, hw=A_HALF_WINDOW,
                          name="a_banded_attention")
    return _out_projection(o, w_o, slot, h, "a_out_proj")


def _mixer_b(h, g_mix, w_in, g_qlat, g_kvlat, w_q_up, w_kv_up, slot, gq, gk, w_o, tables, casts):
    scale = LOG2_E / math.sqrt(B_QK)
    half = B_ROPE // 2
    n_lat = B_Q_RANK + B_KV_RANK
    w_in = w_in[slot]
    w_in_pad = jnp.concatenate([w_in[:, :n_lat], _rotary_slab(w_in[:, n_lat:], half)], axis=1)[None]
    tm = 1024
    lat = _norm_matmul(
        h, 0, D_MODEL, g_mix, w_in_pad, 0, 0, 1, tm=tm, tn=B_IN_PAD, extra=(), extra_specs=[],
        epilogue=_plain_epilogue,
        out_shape=jax.ShapeDtypeStruct((SEQ, B_IN_PAD), F32),
        out_specs=pl.BlockSpec((tm, B_IN_PAD), lambda i, j: (i, j)),
        name="b_latent_proj")

    tab = pl.BlockSpec((tm, LANES), lambda i, j: (i, 0))
    tn = 1024
    n_slab = B_HEADS * B_HEAD_PAD

    def head_slab(x):
        return jnp.concatenate([x[..., :B_NOPE], _rotary_slab(x[..., B_NOPE:], half)], axis=-1)

    wq = head_slab(w_q_up[slot].reshape(B_Q_RANK, B_HEADS, B_QK)).reshape(B_Q_RANK, n_slab)[None]
    gq_slab = jnp.tile(head_slab(gq * scale), B_HEADS)[None, :]
    q = _norm_matmul(
        lat, 0, B_Q_RANK, g_qlat, wq, 0, 0, n_slab // tn, tm=tm, tn=tn,
        extra=(gq_slab,) + tuple(tables),
        extra_specs=[pl.BlockSpec((1, tn), lambda i, j: (0, j)), tab, tab],
        epilogue=_b_q_epilogue,
        out_shape=jax.ShapeDtypeStruct((SEQ, n_slab), BF16),
        out_specs=pl.BlockSpec((tm, tn), lambda i, j: (i, j)),
        name="b_q_proj")

    gk_slab = head_slab(gk)[None, :]
    slab_out = pl.BlockSpec((tm, tn), lambda i, j: (i, j))
    k, v_ext = _norm_matmul(
        lat, 1, B_KV_RANK, g_kvlat, w_kv_up, slot, 0, n_slab // tn, tm=tm, tn=tn,
        extra=(gk_slab, lat) + tuple(tables),
        extra_specs=[pl.BlockSpec((1, B_HEAD_PAD), lambda i, j: (0, 0)),
                     pl.BlockSpec((tm, LANES), lambda i, j: (i, n_lat // LANES)),
                     tab, tab],
        epilogue=_b_kv_epilogue,
        out_shape=[jax.ShapeDtypeStruct((SEQ, n_slab), BF16)] * 2,
        out_specs=[slab_out, slab_out],
        name="b_kv_proj")
    o, cast_weights = _dense_attention(q, k, v_ext, casts)
    return _out_projection(o, w_o, slot, h, "b_out_proj"), cast_weights


def _dilated_kernel(q0_ref, q1_ref, q2_ref, kp_ref, kc_ref, kn_ref, vp_ref, vc_ref, vn_ref,
                    band_ref, o_ref, *scratch):
    i = pl.program_id(0)
    blk = q0_ref.shape[0]
    band = band_ref[...]
    win = band.shape[1]
    ones_blk = jnp.ones((win, LANES), BF16)
    col = lax.broadcasted_iota(jnp.int32, (1, win), 1)
    n4, n16 = blk // HOP, blk // (HOP * HOP)
    scratch = list(scratch)
    take = lambda k: [scratch.pop(0) for _ in range(k)]
    o_s, l_s = take(C_GROUPS), take(C_GROUPS)
    hop1 = take(8)
    hop2 = take(7)
    o2_hop1, l2_hop1, merged_tok = take(3)

    for src, dst in zip((q1_ref, q2_ref, kp_ref, kc_ref, kn_ref, vp_ref, vc_ref, vn_ref), hop1):
        for r in range(HOP):
            dst[r * n4:(r + 1) * n4, :] = src[pl.ds(r, n4, stride=HOP), :]
    for src, dst in zip(hop1[1:], hop2):
        for r in range(HOP * HOP):
            dst[r * n16:(r + 1) * n16, :] = src[pl.ds((r % HOP) * n4 + r // HOP, n16, stride=HOP), :]
    operands = ((q0_ref, kp_ref, kc_ref, kn_ref, vp_ref, vc_ref, vn_ref),
                (hop1[0],) + tuple(hop1[2:]),
                tuple(hop2))

    for g, ((window, d), refs) in enumerate(zip(C_PATTERNS, operands)):
        q_ref, kprev, kcur, knext, vprev, vcur, vnext = refs
        n = blk // d
        rq = min(n, win - 2 * C_HALF)
        pad_rows = win - min(n + 2 * C_HALF, win)
        tiles = []
        for r in range(d):
            def chain_window(prev_ref, cur_ref, next_ref):
                parts = [prev_ref[(r + 1) * n - C_HALF:(r + 1) * n, :], cur_ref[r * n:(r + 1) * n, :],
                         next_ref[r * n:r * n + C_HALF, :]]
                if pad_rows:
                    parts.append(jnp.zeros((pad_rows, LANES), F32))
                return jnp.concatenate(parts, axis=0).astype(BF16)

            k_chain = chain_window(kprev, kcur, knext)
            v_chain = chain_window(vprev, vcur, vnext)
            for sb in range(n // rq):
                kpos = i * n + sb * rq - C_HALF + col
                edge = jnp.where((kpos >= 0) & (kpos < SEQ // d), 0.0, NEG)
                dst = slice(r * n + sb * rq, r * n + (sb + 1) * rq)
                tiles.append((dst, q_ref[dst, :].astype(BF16), k_chain[sb * rq:sb * rq + win],
                              v_chain[sb * rq:sb * rq + win], band[:rq] + edge))
        full = win - 2 * C_HALF
        batch = 1 if rq == full else 4 * full // rq
        for b0 in range(0, len(tiles), batch):
            group_tiles = tiles[b0:b0 + batch]
            scores = [lax.dot_general(q, k_win, (((1,), (1,)), ((), ())),
                                      preferred_element_type=F32) + bias
                      for _, q, k_win, _, bias in group_tiles]
            probs = []
            for s in scores:
                mx = jnp.maximum(s[:, :LANES], s[:, LANES:])
                m = jnp.broadcast_to(jnp.max(mx, axis=-1, keepdims=True), mx.shape)
                probs.append((m, jnp.exp2(s - jnp.concatenate([m, m], axis=1)).astype(BF16)))
            for (dst, _, _, v_win, _), (m, p) in zip(group_tiles, probs):
                o_ext = jnp.dot(p, jnp.concatenate([v_win, ones_blk], axis=1),
                                preferred_element_type=F32)
                denom = o_ext[:, HEAD_DIM:]
                o_s[g][dst, :] = o_ext[:, :HEAD_DIM] / denom
                l_s[g][dst, :] = m + jnp.log2(denom)

    for src, dst in ((o_s[2], o2_hop1), (l_s[2], l2_hop1)):
        for r in range(HOP * HOP):
            dst[pl.ds((r % HOP) * n4 + r // HOP, n16, stride=HOP), :] = src[r * n16:(r + 1) * n16, :]
    for r in range(HOP):
        rows = slice(r * n4, (r + 1) * n4)
        tok = pl.ds(r, n4, stride=HOP)
        l0, l1, l2 = l_s[0][tok, :], l_s[1][rows, :], l2_hop1[rows, :]
        m = jnp.maximum(jnp.maximum(l0, l1), l2)
        e0, e1, e2 = jnp.exp2(l0 - m), jnp.exp2(l1 - m), jnp.exp2(l2 - m)
        merged = (e0 * o_s[0][tok, :] + e1 * o_s[1][rows, :] + e2 * o2_hop1[rows, :]) / (e0 + e1 + e2)
        merged_tok[tok, :] = merged
    o_ref[...] = merged_tok[...].astype(o_ref.dtype)


def _dilated_attention(qk, v):
    blk = 1024
    nb = SEQ // blk
    win = 2 * LANES
    rows = jnp.arange(LANES, dtype=jnp.int32)[:, None]
    cols = jnp.arange(win, dtype=jnp.int32)[None, :]
    band = jnp.where(jnp.abs(rows + C_HALF - cols) <= C_HALF, 0.0, NEG).astype(F32)

    def slab(col0, shift):
        return pl.BlockSpec((blk, HEAD_DIM),
                            lambda i, h: (jnp.clip(i + shift, 0, nb - 1), col0 + h))

    k_col = C_GROUPS * C_HEADS
    in_specs = [slab(g * C_HEADS, 0) for g in range(C_GROUPS)]
    in_specs += [slab(k_col, -1), slab(k_col, 0), slab(k_col, 1)]
    in_specs += [slab(0, -1), slab(0, 0), slab(0, 1)]
    in_specs.append(pl.BlockSpec((LANES, win), lambda i, h: (0, 0)))
    return pl.pallas_call(
        _dilated_kernel,
        grid=(nb, C_HEADS),
        in_specs=in_specs,
        out_specs=pl.BlockSpec((blk, HEAD_DIM), lambda i, h: (i, h)),
        out_shape=jax.ShapeDtypeStruct((SEQ, C_HEADS * HEAD_DIM), BF16),
        scratch_shapes=[pltpu.VMEM((blk, LANES), F32)] * (2 * C_GROUPS + 8 + 7 + 3),
        compiler_params=_params("parallel", "parallel"),
        name="c_dilated_attention",
    )(qk, qk, qk, qk, qk, qk, v, v, v, band)


def _mixer_c(h, g_mix, w_in, slot, gq, gk, w_o, tables):
    n_qh = C_GROUPS * C_HEADS
    nq = n_qh * HEAD_DIM
    nkv = C_HEADS * HEAD_DIM
    scale = LOG2_E / math.sqrt(HEAD_DIM)
    head_gain = jnp.concatenate([jnp.tile(gq * scale, n_qh), jnp.tile(gk, C_HEADS)])[None, :]
    qk, v = _qkv_projection(h, g_mix, w_in, slot, nq + nkv, nkv, head_gain, tables, "c", dtype=F32)
    o = _dilated_attention(qk, v)
    return _out_projection(o, w_o, slot, h, "c_out_proj")


def kernel(x, p, positions, g_mix, g_ffn, g_ple, w_ple_gate, w_ple_proj,
           w_ffn_gate, w_ffn_up, w_ffn_down,
           a_w_in, a_q_norm, a_k_norm, a_sink, a_w_o,
           b_w_in, b_q_lat_norm, b_kv_lat_norm, b_w_q_up, b_w_kv_up, b_q_norm, b_k_norm, b_w_o,
           c_w_in, c_q_norm, c_k_norm, c_w_o):
    h = x.reshape(SEQ, D_MODEL)
    p = p.reshape(DEPTH, SEQ, PLE_DIM)
    pos_col = positions.reshape(SEQ, 1)
    tables_ac = _tables_adjacent(pos_col, PARTIAL_ROT)
    tables_b = _tables_split(pos_col, B_ROPE)
    (w_ple_proj, a_w_in, a_w_o, b_w_in, b_w_q_up, b_w_kv_up, b_w_o, c_w_o) = [
        w.astype(BF16) for w in (w_ple_proj, a_w_in, a_w_o, b_w_in, b_w_q_up, b_w_kv_up, b_w_o, c_w_o)]
    late = {"ffn_gate": (w_ffn_gate, 1, DEPTH - 1, 64), "ffn_up": (w_ffn_up, 1, DEPTH - 1, 64),
            "ffn_down": (w_ffn_down, 1, DEPTH - 1, 176), "ple_gate": (w_ple_gate, 1, DEPTH - 1, 64),
            "c_in": (c_w_in, 0, 1, 16)}
    early = {name: _cast_layers(w, first, 4 * rows)
             for name, (w, first, _, rows) in late.items() if first > 0}
    cast = {}

    def weight(name, layer):
        first = late[name][1]
        if layer < first:
            return early[name], layer
        return cast[name], layer - first

    for i in range(DEPTH):
        kind, slot = i % 3, i // 3
        gm = g_mix[i][None, :]
        if kind == 0:
            h = _mixer_a(h, gm, a_w_in, slot, a_q_norm[slot], a_k_norm[slot], a_sink[slot],
                         a_w_o, tables_ac)
        elif kind == 1:
            h, cast_weights = _mixer_b(
                h, gm, b_w_in, b_q_lat_norm[slot][None, :], b_kv_lat_norm[slot][None, :],
                b_w_q_up, b_w_kv_up, slot, b_q_norm[slot], b_k_norm[slot], b_w_o, tables_b,
                list(late.values()))
            cast = dict(zip(late, cast_weights))
        else:
            h = _mixer_c(h, gm, *weight("c_in", slot), c_q_norm[slot], c_k_norm[slot], c_w_o,
                         tables_ac)
        (wg, l), (wu, _), (wd, _) = weight("ffn_gate", i), weight("ffn_up", i), weight("ffn_down", i)
        h = _ffn(h, g_ffn[i][None, :], wg, wu, wd, l)
        wpg, l = weight("ple_gate", i)
        h = _ple(h, g_ple[i][None, :], wpg, l, p, w_ple_proj, i)
    return h.reshape(1, SEQ, D_MODEL)
```

```python
import functools
import math

import jax
import jax.numpy as jnp
from jax import lax
from jax.experimental import pallas as pl
from jax.experimental.pallas import tpu as pltpu

F32 = jnp.float32
BF16 = jnp.bfloat16

SEQ = 8192
D_MODEL = 2048
DEPTH = 4
HEAD_DIM = 128
ROPE_THETA = 500000.0
PARTIAL_ROT = HEAD_DIM // 4
NORM_EPS = 1e-6
NEG = -1e30
LOG2_E = math.log2(math.e)
LANES = 128
HALF_LANES = LANES // 2

A_HEADS = 16
A_KV_HEADS = 4
A_HALF_WINDOW = 128
A_BLOCK = 512
A_TILE = 128
B_HEADS = 16
B_Q_RANK = 512
B_KV_RANK = 512
B_NOPE = 128
B_ROPE = 64
B_QK = B_NOPE + B_ROPE
B_HEAD_PAD = 256
B_IN_PAD = B_Q_RANK + B_KV_RANK + LANES
C_PATTERNS = ((128, 1), (512, 4), (2048, 16))
C_GROUPS = 3
C_HEADS = 16
C_HALF = 64
assert all(window // 2 // dil == C_HALF for window, dil in C_PATTERNS)
HOP = 4
assert tuple(dil for _, dil in C_PATTERNS) == (1, HOP, HOP * HOP)
D_FF = 5632
PLE_DIM = 256

VMEM_LIMIT = 56 * 1024 * 1024


def _params(*sem):
    return pltpu.CompilerParams(dimension_semantics=sem, vmem_limit_bytes=VMEM_LIMIT)


def _rms_scale(x, width):
    ss = jnp.sum(x * x, axis=-1, keepdims=True)
    return x * lax.rsqrt(ss * (1.0 / width) + NORM_EPS)


def _rope_adjacent(y, cos, sin_lo, sin_hi, half):
    return (y * cos + pltpu.roll(y, LANES - half, 1) * sin_lo
            + pltpu.roll(y, half, 1) * sin_hi)


def _rope_split(y, cos, sin):
    return y * cos + pltpu.roll(y, HALF_LANES, 1) * sin


def _rotary_slab(x, half):
    pad = jnp.zeros(x.shape[:-1] + (HALF_LANES - half,), x.dtype)
    return jnp.concatenate([x[..., :half], pad, x[..., half:], pad], axis=-1)


def _cast_kernel(w_ref, o_ref):
    o_ref[...] = w_ref[...].astype(o_ref.dtype)


def _cast_layers(w, n_layers, rows):
    _, kdim, ndim = w.shape
    assert kdim % rows == 0
    spec = pl.BlockSpec((None, rows, ndim), lambda l, r: (l, r, 0))
    return pl.pallas_call(
        _cast_kernel,
        grid=(n_layers, kdim // rows),
        in_specs=[spec],
        out_specs=spec,
        out_shape=jax.ShapeDtypeStruct((n_layers, kdim, ndim), BF16),
        compiler_params=_params("parallel", "parallel"),
        name="cast_early_weights",
    )(w)


def _rope_table_kernel(pos_ref, inv_ref, sign_ref, cos_ref, *sin_refs):
    ang = pos_ref[...].astype(F32) * inv_ref[...]
    cos_ref[...] = jnp.cos(ang)
    s = jnp.sin(ang)
    for r, sin_ref in enumerate(sin_refs):
        sin_ref[...] = s * sign_ref[r:r + 1, :]


def _rope_tables(pos_col, inv_lanes, signs):
    tm = 1024
    n = signs.shape[0]
    tab = pl.BlockSpec((tm, LANES), lambda i: (i, 0))
    return pl.pallas_call(
        _rope_table_kernel,
        grid=(SEQ // tm,),
        in_specs=[pl.BlockSpec((tm, 1), lambda i: (i, 0)),
                  pl.BlockSpec((1, LANES), lambda i: (0, 0)),
                  pl.BlockSpec((n, LANES), lambda i: (0, 0))],
        out_specs=[tab] * (n + 1),
        out_shape=[jax.ShapeDtypeStruct((SEQ, LANES), F32)] * (n + 1),
        compiler_params=_params("parallel"),
        name="rope_tables",
    )(pos_col, inv_lanes, signs)


def _inv_freq(rot_dim):
    half = rot_dim // 2
    return ROPE_THETA ** (-jnp.arange(half, dtype=F32) * 2.0 / rot_dim)


def _tables_adjacent(pos_col, rot_dim):
    half = rot_dim // 2
    inv = _inv_freq(rot_dim)
    rest = jnp.zeros((LANES - rot_dim,), F32)
    zero, one = jnp.zeros((half,), F32), jnp.ones((half,), F32)
    inv_l = jnp.concatenate([inv, inv, rest])[None, :]
    signs = jnp.stack([jnp.concatenate([-one, zero, rest]), jnp.concatenate([zero, one, rest])])
    return _rope_tables(pos_col, inv_l, signs)


def _tables_split(pos_col, rot_dim):
    half = rot_dim // 2
    inv = _inv_freq(rot_dim)
    one = jnp.ones((half,), F32)
    inv_l = _rotary_slab(jnp.concatenate([inv, inv]), half)[None, :]
    signs = _rotary_slab(jnp.concatenate([-one, one]), half)[None, :]
    return _rope_tables(pos_col, inv_l, signs)


def _norm_matmul_kernel(x_ref, g_ref, w_ref, *rest, n_extra, epilogue, n_sub):
    extra = rest[:n_extra]
    outs = rest[n_extra:-1]
    xn_ref = rest[-1]

    sub = xn_ref.shape[0] // n_sub

    def column_tile(first):
        for c in range(n_sub):
            rows = slice(c * sub, (c + 1) * sub)
            if first:
                x = x_ref[rows, :]
                xn_ref[rows, :] = (_rms_scale(x, x.shape[-1]) * g_ref[...]).astype(BF16)
            acc = jnp.dot(xn_ref[rows, :], w_ref[...], preferred_element_type=F32)
            epilogue(acc, rows, extra, outs)

    @pl.when(pl.program_id(1) == 0)
    def _():
        column_tile(True)

    @pl.when(pl.program_id(1) > 0)
    def _():
        column_tile(False)


def _norm_matmul(x, x_col, kin, gain, w, w_slot, w_col0, n_tiles, *, tm, tn, extra, extra_specs,
                 epilogue, out_shape, out_specs, name, n_sub=4):
    m = x.shape[0]
    kernel = functools.partial(_norm_matmul_kernel, n_extra=len(extra), epilogue=epilogue,
                               n_sub=n_sub)
    return pl.pallas_call(
        kernel,
        grid=(m // tm, n_tiles),
        in_specs=[pl.BlockSpec((tm, kin), lambda i, j: (i, x_col)),
                  pl.BlockSpec((1, kin), lambda i, j: (0, 0)),
                  pl.BlockSpec((None, kin, tn), lambda i, j: (w_slot, 0, w_col0 + j))]
        + list(extra_specs),
        out_specs=out_specs,
        out_shape=out_shape,
        scratch_shapes=[pltpu.VMEM((tm, kin), BF16)],
        compiler_params=_params("parallel", "arbitrary"),
        name=name,
    )(x, gain, w, *extra)


def _plain_epilogue(acc, rows, extra, outs):
    outs[0][rows, :] = acc.astype(outs[0].dtype)


def _head_norm_rope_epilogue(acc, rows, extra, outs):
    gain_ref, cos_ref, slo_ref, shi_ref = extra
    (o_ref,) = outs
    cos, slo, shi = cos_ref[rows, :], slo_ref[rows, :], shi_ref[rows, :]
    for c in range(acc.shape[1] // HEAD_DIM):
        cols = slice(c * HEAD_DIM, (c + 1) * HEAD_DIM)
        y = _rms_scale(acc[:, cols], HEAD_DIM) * gain_ref[:, cols]
        o_ref[rows, cols] = _rope_adjacent(y, cos, slo, shi, PARTIAL_ROT // 2).astype(o_ref.dtype)


def _qkv_projection(h, g_mix, w, slot, n_qk, n_v, head_gain, tables, name, dtype=BF16):
    tm = 1024
    tn = next(t for t in (1280, 1024, 512) if n_qk % t == 0)
    tn_v = next(t for t in (1024, 512) if n_v % t == 0 and n_qk % t == 0)
    tab = pl.BlockSpec((tm, LANES), lambda i, j: (i, 0))
    qk = _norm_matmul(
        h, 0, D_MODEL, g_mix, w, slot, 0, n_qk // tn, tm=tm, tn=tn,
        extra=(head_gain,) + tuple(tables),
        extra_specs=[pl.BlockSpec((1, tn), lambda i, j: (0, j)), tab, tab, tab],
        epilogue=_head_norm_rope_epilogue,
        out_shape=jax.ShapeDtypeStruct((SEQ, n_qk), dtype),
        out_specs=pl.BlockSpec((tm, tn), lambda i, j: (i, j)),
        name=name + "_qk_proj")
    v = _norm_matmul(
        h, 0, D_MODEL, g_mix, w, slot, n_qk // tn_v, n_v // tn_v, tm=tm, tn=tn_v,
        extra=(), extra_specs=[], epilogue=_plain_epilogue,
        out_shape=jax.ShapeDtypeStruct((SEQ, n_v), dtype),
        out_specs=pl.BlockSpec((tm, tn_v), lambda i, j: (i, j)),
        name=name + "_v_proj")
    return qk, v


def _b_q_epilogue(acc, rows, extra, outs):
    gain_ref, cos_ref, sin_ref = extra
    (o_ref,) = outs
    cos, sin = cos_ref[rows, :], sin_ref[rows, :]
    for c in range(acc.shape[1] // B_HEAD_PAD):
        c0 = c * B_HEAD_PAD
        y = _rms_scale(acc[:, c0:c0 + B_HEAD_PAD], B_QK) * gain_ref[:, c0:c0 + B_HEAD_PAD]
        o_ref[rows, c0:c0 + B_NOPE] = y[:, :B_NOPE].astype(BF16)
        o_ref[rows, c0 + B_NOPE:c0 + B_HEAD_PAD] = _rope_split(y[:, B_NOPE:], cos, sin).astype(BF16)


def _b_kv_epilogue(acc, rows, extra, outs):
    gain_ref, krope_ref, cos_ref, sin_ref = extra
    k_ref, v_ref = outs
    cos, sin = cos_ref[rows, :], sin_ref[rows, :]
    kr = krope_ref[rows, :]
    kr_ss = jnp.sum(kr * kr, axis=-1, keepdims=True)
    g_nope = gain_ref[:, :B_NOPE]
    kr_rot = _rope_split(kr * gain_ref[:, B_NOPE:], cos, sin)
    ones_blk = jnp.ones((acc.shape[0], LANES), BF16)
    for c in range(acc.shape[1] // B_HEAD_PAD):
        c0 = c * B_HEAD_PAD
        y = acc[:, c0:c0 + B_NOPE]
        ss = jnp.sum(y * y, axis=-1, keepdims=True) + kr_ss
        rinv = lax.rsqrt(ss * (1.0 / B_QK) + NORM_EPS)
        k_ref[rows, c0:c0 + B_NOPE] = (y * rinv * g_nope).astype(BF16)
        k_ref[rows, c0 + B_NOPE:c0 + B_HEAD_PAD] = (kr_rot * rinv).astype(BF16)
        v_ref[rows, c0:c0 + B_NOPE] = acc[:, c0 + B_NOPE:c0 + B_HEAD_PAD].astype(BF16)
        v_ref[rows, c0 + B_NOPE:c0 + B_HEAD_PAD] = ones_blk


def _banded_kernel(q_ref, kp_ref, kc_ref, kn_ref, vp_ref, vc_ref, vn_ref, band_ref, sink_ref,
                   o_ref, *, hw, sq, n_kv, group):
    i = pl.program_id(0)
    blk = q_ref.shape[0]
    win = sq + 2 * hw
    n_col = win // LANES
    ones_blk = jnp.ones((win, LANES), BF16)
    col = lax.broadcasted_iota(jnp.int32, (1, win), 1)

    def halo_concat(prev_ref, cur_ref, next_ref, kv):
        cols = slice(kv * HEAD_DIM, (kv + 1) * HEAD_DIM)
        return jnp.concatenate(
            [prev_ref[blk - hw:, cols], cur_ref[:, cols], next_ref[:hw, cols]], axis=0)

    k_cat = [halo_concat(kp_ref, kc_ref, kn_ref, kv) for kv in range(n_kv)]
    v_cat = [halo_concat(vp_ref, vc_ref, vn_ref, kv) for kv in range(n_kv)]
    for sb in range(blk // sq):
        rows = slice(sb * sq, (sb + 1) * sq)
        kpos = i * blk + sb * sq - hw + col
        edge = jnp.where((kpos >= 0) & (kpos < SEQ), 0.0, NEG)
        bias = jnp.concatenate([band_ref[...] + edge] * group, axis=0)
        scores = []
        for kv in range(n_kv):
            q = jnp.concatenate(
                [q_ref[rows, hd * HEAD_DIM:(hd + 1) * HEAD_DIM]
                 for hd in range(kv * group, (kv + 1) * group)], axis=0)
            scores.append(lax.dot_general(q, k_cat[kv][sb * sq:sb * sq + win],
                                          (((1,), (1,)), ((), ())),
                                          preferred_element_type=F32) + bias)
        probs = []
        for kv, s in enumerate(scores):
            mx = s[:, :LANES]
            for cb in range(1, n_col):
                mx = jnp.maximum(mx, s[:, cb * LANES:(cb + 1) * LANES])
            sink = sink_ref[kv * group * sq:(kv + 1) * group * sq, :]
            m = jnp.maximum(jnp.broadcast_to(jnp.max(mx, axis=-1, keepdims=True), mx.shape), sink)
            probs.append((m, jnp.exp2(s - jnp.concatenate([m] * n_col, axis=1)).astype(BF16)))
        for kv, (m, p) in enumerate(probs):
            v_ext = jnp.concatenate([v_cat[kv][sb * sq:sb * sq + win], ones_blk], axis=1)
            o_ext = jnp.dot(p, v_ext, preferred_element_type=F32)
            sink = sink_ref[kv * group * sq:(kv + 1) * group * sq, :]
            o = o_ext[:, :HEAD_DIM] / (o_ext[:, HEAD_DIM:] + jnp.exp2(sink - m))
            for c in range(group):
                hd = kv * group + c
                o_ref[rows, hd * HEAD_DIM:(hd + 1) * HEAD_DIM] = (
                    o[c * sq:(c + 1) * sq].astype(o_ref.dtype))


def _banded_attention(qk, v, sink, *, n_q, n_kv, hw, name):
    blk, sq = A_BLOCK, A_TILE
    nb = SEQ // blk
    group = n_q // n_kv
    qw, kw = n_q * HEAD_DIM, n_kv * HEAD_DIM
    win = sq + 2 * hw
    rows = jnp.arange(sq, dtype=jnp.int32)[:, None]
    cols = jnp.arange(win, dtype=jnp.int32)[None, :]
    band = jnp.where(jnp.abs(rows + hw - cols) <= hw, 0.0, NEG).astype(F32)
    sink_rep = jnp.broadcast_to(jnp.repeat(sink, sq)[:, None], (n_q * sq, LANES))

    def kv_spec(col, shift):
        return pl.BlockSpec((blk, kw), lambda i: (jnp.clip(i + shift, 0, nb - 1), col))

    k_col = qw // kw
    return pl.pallas_call(
        functools.partial(_banded_kernel, hw=hw, sq=sq, n_kv=n_kv, group=group),
        grid=(nb,),
        in_specs=[pl.BlockSpec((blk, qw), lambda i: (i, 0)),
                  kv_spec(k_col, -1), kv_spec(k_col, 0), kv_spec(k_col, 1),
                  kv_spec(0, -1), kv_spec(0, 0), kv_spec(0, 1),
                  pl.BlockSpec((sq, win), lambda i: (0, 0)),
                  pl.BlockSpec((n_q * sq, LANES), lambda i: (0, 0))],
        out_specs=pl.BlockSpec((blk, qw), lambda i: (i, 0)),
        out_shape=jax.ShapeDtypeStruct((SEQ, qw), BF16),
        compiler_params=_params("parallel"),
        name=name,
    )(qk, qk, qk, qk, v, v, v, band, sink_rep)


def _flash_kernel(*refs, tk, n_cast):
    q_ref, k_ref, v_ref = refs[:3]
    w_refs = refs[3:3 + n_cast]
    o_ref = refs[3 + n_cast]
    wb_refs = refs[4 + n_cast:4 + 2 * n_cast]
    m_ref, acc_ref = refs[4 + 2 * n_cast:]
    tq = q_ref.shape[0]
    n_chunks = k_ref.shape[0] // tk
    n_col = tk // LANES
    m_ref[...] = jnp.full((tq, LANES), NEG, F32)
    acc_ref[...] = jnp.zeros((tq, B_HEAD_PAD), F32)
    q = q_ref[...]
    for w_ref, wb_ref in zip(w_refs, wb_refs):
        wb_ref[...] = w_ref[...].astype(BF16)

    for c in range(n_chunks):
        k = k_ref[c * tk:(c + 1) * tk, :]
        v = v_ref[c * tk:(c + 1) * tk, :]
        s = lax.dot_general(q, k, (((1,), (1,)), ((), ())), preferred_element_type=F32)
        mx = s[:, :LANES]
        for cb in range(1, n_col):
            mx = jnp.maximum(mx, s[:, cb * LANES:(cb + 1) * LANES])
        m_old = m_ref[...]
        m_new = jnp.maximum(m_old, jnp.max(mx, axis=-1, keepdims=True))
        alpha = jnp.exp2(m_old - m_new)
        p = jnp.exp2(s - jnp.concatenate([m_new] * n_col, axis=1)).astype(BF16)
        pv = jnp.dot(p, v, preferred_element_type=F32)
        acc_ref[...] = acc_ref[...] * jnp.concatenate([alpha, alpha], axis=1) + pv
        m_ref[...] = m_new

    acc = acc_ref[...]
    o_ref[...] = (acc[:, :B_NOPE] / acc[:, B_NOPE:]).astype(o_ref.dtype)


def _dense_attention(q, k, v_ext, casts):
    tq, tk = 1024, 256
    nq = SEQ // tq
    kv_spec = pl.BlockSpec((SEQ, B_HEAD_PAD), lambda h, i: (0, h))
    w_specs, wb_specs, wb_shapes = [], [], []
    for w, first, n_layers, rows in casts:
        _, kdim, ndim = w.shape
        per_layer = kdim // rows
        last = n_layers * per_layer - 1
        assert last < B_HEADS * nq and kdim % rows == 0

        def slab(h, i, per_layer=per_layer, last=last):
            s = jnp.minimum(h * nq + i, last)
            return s // per_layer, s % per_layer

        w_specs.append(pl.BlockSpec(
            (None, rows, ndim),
            lambda h, i, slab=slab, first=first: (first + slab(h, i)[0], slab(h, i)[1], 0)))
        wb_specs.append(pl.BlockSpec(
            (None, rows, ndim), lambda h, i, slab=slab: (slab(h, i)[0], slab(h, i)[1], 0)))
        wb_shapes.append(jax.ShapeDtypeStruct((n_layers, kdim, ndim), BF16))
    outs = pl.pallas_call(
        functools.partial(_flash_kernel, tk=tk, n_cast=len(casts)),
        grid=(B_HEADS, nq),
        in_specs=[pl.BlockSpec((tq, B_HEAD_PAD), lambda h, i: (i, h)), kv_spec, kv_spec] + w_specs,
        out_specs=[pl.BlockSpec((tq, B_NOPE), lambda h, i: (i, h))] + wb_specs,
        out_shape=[jax.ShapeDtypeStruct((SEQ, B_HEADS * B_NOPE), BF16)] + wb_shapes,
        scratch_shapes=[pltpu.VMEM((tq, LANES), F32), pltpu.VMEM((tq, B_HEAD_PAD), F32)],
        compiler_params=_params("arbitrary", "arbitrary"),
        name="b_flash_attention",
    )(q, k, v_ext, *[c[0] for c in casts])
    return outs[0], outs[1:]


def _out_proj_kernel(o_ref, w_ref, h_ref, out_ref, *, n_sub):
    sub = o_ref.shape[0] // n_sub
    for c in range(n_sub):
        rows = slice(c * sub, (c + 1) * sub)
        out_ref[rows, :] = h_ref[rows, :] + jnp.dot(o_ref[rows, :], w_ref[...],
                                                    preferred_element_type=F32)


def _out_projection(o, w, slot, h, name):
    tm = 1024
    row = pl.BlockSpec((tm, D_MODEL), lambda i: (i, 0))
    return pl.pallas_call(
        functools.partial(_out_proj_kernel, n_sub=4),
        grid=(SEQ // tm,),
        in_specs=[row,
                  pl.BlockSpec((None, D_MODEL, D_MODEL), lambda i: (slot, 0, 0),
                               pipeline_mode=pl.Buffered(1)),
                  row],
        out_specs=row,
        out_shape=jax.ShapeDtypeStruct((SEQ, D_MODEL), F32),
        compiler_params=_params("parallel"),
        name=name,
    )(o, w, h)


def _ffn_kernel(x_ref, g_ref, wg_ref, wu_ref, wd_ref, o_ref, xn_ref):
    @pl.when(pl.program_id(1) == 0)
    def _():
        x = x_ref[...]
        xn_ref[...] = (_rms_scale(x, D_MODEL) * g_ref[...]).astype(BF16)
        o_ref[...] = x

    xn = xn_ref[...]
    gate = jnp.dot(xn, wg_ref[...], preferred_element_type=F32)
    up = jnp.dot(xn, wu_ref[...], preferred_element_type=F32)
    act = (gate * jax.nn.sigmoid(gate) * up).astype(BF16)
    o_ref[...] += jnp.dot(act, wd_ref[...], preferred_element_type=F32)


def _ffn(h, g, wg, wu, wd, layer):
    tm, tf = 1024, 512
    return pl.pallas_call(
        _ffn_kernel,
        grid=(SEQ // tm, D_FF // tf),
        in_specs=[pl.BlockSpec((tm, D_MODEL), lambda i, f: (i, 0)),
                  pl.BlockSpec((1, D_MODEL), lambda i, f: (0, 0)),
                  pl.BlockSpec((None, D_MODEL, tf), lambda i, f: (layer, 0, f)),
                  pl.BlockSpec((None, D_MODEL, tf), lambda i, f: (layer, 0, f)),
                  pl.BlockSpec((None, tf, D_MODEL), lambda i, f: (layer, f, 0))],
        out_specs=pl.BlockSpec((tm, D_MODEL), lambda i, f: (i, 0)),
        out_shape=jax.ShapeDtypeStruct((SEQ, D_MODEL), F32),
        scratch_shapes=[pltpu.VMEM((tm, D_MODEL), BF16)],
        compiler_params=_params("parallel", "arbitrary"),
        name="ffn_swiglu",
    )(h, g, wg, wu, wd)


def _ple_kernel(x_ref, g_ref, wg_ref, p_ref, wp_ref, o_ref, *, n_sub):
    sub = x_ref.shape[0] // n_sub
    for c in range(n_sub):
        rows = slice(c * sub, (c + 1) * sub)
        x = x_ref[rows, :]
        xn = (_rms_scale(x, D_MODEL) * g_ref[...]).astype(BF16)
        gate = jnp.dot(xn, wg_ref[...], preferred_element_type=F32)
        proj = jnp.dot(p_ref[rows, :].astype(BF16), wp_ref[...], preferred_element_type=F32)
        o_ref[rows, :] = x + jax.nn.sigmoid(gate) * proj


def _ple(h, g, w_gate, gate_slot, p, w_proj, layer):
    tm = 1024
    resident = pl.Buffered(1)
    return pl.pallas_call(
        functools.partial(_ple_kernel, n_sub=4),
        grid=(SEQ // tm,),
        in_specs=[pl.BlockSpec((tm, D_MODEL), lambda i: (i, 0)),
                  pl.BlockSpec((1, D_MODEL), lambda i: (0, 0)),
                  pl.BlockSpec((None, D_MODEL, D_MODEL), lambda i: (gate_slot, 0, 0),
                               pipeline_mode=resident),
                  pl.BlockSpec((None, tm, PLE_DIM), lambda i: (layer, i, 0)),
                  pl.BlockSpec((None, PLE_DIM, D_MODEL), lambda i: (layer, 0, 0),
                               pipeline_mode=resident)],
        out_specs=pl.BlockSpec((tm, D_MODEL), lambda i: (i, 0)),
        out_shape=jax.ShapeDtypeStruct((SEQ, D_MODEL), F32),
        compiler_params=_params("parallel"),
        name="ple_gate",
    )(h, g, w_gate, p, w_proj)


def _mixer_a(h, g_mix, w_in, slot, gq, gk, sink, w_o, tables):
    nq, nk = A_HEADS * HEAD_DIM, A_KV_HEADS * HEAD_DIM
    scale = LOG2_E / math.sqrt(HEAD_DIM)
    head_gain = jnp.concatenate([jnp.tile(gq * scale, A_HEADS), jnp.tile(gk, A_KV_HEADS)])[None, :]
    qk, v = _qkv_projection(h, g_mix, w_in, slot, nq + nk, nk, head_gain, tables, "a")
    o = _banded_attention(qk, v, sink * LOG2_E, n_q=A_HEADS, n_kv=A_KV_HEADS, hw=A_HALF_WINDOW,
                          name="a_banded_attention")
    return _out_projection(o, w_o, slot, h, "a_out_proj")


def _mixer_b(h, g_mix, w_in, g_qlat, g_kvlat, w_q_up, w_kv_up, slot, gq, gk, w_o, tables, casts):
    scale = LOG2_E / math.sqrt(B_QK)
    half = B_ROPE // 2
    n_lat = B_Q_RANK + B_KV_RANK
    w_in = w_in[slot]
    w_in_pad = jnp.concatenate([w_in[:, :n_lat], _rotary_slab(w_in[:, n_lat:], half)], axis=1)[None]
    tm = 1024
    lat = _norm_matmul(
        h, 0, D_MODEL, g_mix, w_in_pad, 0, 0, 1, tm=tm, tn=B_IN_PAD, extra=(), extra_specs=[],
        epilogue=_plain_epilogue,
        out_shape=jax.ShapeDtypeStruct((SEQ, B_IN_PAD), F32),
        out_specs=pl.BlockSpec((tm, B_IN_PAD), lambda i, j: (i, j)),
        name="b_latent_proj")

    tab = pl.BlockSpec((tm, LANES), lambda i, j: (i, 0))
    tn = 2048
    n_slab = B_HEADS * B_HEAD_PAD

    def head_slab(x):
        return jnp.concatenate([x[..., :B_NOPE], _rotary_slab(x[..., B_NOPE:], half)], axis=-1)

    wq = head_slab(w_q_up[slot].reshape(B_Q_RANK, B_HEADS, B_QK)).reshape(B_Q_RANK, n_slab)[None]
    gq_slab = jnp.tile(head_slab(gq * scale), B_HEADS)[None, :]
    q = _norm_matmul(
        lat, 0, B_Q_RANK, g_qlat, wq, 0, 0, n_slab // tn, tm=tm, tn=tn,
        extra=(gq_slab,) + tuple(tables),
        extra_specs=[pl.BlockSpec((1, tn), lambda i, j: (0, j)), tab, tab],
        epilogue=_b_q_epilogue,
        out_shape=jax.ShapeDtypeStruct((SEQ, n_slab), BF16),
        out_specs=pl.BlockSpec((tm, tn), lambda i, j: (i, j)),
        name="b_q_proj")

    gk_slab = head_slab(gk)[None, :]
    slab_out = pl.BlockSpec((tm, tn), lambda i, j: (i, j))
    k, v_ext = _norm_matmul(
        lat, 1, B_KV_RANK, g_kvlat, w_kv_up, slot, 0, n_slab // tn, tm=tm, tn=tn,
        extra=(gk_slab, lat) + tuple(tables),
        extra_specs=[pl.BlockSpec((1, B_HEAD_PAD), lambda i, j: (0, 0)),
                     pl.BlockSpec((tm, LANES), lambda i, j: (i, n_lat // LANES)),
                     tab, tab],
        epilogue=_b_kv_epilogue,
        out_shape=[jax.ShapeDtypeStruct((SEQ, n_slab), BF16)] * 2,
        out_specs=[slab_out, slab_out],
        name="b_kv_proj")
    o, cast_weights = _dense_attention(q, k, v_ext, casts)
    return _out_projection(o, w_o, slot, h, "b_out_proj"), cast_weights


def _dilated_kernel(q0_ref, q1_ref, q2_ref, kp_ref, kc_ref, kn_ref, vp_ref, vc_ref, vn_ref,
                    band_ref, o_ref, *scratch):
    i = pl.program_id(0)
    blk = q0_ref.shape[0]
    band = band_ref[...]
    win = band.shape[1]
    ones_blk = jnp.ones((win, LANES), BF16)
    col = lax.broadcasted_iota(jnp.int32, (1, win), 1)
    n4, n16 = blk // HOP, blk // (HOP * HOP)
    scratch = list(scratch)
    take = lambda k: [scratch.pop(0) for _ in range(k)]
    o_s, l_s = take(C_GROUPS), take(C_GROUPS)
    hop1 = take(8)
    hop2 = take(7)
    o2_hop1, l2_hop1, merged_tok = take(3)

    for src, dst in zip((q1_ref, q2_ref, kp_ref, kc_ref, kn_ref, vp_ref, vc_ref, vn_ref), hop1):
        for r in range(HOP):
            dst[r * n4:(r + 1) * n4, :] = src[pl.ds(r, n4, stride=HOP), :]
    for src, dst in zip(hop1[1:], hop2):
        for r in range(HOP * HOP):
            dst[r * n16:(r + 1) * n16, :] = src[pl.ds((r % HOP) * n4 + r // HOP, n16, stride=HOP), :]
    operands = ((q0_ref, kp_ref, kc_ref, kn_ref, vp_ref, vc_ref, vn_ref),
                (hop1[0],) + tuple(hop1[2:]),
                tuple(hop2))

    for g, ((window, d), refs) in enumerate(zip(C_PATTERNS, operands)):
        q_ref, kprev, kcur, knext, vprev, vcur, vnext = refs
        n = blk // d
        rq = min(n, win - 2 * C_HALF)
        pad_rows = win - min(n + 2 * C_HALF, win)
        tiles = []
        for r in range(d):
            def chain_window(prev_ref, cur_ref, next_ref):
                parts = [prev_ref[(r + 1) * n - C_HALF:(r + 1) * n, :], cur_ref[r * n:(r + 1) * n, :],
                         next_ref[r * n:r * n + C_HALF, :]]
                if pad_rows:
                    parts.append(jnp.zeros((pad_rows, LANES), F32))
                return jnp.concatenate(parts, axis=0).astype(BF16)

            k_chain = chain_window(kprev, kcur, knext)
            v_chain = chain_window(vprev, vcur, vnext)
            for sb in range(n // rq):
                kpos = i * n + sb * rq - C_HALF + col
                edge = jnp.where((kpos >= 0) & (kpos < SEQ // d), 0.0, NEG)
                dst = slice(r * n + sb * rq, r * n + (sb + 1) * rq)
                tiles.append((dst, q_ref[dst, :].astype(BF16), k_chain[sb * rq:sb * rq + win],
                              v_chain[sb * rq:sb * rq + win], band[:rq] + edge))
        full = win - 2 * C_HALF
        batch = 1 if rq == full else 4 * full // rq
        for b0 in range(0, len(tiles), batch):
            group_tiles = tiles[b0:b0 + batch]
            scores = [lax.dot_general(q, k_win, (((1,), (1,)), ((), ())),
                                      preferred_element_type=F32) + bias
                      for _, q, k_win, _, bias in group_tiles]
            probs = []
            for s in scores:
                mx = jnp.maximum(s[:, :LANES], s[:, LANES:])
                m = jnp.broadcast_to(jnp.max(mx, axis=-1, keepdims=True), mx.shape)
                probs.append((m, jnp.exp2(s - jnp.concatenate([m, m], axis=1)).astype(BF16)))
            for (dst, _, _, v_win, _), (m, p) in zip(group_tiles, probs):
                o_ext = jnp.dot(p, jnp.concatenate([v_win, ones_blk], axis=1),
                                preferred_element_type=F32)
                denom = o_ext[:, HEAD_DIM:]
                o_s[g][dst, :] = o_ext[:, :HEAD_DIM] / denom
                l_s[g][dst, :] = m + jnp.log2(denom)

    for src, dst in ((o_s[2], o2_hop1), (l_s[2], l2_hop1)):
        for r in range(HOP * HOP):
            dst[pl.ds((r % HOP) * n4 + r // HOP, n16, stride=HOP), :] = src[r * n16:(r + 1) * n16, :]
    for r in range(HOP):
        rows = slice(r * n4, (r + 1) * n4)
        tok = pl.ds(r, n4, stride=HOP)
        l0, l1, l2 = l_s[0][tok, :], l_s[1][rows, :], l2_hop1[rows, :]
        m = jnp.maximum(jnp.maximum(l0, l1), l2)
        e0, e1, e2 = jnp.exp2(l0 - m), jnp.exp2(l1 - m), jnp.exp2(l2 - m)
        merged = (e0 * o_s[0][tok, :] + e1 * o_s[1][rows, :] + e2 * o2_hop1[rows, :]) / (e0 + e1 + e2)
        merged_tok[tok, :] = merged
    o_ref[...] = merged_tok[...].astype(o_ref.dtype)


def _dilated_attention(qk, v):
    blk = 1024
    nb = SEQ // blk
    win = 2 * LANES
    rows = jnp.arange(LANES, dtype=jnp.int32)[:, None]
    cols = jnp.arange(win, dtype=jnp.int32)[None, :]
    band = jnp.where(jnp.abs(rows + C_HALF - cols) <= C_HALF, 0.0, NEG).astype(F32)

    def slab(col0, shift):
        return pl.BlockSpec((blk, HEAD_DIM),
                            lambda i, h: (jnp.clip(i + shift, 0, nb - 1), col0 + h))

    k_col = C_GROUPS * C_HEADS
    in_specs = [slab(g * C_HEADS, 0) for g in range(C_GROUPS)]
    in_specs += [slab(k_col, -1), slab(k_col, 0), slab(k_col, 1)]
    in_specs += [slab(0, -1), slab(0, 0), slab(0, 1)]
    in_specs.append(pl.BlockSpec((LANES, win), lambda i, h: (0, 0)))
    return pl.pallas_call(
        _dilated_kernel,
        grid=(nb, C_HEADS),
        in_specs=in_specs,
        out_specs=pl.BlockSpec((blk, HEAD_DIM), lambda i, h: (i, h)),
        out_shape=jax.ShapeDtypeStruct((SEQ, C_HEADS * HEAD_DIM), BF16),
        scratch_shapes=[pltpu.VMEM((blk, LANES), F32)] * (2 * C_GROUPS + 8 + 7 + 3),
        compiler_params=_params("parallel", "parallel"),
        name="c_dilated_attention",
    )(qk, qk, qk, qk, qk, qk, v, v, v, band)


def _mixer_c(h, g_mix, w_in, slot, gq, gk, w_o, tables):
    n_qh = C_GROUPS * C_HEADS
    nq = n_qh * HEAD_DIM
    nkv = C_HEADS * HEAD_DIM
    scale = LOG2_E / math.sqrt(HEAD_DIM)
    head_gain = jnp.concatenate([jnp.tile(gq * scale, n_qh), jnp.tile(gk, C_HEADS)])[None, :]
    qk, v = _qkv_projection(h, g_mix, w_in, slot, nq + nkv, nkv, head_gain, tables, "c", dtype=F32)
    o = _dilated_attention(qk, v)
    return _out_projection(o, w_o, slot, h, "c_out_proj")


def kernel(x, p, positions, g_mix, g_ffn, g_ple, w_ple_gate, w_ple_proj,
           w_ffn_gate, w_ffn_up, w_ffn_down,
           a_w_in, a_q_norm, a_k_norm, a_sink, a_w_o,
           b_w_in, b_q_lat_norm, b_kv_lat_norm, b_w_q_up, b_w_kv_up, b_q_norm, b_k_norm, b_w_o,
           c_w_in, c_q_norm, c_k_norm, c_w_o):
    h = x.reshape(SEQ, D_MODEL)
    p = p.reshape(DEPTH, SEQ, PLE_DIM)
    pos_col = positions.reshape(SEQ, 1)
    tables_ac = _tables_adjacent(pos_col, PARTIAL_ROT)
    tables_b = _tables_split(pos_col, B_ROPE)
    (w_ple_proj, a_w_in, a_w_o, b_w_in, b_w_q_up, b_w_kv_up, b_w_o, c_w_o) = [
        w.astype(BF16) for w in (w_ple_proj, a_w_in, a_w_o, b_w_in, b_w_q_up, b_w_kv_up, b_w_o, c_w_o)]
    late = {"ffn_gate": (w_ffn_gate, 1, DEPTH - 1, 64), "ffn_up": (w_ffn_up, 1, DEPTH - 1, 64),
            "ffn_down": (w_ffn_down, 1, DEPTH - 1, 176), "ple_gate": (w_ple_gate, 1, DEPTH - 1, 64),
            "c_in": (c_w_in, 0, 1, 16)}
    early = {name: _cast_layers(w, first, 4 * rows)
             for name, (w, first, _, rows) in late.items() if first > 0}
    cast = {}

    def weight(name, layer):
        first = late[name][1]
        if layer < first:
            return early[name], layer
        return cast[name], layer - first

    for i in range(DEPTH):
        kind, slot = i % 3, i // 3
        gm = g_mix[i][None, :]
        if kind == 0:
            h = _mixer_a(h, gm, a_w_in, slot, a_q_norm[slot], a_k_norm[slot], a_sink[slot],
                         a_w_o, tables_ac)
        elif kind == 1:
            h, cast_weights = _mixer_b(
                h, gm, b_w_in, b_q_lat_norm[slot][None, :], b_kv_lat_norm[slot][None, :],
                b_w_q_up, b_w_kv_up, slot, b_q_norm[slot], b_k_norm[slot], b_w_o, tables_b,
                list(late.values()))
            cast = dict(zip(late, cast_weights))
        else:
            h = _mixer_c(h, gm, *weight("c_in", slot), c_q_norm[slot], c_k_norm[slot], c_w_o,
                         tables_ac)
        (wg, l), (wu, _), (wd, _) = weight("ffn_gate", i), weight("ffn_up", i), weight("ffn_down", i)
        h = _ffn(h, g_ffn[i][None, :], wg, wu, wd, l)
        wpg, l = weight("ple_gate", i)
        h = _ple(h, g_ple[i][None, :], wpg, l, p, w_ple_proj, i)
    return h.reshape(1, SEQ, D_MODEL)
```

```python
import functools
import math

import jax
import jax.numpy as jnp
from jax import lax
from jax.experimental import pallas as pl
from jax.experimental.pallas import tpu as pltpu

F32 = jnp.float32
BF16 = jnp.bfloat16

SEQ = 8192
D_MODEL = 2048
DEPTH = 4
HEAD_DIM = 128
ROPE_THETA = 500000.0
PARTIAL_ROT = HEAD_DIM // 4
NORM_EPS = 1e-6
NEG = -1e30
LOG2_E = math.log2(math.e)
LANES = 128
HALF_LANES = LANES // 2

A_HEADS = 16
A_KV_HEADS = 4
A_HALF_WINDOW = 128
A_BLOCK = 512
A_TILE = 128
B_HEADS = 16
B_Q_RANK = 512
B_KV_RANK = 512
B_NOPE = 128
B_ROPE = 64
B_QK = B_NOPE + B_ROPE
B_HEAD_PAD = 256
B_IN_PAD = B_Q_RANK + B_KV_RANK + LANES
C_PATTERNS = ((128, 1), (512, 4), (2048, 16))
C_GROUPS = 3
C_HEADS = 16
C_HALF = 64
assert all(window // 2 // dil == C_HALF for window, dil in C_PATTERNS)
HOP = 4
assert tuple(dil for _, dil in C_PATTERNS) == (1, HOP, HOP * HOP)
D_FF = 5632
PLE_DIM = 256

VMEM_LIMIT = 56 * 1024 * 1024


def _params(*sem):
    return pltpu.CompilerParams(dimension_semantics=sem, vmem_limit_bytes=VMEM_LIMIT)


def _rms_scale(x, width):
    ss = jnp.sum(x * x, axis=-1, keepdims=True)
    return x * lax.rsqrt(ss * (1.0 / width) + NORM_EPS)


def _rope_adjacent(y, cos, sin_lo, sin_hi, half):
    return (y * cos + pltpu.roll(y, LANES - half, 1) * sin_lo
            + pltpu.roll(y, half, 1) * sin_hi)


def _rope_split(y, cos, sin):
    return y * cos + pltpu.roll(y, HALF_LANES, 1) * sin


def _rotary_slab(x, half):
    pad = jnp.zeros(x.shape[:-1] + (HALF_LANES - half,), x.dtype)
    return jnp.concatenate([x[..., :half], pad, x[..., half:], pad], axis=-1)


def _cast_kernel(w_ref, o_ref):
    o_ref[...] = w_ref[...].astype(o_ref.dtype)


def _cast_layers(w, n_layers, rows):
    _, kdim, ndim = w.shape
    assert kdim % rows == 0
    spec = pl.BlockSpec((None, rows, ndim), lambda l, r: (l, r, 0))
    return pl.pallas_call(
        _cast_kernel,
        grid=(n_layers, kdim // rows),
        in_specs=[spec],
        out_specs=spec,
        out_shape=jax.ShapeDtypeStruct((n_layers, kdim, ndim), BF16),
        compiler_params=_params("parallel", "parallel"),
        name="cast_early_weights",
    )(w)


def _rope_table_kernel(pos_ref, inv_ref, sign_ref, cos_ref, *sin_refs):
    ang = pos_ref[...].astype(F32) * inv_ref[...]
    cos_ref[...] = jnp.cos(ang)
    s = jnp.sin(ang)
    for r, sin_ref in enumerate(sin_refs):
        sin_ref[...] = s * sign_ref[r:r + 1, :]


def _rope_tables(pos_col, inv_lanes, signs):
    tm = 1024
    n = signs.shape[0]
    tab = pl.BlockSpec((tm, LANES), lambda i: (i, 0))
    return pl.pallas_call(
        _rope_table_kernel,
        grid=(SEQ // tm,),
        in_specs=[pl.BlockSpec((tm, 1), lambda i: (i, 0)),
                  pl.BlockSpec((1, LANES), lambda i: (0, 0)),
                  pl.BlockSpec((n, LANES), lambda i: (0, 0))],
        out_specs=[tab] * (n + 1),
        out_shape=[jax.ShapeDtypeStruct((SEQ, LANES), F32)] * (n + 1),
        compiler_params=_params("parallel"),
        name="rope_tables",
    )(pos_col, inv_lanes, signs)


def _inv_freq(rot_dim):
    half = rot_dim // 2
    return ROPE_THETA ** (-jnp.arange(half, dtype=F32) * 2.0 / rot_dim)


def _tables_adjacent(pos_col, rot_dim):
    half = rot_dim // 2
    inv = _inv_freq(rot_dim)
    rest = jnp.zeros((LANES - rot_dim,), F32)
    zero, one = jnp.zeros((half,), F32), jnp.ones((half,), F32)
    inv_l = jnp.concatenate([inv, inv, rest])[None, :]
    signs = jnp.stack([jnp.concatenate([-one, zero, rest]), jnp.concatenate([zero, one, rest])])
    return _rope_tables(pos_col, inv_l, signs)


def _tables_split(pos_col, rot_dim):
    half = rot_dim // 2
    inv = _inv_freq(rot_dim)
    one = jnp.ones((half,), F32)
    inv_l = _rotary_slab(jnp.concatenate([inv, inv]), half)[None, :]
    signs = _rotary_slab(jnp.concatenate([-one, one]), half)[None, :]
    return _rope_tables(pos_col, inv_l, signs)


def _norm_matmul_kernel(x_ref, g_ref, w_ref, *rest, n_extra, epilogue, n_sub):
    extra = rest[:n_extra]
    outs = rest[n_extra:-1]
    xn_ref = rest[-1]

    sub = xn_ref.shape[0] // n_sub

    def column_tile(first):
        for c in range(n_sub):
            rows = slice(c * sub, (c + 1) * sub)
            if first:
                x = x_ref[rows, :]
                xn_ref[rows, :] = (_rms_scale(x, x.shape[-1]) * g_ref[...]).astype(BF16)
            acc = jnp.dot(xn_ref[rows, :], w_ref[...], preferred_element_type=F32)
            epilogue(acc, rows, extra, outs)

    @pl.when(pl.program_id(1) == 0)
    def _():
        column_tile(True)

    @pl.when(pl.program_id(1) > 0)
    def _():
        column_tile(False)


def _norm_matmul(x, x_col, kin, gain, w, w_slot, w_col0, n_tiles, *, tm, tn, extra, extra_specs,
                 epilogue, out_shape, out_specs, name, n_sub=4):
    m = x.shape[0]
    kernel = functools.partial(_norm_matmul_kernel, n_extra=len(extra), epilogue=epilogue,
                               n_sub=n_sub)
    return pl.pallas_call(
        kernel,
        grid=(m // tm, n_tiles),
        in_specs=[pl.BlockSpec((tm, kin), lambda i, j: (i, x_col)),
                  pl.BlockSpec((1, kin), lambda i, j: (0, 0)),
                  pl.BlockSpec((None, kin, tn), lambda i, j: (w_slot, 0, w_col0 + j))]
        + list(extra_specs),
        out_specs=out_specs,
        out_shape=out_shape,
        scratch_shapes=[pltpu.VMEM((tm, kin), BF16)],
        compiler_params=_params("parallel", "arbitrary"),
        name=name,
    )(x, gain, w, *extra)


def _plain_epilogue(acc, rows, extra, outs):
    outs[0][rows, :] = acc.astype(outs[0].dtype)


def _head_norm_rope_epilogue(acc, rows, extra, outs):
    gain_ref, cos_ref, slo_ref, shi_ref = extra
    (o_ref,) = outs
    cos, slo, shi = cos_ref[rows, :], slo_ref[rows, :], shi_ref[rows, :]
    for c in range(acc.shape[1] // HEAD_DIM):
        cols = slice(c * HEAD_DIM, (c + 1) * HEAD_DIM)
        y = _rms_scale(acc[:, cols], HEAD_DIM) * gain_ref[:, cols]
        o_ref[rows, cols] = _rope_adjacent(y, cos, slo, shi, PARTIAL_ROT // 2).astype(o_ref.dtype)


def _qkv_projection(h, g_mix, w, slot, n_qk, n_v, head_gain, tables, name, dtype=BF16):
    tm = 1024
    tn = next(t for t in (1280, 1024, 512) if n_qk % t == 0)
    tn_v = next(t for t in (1024, 512) if n_v % t == 0 and n_qk % t == 0)
    tab = pl.BlockSpec((tm, LANES), lambda i, j: (i, 0))
    qk = _norm_matmul(
        h, 0, D_MODEL, g_mix, w, slot, 0, n_qk // tn, tm=tm, tn=tn,
        extra=(head_gain,) + tuple(tables),
        extra_specs=[pl.BlockSpec((1, tn), lambda i, j: (0, j)), tab, tab, tab],
        epilogue=_head_norm_rope_epilogue,
        out_shape=jax.ShapeDtypeStruct((SEQ, n_qk), dtype),
        out_specs=pl.BlockSpec((tm, tn), lambda i, j: (i, j)),
        name=name + "_qk_proj")
    v = _norm_matmul(
        h, 0, D_MODEL, g_mix, w, slot, n_qk // tn_v, n_v // tn_v, tm=tm, tn=tn_v,
        extra=(), extra_specs=[], epilogue=_plain_epilogue,
        out_shape=jax.ShapeDtypeStruct((SEQ, n_v), dtype),
        out_specs=pl.BlockSpec((tm, tn_v), lambda i, j: (i, j)),
        name=name + "_v_proj")
    return qk, v


def _b_q_epilogue(acc, rows, extra, outs):
    gain_ref, cos_ref, sin_ref = extra
    (o_ref,) = outs
    cos, sin = cos_ref[rows, :], sin_ref[rows, :]
    for c in range(acc.shape[1] // B_HEAD_PAD):
        c0 = c * B_HEAD_PAD
        y = _rms_scale(acc[:, c0:c0 + B_HEAD_PAD], B_QK) * gain_ref[:, c0:c0 + B_HEAD_PAD]
        o_ref[rows, c0:c0 + B_NOPE] = y[:, :B_NOPE].astype(BF16)
        o_ref[rows, c0 + B_NOPE:c0 + B_HEAD_PAD] = _rope_split(y[:, B_NOPE:], cos, sin).astype(BF16)


def _b_kv_epilogue(acc, rows, extra, outs):
    gain_ref, krope_ref, cos_ref, sin_ref = extra
    k_ref, v_ref = outs
    cos, sin = cos_ref[rows, :], sin_ref[rows, :]
    kr = krope_ref[rows, :]
    kr_ss = jnp.sum(kr * kr, axis=-1, keepdims=True)
    g_nope = gain_ref[:, :B_NOPE]
    kr_rot = _rope_split(kr * gain_ref[:, B_NOPE:], cos, sin)
    ones_blk = jnp.ones((acc.shape[0], LANES), BF16)
    for c in range(acc.shape[1] // B_HEAD_PAD):
        c0 = c * B_HEAD_PAD
        y = acc[:, c0:c0 + B_NOPE]
        ss = jnp.sum(y * y, axis=-1, keepdims=True) + kr_ss
        rinv = lax.rsqrt(ss * (1.0 / B_QK) + NORM_EPS)
        k_ref[rows, c0:c0 + B_NOPE] = (y * rinv * g_nope).astype(BF16)
        k_ref[rows, c0 + B_NOPE:c0 + B_HEAD_PAD] = (kr_rot * rinv).astype(BF16)
        v_ref[rows, c0:c0 + B_NOPE] = acc[:, c0 + B_NOPE:c0 + B_HEAD_PAD].astype(BF16)
        v_ref[rows, c0 + B_NOPE:c0 + B_HEAD_PAD] = ones_blk


def _banded_kernel(q_ref, kp_ref, kc_ref, kn_ref, vp_ref, vc_ref, vn_ref, band_ref, sink_ref,
                   o_ref, *, hw, sq, n_kv, group):
    i = pl.program_id(0)
    blk = q_ref.shape[0]
    win = sq + 2 * hw
    n_col = win // LANES
    ones_blk = jnp.ones((win, LANES), BF16)
    col = lax.broadcasted_iota(jnp.int32, (1, win), 1)

    def halo_concat(prev_ref, cur_ref, next_ref, kv):
        cols = slice(kv * HEAD_DIM, (kv + 1) * HEAD_DIM)
        return jnp.concatenate(
            [prev_ref[blk - hw:, cols], cur_ref[:, cols], next_ref[:hw, cols]], axis=0)

    k_cat = [halo_concat(kp_ref, kc_ref, kn_ref, kv) for kv in range(n_kv)]
    v_cat = [halo_concat(vp_ref, vc_ref, vn_ref, kv) for kv in range(n_kv)]
    for sb in range(blk // sq):
        rows = slice(sb * sq, (sb + 1) * sq)
        kpos = i * blk + sb * sq - hw + col
        edge = jnp.where((kpos >= 0) & (kpos < SEQ), 0.0, NEG)
        bias = jnp.concatenate([band_ref[...] + edge] * group, axis=0)
        scores = []
        for kv in range(n_kv):
            q = jnp.concatenate(
                [q_ref[rows, hd * HEAD_DIM:(hd + 1) * HEAD_DIM]
                 for hd in range(kv * group, (kv + 1) * group)], axis=0)
            scores.append(lax.dot_general(q, k_cat[kv][sb * sq:sb * sq + win],
                                          (((1,), (1,)), ((), ())),
                                          preferred_element_type=F32) + bias)
        probs = []
        for kv, s in enumerate(scores):
            mx = s[:, :LANES]
            for cb in range(1, n_col):
                mx = jnp.maximum(mx, s[:, cb * LANES:(cb + 1) * LANES])
            sink = sink_ref[kv * group * sq:(kv + 1) * group * sq, :]
            m = jnp.maximum(jnp.broadcast_to(jnp.max(mx, axis=-1, keepdims=True), mx.shape), sink)
            probs.append((m, jnp.exp2(s - jnp.concatenate([m] * n_col, axis=1)).astype(BF16)))
        for kv, (m, p) in enumerate(probs):
            v_ext = jnp.concatenate([v_cat[kv][sb * sq:sb * sq + win], ones_blk], axis=1)
            o_ext = jnp.dot(p, v_ext, preferred_element_type=F32)
            sink = sink_ref[kv * group * sq:(kv + 1) * group * sq, :]
            o = o_ext[:, :HEAD_DIM] / (o_ext[:, HEAD_DIM:] + jnp.exp2(sink - m))
            for c in range(group):
                hd = kv * group + c
                o_ref[rows, hd * HEAD_DIM:(hd + 1) * HEAD_DIM] = (
                    o[c * sq:(c + 1) * sq].astype(o_ref.dtype))


def _banded_attention(qk, v, sink, *, n_q, n_kv, hw, name):
    blk, sq = A_BLOCK, A_TILE
    nb = SEQ // blk
    group = n_q // n_kv
    qw, kw = n_q * HEAD_DIM, n_kv * HEAD_DIM
    win = sq + 2 * hw
    rows = jnp.arange(sq, dtype=jnp.int32)[:, None]
    cols = jnp.arange(win, dtype=jnp.int32)[None, :]
    band = jnp.where(jnp.abs(rows + hw - cols) <= hw, 0.0, NEG).astype(F32)
    sink_rep = jnp.broadcast_to(jnp.repeat(sink, sq)[:, None], (n_q * sq, LANES))

    def kv_spec(col, shift):
        return pl.BlockSpec((blk, kw), lambda i: (jnp.clip(i + shift, 0, nb - 1), col))

    k_col = qw // kw
    return pl.pallas_call(
        functools.partial(_banded_kernel, hw=hw, sq=sq, n_kv=n_kv, group=group),
        grid=(nb,),
        in_specs=[pl.BlockSpec((blk, qw), lambda i: (i, 0)),
                  kv_spec(k_col, -1), kv_spec(k_col, 0), kv_spec(k_col, 1),
                  kv_spec(0, -1), kv_spec(0, 0), kv_spec(0, 1),
                  pl.BlockSpec((sq, win), lambda i: (0, 0)),
                  pl.BlockSpec((n_q * sq, LANES), lambda i: (0, 0))],
        out_specs=pl.BlockSpec((blk, qw), lambda i: (i, 0)),
        out_shape=jax.ShapeDtypeStruct((SEQ, qw), BF16),
        compiler_params=_params("parallel"),
        name=name,
    )(qk, qk, qk, qk, v, v, v, band, sink_rep)


def _flash_kernel(*refs, tk, n_cast):
    q_ref, k_ref, v_ref = refs[:3]
    w_refs = refs[3:3 + n_cast]
    o_ref = refs[3 + n_cast]
    wb_refs = refs[4 + n_cast:4 + 2 * n_cast]
    m_ref, acc_ref = refs[4 + 2 * n_cast:]
    tq = q_ref.shape[0]
    n_chunks = k_ref.shape[0] // tk
    n_col = tk // LANES
    m_ref[...] = jnp.full((tq, LANES), NEG, F32)
    acc_ref[...] = jnp.zeros((tq, B_HEAD_PAD), F32)
    q = q_ref[...]
    for w_ref, wb_ref in zip(w_refs, wb_refs):
        wb_ref[...] = w_ref[...].astype(BF16)

    for c in range(n_chunks):
        k = k_ref[c * tk:(c + 1) * tk, :]
        v = v_ref[c * tk:(c + 1) * tk, :]
        s = lax.dot_general(q, k, (((1,), (1,)), ((), ())), preferred_element_type=F32)
        mx = s[:, :LANES]
        for cb in range(1, n_col):
            mx = jnp.maximum(mx, s[:, cb * LANES:(cb + 1) * LANES])
        m_old = m_ref[...]
        m_new = jnp.maximum(m_old, jnp.max(mx, axis=-1, keepdims=True))
        alpha = jnp.exp2(m_old - m_new)
        p = jnp.exp2(s - jnp.concatenate([m_new] * n_col, axis=1)).astype(BF16)
        pv = jnp.dot(p, v, preferred_element_type=F32)
        acc_ref[...] = acc_ref[...] * jnp.concatenate([alpha, alpha], axis=1) + pv
        m_ref[...] = m_new

    acc = acc_ref[...]
    o_ref[...] = (acc[:, :B_NOPE] / acc[:, B_NOPE:]).astype(o_ref.dtype)


def _dense_attention(q, k, v_ext, casts):
    tq, tk = 1024, 256
    nq = SEQ // tq
    kv_spec = pl.BlockSpec((SEQ, B_HEAD_PAD), lambda h, i: (0, h))
    w_specs, wb_specs, wb_shapes = [], [], []
    for w, first, n_layers, rows in casts:
        _, kdim, ndim = w.shape
        per_layer = kdim // rows
        last = n_layers * per_layer - 1
        assert last < B_HEADS * nq and kdim % rows == 0

        def slab(h, i, per_layer=per_layer, last=last):
            s = jnp.minimum(h * nq + i, last)
            return s // per_layer, s % per_layer

        w_specs.append(pl.BlockSpec(
            (None, rows, ndim),
            lambda h, i, slab=slab, first=first: (first + slab(h, i)[0], slab(h, i)[1], 0)))
        wb_specs.append(pl.BlockSpec(
            (None, rows, ndim), lambda h, i, slab=slab: (slab(h, i)[0], slab(h, i)[1], 0)))
        wb_shapes.append(jax.ShapeDtypeStruct((n_layers, kdim, ndim), BF16))
    outs = pl.pallas_call(
        functools.partial(_flash_kernel, tk=tk, n_cast=len(casts)),
        grid=(B_HEADS, nq),
        in_specs=[pl.BlockSpec((tq, B_HEAD_PAD), lambda h, i: (i, h)), kv_spec, kv_spec] + w_specs,
        out_specs=[pl.BlockSpec((tq, B_NOPE), lambda h, i: (i, h))] + wb_specs,
        out_shape=[jax.ShapeDtypeStruct((SEQ, B_HEADS * B_NOPE), BF16)] + wb_shapes,
        scratch_shapes=[pltpu.VMEM((tq, LANES), F32), pltpu.VMEM((tq, B_HEAD_PAD), F32)],
        compiler_params=_params("arbitrary", "arbitrary"),
        name="b_flash_attention",
    )(q, k, v_ext, *[c[0] for c in casts])
    return outs[0], outs[1:]


def _out_proj_kernel(o_ref, w_ref, h_ref, out_ref, *, n_sub):
    sub = o_ref.shape[0] // n_sub
    for c in range(n_sub):
        rows = slice(c * sub, (c + 1) * sub)
        out_ref[rows, :] = h_ref[rows, :] + jnp.dot(o_ref[rows, :], w_ref[...],
                                                    preferred_element_type=F32)


def _out_projection(o, w, slot, h, name):
    tm = 1024
    row = pl.BlockSpec((tm, D_MODEL), lambda i: (i, 0))
    return pl.pallas_call(
        functools.partial(_out_proj_kernel, n_sub=4),
        grid=(SEQ // tm,),
        in_specs=[row,
                  pl.BlockSpec((None, D_MODEL, D_MODEL), lambda i: (slot, 0, 0),
                               pipeline_mode=pl.Buffered(1)),
                  row],
        out_specs=row,
        out_shape=jax.ShapeDtypeStruct((SEQ, D_MODEL), F32),
        compiler_params=_params("parallel"),
        name=name,
    )(o, w, h)


def _ffn_kernel(x_ref, g_ref, wg_ref, wu_ref, wd_ref, o_ref, xn_ref):
    @pl.when(pl.program_id(1) == 0)
    def _():
        x = x_ref[...]
        xn_ref[...] = (_rms_scale(x, D_MODEL) * g_ref[...]).astype(BF16)
        o_ref[...] = x

    xn = xn_ref[...]
    gate = jnp.dot(xn, wg_ref[...], preferred_element_type=F32)
    up = jnp.dot(xn, wu_ref[...], preferred_element_type=F32)
    act = (gate * jax.nn.sigmoid(gate) * up).astype(BF16)
    o_ref[...] += jnp.dot(act, wd_ref[...], preferred_element_type=F32)


def _ffn(h, g, wg, wu, wd, layer):
    tm, tf = 1024, 512
    return pl.pallas_call(
        _ffn_kernel,
        grid=(SEQ // tm, D_FF // tf),
        in_specs=[pl.BlockSpec((tm, D_MODEL), lambda i, f: (i, 0)),
                  pl.BlockSpec((1, D_MODEL), lambda i, f: (0, 0)),
                  pl.BlockSpec((None, D_MODEL, tf), lambda i, f: (layer, 0, f)),
                  pl.BlockSpec((None, D_MODEL, tf), lambda i, f: (layer, 0, f)),
                  pl.BlockSpec((None, tf, D_MODEL), lambda i, f: (layer, f, 0))],
        out_specs=pl.BlockSpec((tm, D_MODEL), lambda i, f: (i, 0)),
        out_shape=jax.ShapeDtypeStruct((SEQ, D_MODEL), F32),
        scratch_shapes=[pltpu.VMEM((tm, D_MODEL), BF16)],
        compiler_params=_params("parallel", "arbitrary"),
        name="ffn_swiglu",
    )(h, g, wg, wu, wd)


def _ple_kernel(x_ref, g_ref, wg_ref, p_ref, wp_ref, o_ref, *, n_sub):
    sub = x_ref.shape[0] // n_sub
    for c in range(n_sub):
        rows = slice(c * sub, (c + 1) * sub)
        x = x_ref[rows, :]
        xn = (_rms_scale(x, D_MODEL) * g_ref[...]).astype(BF16)
        gate = jnp.dot(xn, wg_ref[...], preferred_element_type=F32)
        proj = jnp.dot(p_ref[rows, :].astype(BF16), wp_ref[...], preferred_element_type=F32)
        o_ref[rows, :] = x + jax.nn.sigmoid(gate) * proj


def _ple(h, g, w_gate, gate_slot, p, w_proj, layer):
    tm = 1024
    resident = pl.Buffered(1)
    return pl.pallas_call(
        functools.partial(_ple_kernel, n_sub=4),
        grid=(SEQ // tm,),
        in_specs=[pl.BlockSpec((tm, D_MODEL), lambda i: (i, 0)),
                  pl.BlockSpec((1, D_MODEL), lambda i: (0, 0)),
                  pl.BlockSpec((None, D_MODEL, D_MODEL), lambda i: (gate_slot, 0, 0),
                               pipeline_mode=resident),
                  pl.BlockSpec((None, tm, PLE_DIM), lambda i: (layer, i, 0)),
                  pl.BlockSpec((None, PLE_DIM, D_MODEL), lambda i: (layer, 0, 0),
                               pipeline_mode=resident)],
        out_specs=pl.BlockSpec((tm, D_MODEL), lambda i: (i, 0)),
        out_shape=jax.ShapeDtypeStruct((SEQ, D_MODEL), F32),
        compiler_params=_params("parallel"),
        name="ple_gate",
    )(h, g, w_gate, p, w_proj)


def _mixer_a(h, g_mix, w_in, slot, gq, gk, sink, w_o, tables):
    nq, nk = A_HEADS * HEAD_DIM, A_KV_HEADS * HEAD_DIM
    scale = LOG2_E / math.sqrt(HEAD_DIM)
    head_gain = jnp.concatenate([jnp.tile(gq * scale, A_HEADS), jnp.tile(gk, A_KV_HEADS)])[None, :]
    qk, v = _qkv_projection(h, g_mix, w_in, slot, nq + nk, nk, head_gain, tables, "a")
    o = _banded_attention(qk, v, sink * LOG2_E, n_q=A_HEADS, n_kv=A_KV_HEADS, hw=A_HALF_WINDOW,
                          name="a_banded_attention")
    return _out_projection(o, w_o, slot, h, "a_out_proj")


def _mixer_b(h, g_mix, w_in, g_qlat, g_kvlat, w_q_up, w_kv_up, slot, gq, gk, w_o, tables, casts):
    scale = LOG2_E / math.sqrt(B_QK)
    half = B_ROPE // 2
    n_lat = B_Q_RANK + B_KV_RANK
    w_in = w_in[slot]
    w_in_pad = jnp.concatenate([w_in[:, :n_lat], _rotary_slab(w_in[:, n_lat:], half)], axis=1)[None]
    tm = 1024
    lat = _norm_matmul(
        h, 0, D_MODEL, g_mix, w_in_pad, 0, 0, 1, tm=tm, tn=B_IN_PAD, extra=(), extra_specs=[],
        epilogue=_plain_epilogue,
        out_shape=jax.ShapeDtypeStruct((SEQ, B_IN_PAD), F32),
        out_specs=pl.BlockSpec((tm, B_IN_PAD), lambda i, j: (i, j)),
        name="b_latent_proj")

    tab = pl.BlockSpec((tm, LANES), lambda i, j: (i, 0))
    tn = 2048
    n_slab = B_HEADS * B_HEAD_PAD

    def head_slab(x):
        return jnp.concatenate([x[..., :B_NOPE], _rotary_slab(x[..., B_NOPE:], half)], axis=-1)

    wq = head_slab(w_q_up[slot].reshape(B_Q_RANK, B_HEADS, B_QK)).reshape(B_Q_RANK, n_slab)[None]
    gq_slab = jnp.tile(head_slab(gq * scale), B_HEADS)[None, :]
    q = _norm_matmul(
        lat, 0, B_Q_RANK, g_qlat, wq, 0, 0, n_slab // tn, tm=tm, tn=tn,
        extra=(gq_slab,) + tuple(tables),
        extra_specs=[pl.BlockSpec((1, tn), lambda i, j: (0, j)), tab, tab],
        epilogue=_b_q_epilogue,
        out_shape=jax.ShapeDtypeStruct((SEQ, n_slab), BF16),
        out_specs=pl.BlockSpec((tm, tn), lambda i, j: (i, j)),
        name="b_q_proj")

    gk_slab = head_slab(gk)[None, :]
    slab_out = pl.BlockSpec((tm, tn), lambda i, j: (i, j))
    k, v_ext = _norm_matmul(
        lat, 1, B_KV_RANK, g_kvlat, w_kv_up, slot, 0, n_slab // tn, tm=tm, tn=tn,
        extra=(gk_slab, lat) + tuple(tables),
        extra_specs=[pl.BlockSpec((1, B_HEAD_PAD), lambda i, j: (0, 0)),
                     pl.BlockSpec((tm, LANES), lambda i, j: (i, n_lat // LANES)),
                     tab, tab],
        epilogue=_b_kv_epilogue,
        out_shape=[jax.ShapeDtypeStruct((SEQ, n_slab), BF16)] * 2,
        out_specs=[slab_out, slab_out],
        name="b_kv_proj")
    o, cast_weights = _dense_attention(q, k, v_ext, casts)
    return _out_projection(o, w_o, slot, h, "b_out_proj"), cast_weights


def _dilated_kernel(q0_ref, q1_ref, q2_ref, kp_ref, kc_ref, kn_ref, vp_ref, vc_ref, vn_ref,
                    band_ref, o_ref, *scratch):
    i = pl.program_id(0)
    blk = q0_ref.shape[0]
    band = band_ref[...]
    win = band.shape[1]
    ones_blk = jnp.ones((win, LANES), BF16)
    col = lax.broadcasted_iota(jnp.int32, (1, win), 1)
    n4, n16 = blk // HOP, blk // (HOP * HOP)
    scratch = list(scratch)
    take = lambda k: [scratch.pop(0) for _ in range(k)]
    o_s, l_s = take(C_GROUPS), take(C_GROUPS)
    hop1 = take(8)
    hop2 = take(7)
    o2_hop1, l2_hop1, merged_tok = take(3)

    for src, dst in zip((q1_ref, q2_ref, kp_ref, kc_ref, kn_ref, vp_ref, vc_ref, vn_ref), hop1):
        for r in range(HOP):
            dst[r * n4:(r + 1) * n4, :] = src[pl.ds(r, n4, stride=HOP), :]
    for src, dst in zip(hop1[1:], hop2):
        for r in range(HOP * HOP):
            dst[r * n16:(r + 1) * n16, :] = src[pl.ds((r % HOP) * n4 + r // HOP, n16, stride=HOP), :]
    operands = ((q0_ref, kp_ref, kc_ref, kn_ref, vp_ref, vc_ref, vn_ref),
                (hop1[0],) + tuple(hop1[2:]),
                tuple(hop2))

    for g, ((window, d), refs) in enumerate(zip(C_PATTERNS, operands)):
        q_ref, kprev, kcur, knext, vprev, vcur, vnext = refs
        n = blk // d
        rq = min(n, win - 2 * C_HALF)
        pad_rows = win - min(n + 2 * C_HALF, win)
        tiles = []
        for r in range(d):
            def chain_window(prev_ref, cur_ref, next_ref):
                parts = [prev_ref[(r + 1) * n - C_HALF:(r + 1) * n, :], cur_ref[r * n:(r + 1) * n, :],
                         next_ref[r * n:r * n + C_HALF, :]]
                if pad_rows:
                    parts.append(jnp.zeros((pad_rows, LANES), F32))
                return jnp.concatenate(parts, axis=0).astype(BF16)

            k_chain = chain_window(kprev, kcur, knext)
            v_chain = chain_window(vprev, vcur, vnext)
            for sb in range(n // rq):
                kpos = i * n + sb * rq - C_HALF + col
                edge = jnp.where((kpos >= 0) & (kpos < SEQ // d), 0.0, NEG)
                dst = slice(r * n + sb * rq, r * n + (sb + 1) * rq)
                tiles.append((dst, q_ref[dst, :].astype(BF16), k_chain[sb * rq:sb * rq + win],
                              v_chain[sb * rq:sb * rq + win], band[:rq] + edge))
        full = win - 2 * C_HALF
        batch = 1 if rq == full else 4 * full // rq
        for b0 in range(0, len(tiles), batch):
            group_tiles = tiles[b0:b0 + batch]
            scores = [lax.dot_general(q, k_win, (((1,), (1,)), ((), ())),
                                      preferred_element_type=F32) + bias
                      for _, q, k_win, _, bias in group_tiles]
            probs = []
            for s in scores:
                mx = jnp.maximum(s[:, :LANES], s[:, LANES:])
                m = jnp.broadcast_to(jnp.max(mx, axis=-1, keepdims=True), mx.shape)
                probs.append((m, jnp.exp2(s - jnp.concatenate([m, m], axis=1)).astype(BF16)))
            for (dst, _, _, v_win, _), (m, p) in zip(group_tiles, probs):
                o_ext = jnp.dot(p, jnp.concatenate([v_win, ones_blk], axis=1),
                                preferred_element_type=F32)
                denom = o_ext[:, HEAD_DIM:]
                o_s[g][dst, :] = o_ext[:, :HEAD_DIM] / denom
                l_s[g][dst, :] = m + jnp.log2(denom)

    for src, dst in ((o_s[2], o2_hop1), (l_s[2], l2_hop1)):
        for r in range(HOP * HOP):
            dst[pl.ds((r % HOP) * n4 + r // HOP, n16, stride=HOP), :] = src[r * n16:(r + 1) * n16, :]
    for r in range(HOP):
        rows = slice(r * n4, (r + 1) * n4)
        tok = pl.ds(r, n4, stride=HOP)
        l0, l1, l2 = l_s[0][tok, :], l_s[1][rows, :], l2_hop1[rows, :]
        m = jnp.maximum(jnp.maximum(l0, l1), l2)
        e0, e1, e2 = jnp.exp2(l0 - m), jnp.exp2(l1 - m), jnp.exp2(l2 - m)
        merged = (e0 * o_s[0][tok, :] + e1 * o_s[1][rows, :] + e2 * o2_hop1[rows, :]) / (e0 + e1 + e2)
        merged_tok[tok, :] = merged
    o_ref[...] = merged_tok[...].astype(o_ref.dtype)


def _dilated_attention(qk, v):
    blk = 1024
    nb = SEQ // blk
    win = 2 * LANES
    rows = jnp.arange(LANES, dtype=jnp.int32)[:, None]
    cols = jnp.arange(win, dtype=jnp.int32)[None, :]
    band = jnp.where(jnp.abs(rows + C_HALF - cols) <= C_HALF, 0.0, NEG).astype(F32)

    def slab(col0, shift):
        return pl.BlockSpec((blk, HEAD_DIM),
                            lambda i, h: (jnp.clip(i + shift, 0, nb - 1), col0 + h))

    k_col = C_GROUPS * C_HEADS
    in_specs = [slab(g * C_HEADS, 0) for g in range(C_GROUPS)]
    in_specs += [slab(k_col, -1), slab(k_col, 0), slab(k_col, 1)]
    in_specs += [slab(0, -1), slab(0, 0), slab(0, 1)]
    in_specs.append(pl.BlockSpec((LANES, win), lambda i, h: (0, 0)))
    return pl.pallas_call(
        _dilated_kernel,
        grid=(nb, C_HEADS),
        in_specs=in_specs,
        out_specs=pl.BlockSpec((blk, HEAD_DIM), lambda i, h: (i, h)),
        out_shape=jax.ShapeDtypeStruct((SEQ, C_HEADS * HEAD_DIM), BF16),
        scratch_shapes=[pltpu.VMEM((blk, LANES), F32)] * (2 * C_GROUPS + 8 + 7 + 3),
        compiler_params=_params("parallel", "parallel"),
        name="c_dilated_attention",
    )(qk, qk, qk, qk, qk, qk, v, v, v, band)


def _mixer_c(h, g_mix, w_in, slot, gq, gk, w_o, tables):
    n_qh = C_GROUPS * C_HEADS
    nq = n_qh * HEAD_DIM
    nkv = C_HEADS * HEAD_DIM
    scale = LOG2_E / math.sqrt(HEAD_DIM)
    head_gain = jnp.concatenate([jnp.tile(gq * scale, n_qh), jnp.tile(gk, C_HEADS)])[None, :]
    qk, v = _qkv_projection(h, g_mix, w_in, slot, nq + nkv, nkv, head_gain, tables, "c", dtype=F32)
    o = _dilated_attention(qk, v)
    return _out_projection(o, w_o, slot, h, "c_out_proj")


def kernel(x, p, positions, g_mix, g_ffn, g_ple, w_ple_gate, w_ple_proj,
           w_ffn_gate, w_ffn_up, w_ffn_down,
           a_w_in, a_q_norm, a_k_norm, a_sink, a_w_o,
           b_w_in, b_q_lat_norm, b_kv_lat_norm, b_w_q_up, b_w_kv_up, b_q_norm, b_k_norm, b_w_o,
           c_w_in, c_q_norm, c_k_norm, c_w_o):
    h = x.reshape(SEQ, D_MODEL)
    p = p.reshape(DEPTH, SEQ, PLE_DIM)
    pos_col = positions.reshape(SEQ, 1)
    tables_ac = _tables_adjacent(pos_col, PARTIAL_ROT)
    tables_b = _tables_split(pos_col, B_ROPE)
    (w_ple_proj, b_w_in, b_w_q_up, b_w_kv_up, b_w_o) = [
        w.astype(BF16) for w in (w_ple_proj, b_w_in, b_w_q_up, b_w_kv_up, b_w_o)]
    late = {"ffn_gate": (w_ffn_gate, 1, DEPTH - 1, 64), "ffn_up": (w_ffn_up, 1, DEPTH - 1, 64),
            "ffn_down": (w_ffn_down, 1, DEPTH - 1, 176), "ple_gate": (w_ple_gate, 1, DEPTH - 1, 64),
            "a_in": (a_w_in, 1, 1, 16), "a_o": (a_w_o, 1, 1, 16),
            "c_in": (c_w_in, 0, 1, 16), "c_o": (c_w_o, 0, 1, 16)}
    early = {name: _cast_layers(w, first, w.shape[1] // 8)
             for name, (w, first, _, _) in late.items() if first > 0}
    cast = {}

    def weight(name, layer):
        first = late[name][1]
        if layer < first:
            return early[name], layer
        return cast[name], layer - first

    for i in range(DEPTH):
        kind, slot = i % 3, i // 3
        gm = g_mix[i][None, :]
        if kind == 0:
            (w_in, l), (w_o, _) = weight("a_in", slot), weight("a_o", slot)
            h = _mixer_a(h, gm, w_in, l, a_q_norm[slot], a_k_norm[slot], a_sink[slot], w_o,
                         tables_ac)
        elif kind == 1:
            h, cast_weights = _mixer_b(
                h, gm, b_w_in, b_q_lat_norm[slot][None, :], b_kv_lat_norm[slot][None, :],
                b_w_q_up, b_w_kv_up, slot, b_q_norm[slot], b_k_norm[slot], b_w_o, tables_b,
                list(late.values()))
            cast = dict(zip(late, cast_weights))
        else:
            h = _mixer_c(h, gm, *weight("c_in", slot), c_q_norm[slot], c_k_norm[slot],
                         weight("c_o", slot)[0], tables_ac)
        (wg, l), (wu, _), (wd, _) = weight("ffn_gate", i), weight("ffn_up", i), weight("ffn_down", i)
        h = _ffn(h, g_ffn[i][None, :], wg, wu, wd, l)
        wpg, l = weight("ple_gate", i)
        h = _ple(h, g_ple[i][None, :], wpg, l, p, w_ple_proj, i)
    return h.reshape(1, SEQ, D_MODEL)
```

```python
import functools
import math

import jax
import jax.numpy as jnp
from jax import lax
from jax.experimental import pallas as pl
from jax.experimental.pallas import tpu as pltpu

F32 = jnp.float32
BF16 = jnp.bfloat16

SEQ = 8192
D_MODEL = 2048
DEPTH = 4
HEAD_DIM = 128
ROPE_THETA = 500000.0
PARTIAL_ROT = HEAD_DIM // 4
NORM_EPS = 1e-6
NEG = -1e30
LOG2_E = math.log2(math.e)
LANES = 128
HALF_LANES = LANES // 2

A_HEADS = 16
A_KV_HEADS = 4
A_HALF_WINDOW = 128
A_BLOCK = 512
A_TILE = 128
B_HEADS = 16
B_Q_RANK = 512
B_KV_RANK = 512
B_NOPE = 128
B_ROPE = 64
B_QK = B_NOPE + B_ROPE
B_HEAD_PAD = 256
B_IN_PAD = B_Q_RANK + B_KV_RANK + LANES
C_PATTERNS = ((128, 1), (512, 4), (2048, 16))
C_GROUPS = 3
C_HEADS = 16
C_HALF = 64
assert all(window // 2 // dil == C_HALF for window, dil in C_PATTERNS)
HOP = 4
assert tuple(dil for _, dil in C_PATTERNS) == (1, HOP, HOP * HOP)
D_FF = 5632
PLE_DIM = 256

VMEM_LIMIT = 56 * 1024 * 1024


def _params(*sem):
    return pltpu.CompilerParams(dimension_semantics=sem, vmem_limit_bytes=VMEM_LIMIT)


def _rms_scale(x, width):
    ss = jnp.sum(x * x, axis=-1, keepdims=True)
    return x * lax.rsqrt(ss * (1.0 / width) + NORM_EPS)


def _rope_adjacent(y, cos, sin_lo, sin_hi, half):
    return (y * cos + pltpu.roll(y, LANES - half, 1) * sin_lo
            + pltpu.roll(y, half, 1) * sin_hi)


def _rope_split(y, cos, sin):
    return y * cos + pltpu.roll(y, HALF_LANES, 1) * sin


def _rotary_slab(x, half):
    pad = jnp.zeros(x.shape[:-1] + (HALF_LANES - half,), x.dtype)
    return jnp.concatenate([x[..., :half], pad, x[..., half:], pad], axis=-1)


def _cast_kernel(w_ref, o_ref):
    o_ref[...] = w_ref[...].astype(o_ref.dtype)


def _cast_layers(w, n_layers, rows):
    _, kdim, ndim = w.shape
    assert kdim % rows == 0
    spec = pl.BlockSpec((None, rows, ndim), lambda l, r: (l, r, 0))
    return pl.pallas_call(
        _cast_kernel,
        grid=(n_layers, kdim // rows),
        in_specs=[spec],
        out_specs=spec,
        out_shape=jax.ShapeDtypeStruct((n_layers, kdim, ndim), BF16),
        compiler_params=_params("parallel", "parallel"),
        name="cast_early_weights",
    )(w)


def _rope_table_kernel(pos_ref, inv_ref, sign_ref, cos_ref, *sin_refs):
    ang = pos_ref[...].astype(F32) * inv_ref[...]
    cos_ref[...] = jnp.cos(ang)
    s = jnp.sin(ang)
    for r, sin_ref in enumerate(sin_refs):
        sin_ref[...] = s * sign_ref[r:r + 1, :]


def _rope_tables(pos_col, inv_lanes, signs):
    tm = 1024
    n = signs.shape[0]
    tab = pl.BlockSpec((tm, LANES), lambda i: (i, 0))
    return pl.pallas_call(
        _rope_table_kernel,
        grid=(SEQ // tm,),
        in_specs=[pl.BlockSpec((tm, 1), lambda i: (i, 0)),
                  pl.BlockSpec((1, LANES), lambda i: (0, 0)),
                  pl.BlockSpec((n, LANES), lambda i: (0, 0))],
        out_specs=[tab] * (n + 1),
        out_shape=[jax.ShapeDtypeStruct((SEQ, LANES), F32)] * (n + 1),
        compiler_params=_params("parallel"),
        name="rope_tables",
    )(pos_col, inv_lanes, signs)


def _inv_freq(rot_dim):
    half = rot_dim // 2
    return ROPE_THETA ** (-jnp.arange(half, dtype=F32) * 2.0 / rot_dim)


def _tables_adjacent(pos_col, rot_dim):
    half = rot_dim // 2
    inv = _inv_freq(rot_dim)
    rest = jnp.zeros((LANES - rot_dim,), F32)
    zero, one = jnp.zeros((half,), F32), jnp.ones((half,), F32)
    inv_l = jnp.concatenate([inv, inv, rest])[None, :]
    signs = jnp.stack([jnp.concatenate([-one, zero, rest]), jnp.concatenate([zero, one, rest])])
    return _rope_tables(pos_col, inv_l, signs)


def _tables_split(pos_col, rot_dim):
    half = rot_dim // 2
    inv = _inv_freq(rot_dim)
    one = jnp.ones((half,), F32)
    inv_l = _rotary_slab(jnp.concatenate([inv, inv]), half)[None, :]
    signs = _rotary_slab(jnp.concatenate([-one, one]), half)[None, :]
    return _rope_tables(pos_col, inv_l, signs)


def _norm_matmul_kernel(x_ref, g_ref, w_ref, *rest, n_extra, epilogue, n_sub):
    extra = rest[:n_extra]
    outs = rest[n_extra:-1]
    xn_ref = rest[-1]

    sub = xn_ref.shape[0] // n_sub

    def column_tile(first):
        for c in range(n_sub):
            rows = slice(c * sub, (c + 1) * sub)
            if first:
                x = x_ref[rows, :]
                xn_ref[rows, :] = (_rms_scale(x, x.shape[-1]) * g_ref[...]).astype(BF16)
            acc = jnp.dot(xn_ref[rows, :], w_ref[...], preferred_element_type=F32)
            epilogue(acc, rows, extra, outs)

    @pl.when(pl.program_id(1) == 0)
    def _():
        column_tile(True)

    @pl.when(pl.program_id(1) > 0)
    def _():
        column_tile(False)


def _norm_matmul(x, x_col, kin, gain, w, w_slot, w_col0, n_tiles, *, tm, tn, extra, extra_specs,
                 epilogue, out_shape, out_specs, name, n_sub=4):
    m = x.shape[0]
    kernel = functools.partial(_norm_matmul_kernel, n_extra=len(extra), epilogue=epilogue,
                               n_sub=n_sub)
    return pl.pallas_call(
        kernel,
        grid=(m // tm, n_tiles),
        in_specs=[pl.BlockSpec((tm, kin), lambda i, j: (i, x_col)),
                  pl.BlockSpec((1, kin), lambda i, j: (0, 0)),
                  pl.BlockSpec((None, kin, tn), lambda i, j: (w_slot, 0, w_col0 + j))]
        + list(extra_specs),
        out_specs=out_specs,
        out_shape=out_shape,
        scratch_shapes=[pltpu.VMEM((tm, kin), BF16)],
        compiler_params=_params("parallel", "arbitrary"),
        name=name,
    )(x, gain, w, *extra)


def _plain_epilogue(acc, rows, extra, outs):
    outs[0][rows, :] = acc.astype(outs[0].dtype)


def _head_norm_rope_epilogue(acc, rows, extra, outs):
    gain_ref, cos_ref, slo_ref, shi_ref = extra
    (o_ref,) = outs
    cos, slo, shi = cos_ref[rows, :], slo_ref[rows, :], shi_ref[rows, :]
    for c in range(acc.shape[1] // HEAD_DIM):
        cols = slice(c * HEAD_DIM, (c + 1) * HEAD_DIM)
        y = _rms_scale(acc[:, cols], HEAD_DIM) * gain_ref[:, cols]
        o_ref[rows, cols] = _rope_adjacent(y, cos, slo, shi, PARTIAL_ROT // 2).astype(o_ref.dtype)


def _qkv_projection(h, g_mix, w, slot, n_qk, n_v, head_gain, tables, name, dtype=BF16):
    tm = 1024
    tn = next(t for t in (1280, 1024, 512) if n_qk % t == 0)
    tn_v = next(t for t in (1024, 512) if n_v % t == 0 and n_qk % t == 0)
    tab = pl.BlockSpec((tm, LANES), lambda i, j: (i, 0))
    qk = _norm_matmul(
        h, 0, D_MODEL, g_mix, w, slot, 0, n_qk // tn, tm=tm, tn=tn,
        extra=(head_gain,) + tuple(tables),
        extra_specs=[pl.BlockSpec((1, tn), lambda i, j: (0, j)), tab, tab, tab],
        epilogue=_head_norm_rope_epilogue,
        out_shape=jax.ShapeDtypeStruct((SEQ, n_qk), dtype),
        out_specs=pl.BlockSpec((tm, tn), lambda i, j: (i, j)),
        name=name + "_qk_proj")
    v = _norm_matmul(
        h, 0, D_MODEL, g_mix, w, slot, n_qk // tn_v, n_v // tn_v, tm=tm, tn=tn_v,
        extra=(), extra_specs=[], epilogue=_plain_epilogue,
        out_shape=jax.ShapeDtypeStruct((SEQ, n_v), dtype),
        out_specs=pl.BlockSpec((tm, tn_v), lambda i, j: (i, j)),
        name=name + "_v_proj")
    return qk, v


def _b_q_epilogue(acc, rows, extra, outs):
    gain_ref, cos_ref, sin_ref = extra
    (o_ref,) = outs
    cos, sin = cos_ref[rows, :], sin_ref[rows, :]
    for c in range(acc.shape[1] // B_HEAD_PAD):
        c0 = c * B_HEAD_PAD
        y = _rms_scale(acc[:, c0:c0 + B_HEAD_PAD], B_QK) * gain_ref[:, c0:c0 + B_HEAD_PAD]
        o_ref[rows, c0:c0 + B_NOPE] = y[:, :B_NOPE].astype(BF16)
        o_ref[rows, c0 + B_NOPE:c0 + B_HEAD_PAD] = _rope_split(y[:, B_NOPE:], cos, sin).astype(BF16)


def _b_kv_epilogue(acc, rows, extra, outs):
    gain_ref, krope_ref, cos_ref, sin_ref = extra
    k_ref, v_ref = outs
    cos, sin = cos_ref[rows, :], sin_ref[rows, :]
    kr = krope_ref[rows, :]
    kr_ss = jnp.sum(kr * kr, axis=-1, keepdims=True)
    g_nope = gain_ref[:, :B_NOPE]
    kr_rot = _rope_split(kr * gain_ref[:, B_NOPE:], cos, sin)
    ones_blk = jnp.ones((acc.shape[0], LANES), BF16)
    for c in range(acc.shape[1] // B_HEAD_PAD):
        c0 = c * B_HEAD_PAD
        y = acc[:, c0:c0 + B_NOPE]
        ss = jnp.sum(y * y, axis=-1, keepdims=True) + kr_ss
        rinv = lax.rsqrt(ss * (1.0 / B_QK) + NORM_EPS)
        k_ref[rows, c0:c0 + B_NOPE] = (y * rinv * g_nope).astype(BF16)
        k_ref[rows, c0 + B_NOPE:c0 + B_HEAD_PAD] = (kr_rot * rinv).astype(BF16)
        v_ref[rows, c0:c0 + B_NOPE] = acc[:, c0 + B_NOPE:c0 + B_HEAD_PAD].astype(BF16)
        v_ref[rows, c0 + B_NOPE:c0 + B_HEAD_PAD] = ones_blk


def _banded_kernel(q_ref, kp_ref, kc_ref, kn_ref, vp_ref, vc_ref, vn_ref, band_ref, sink_ref,
                   o_ref, *, hw, sq, n_kv, group):
    i = pl.program_id(0)
    blk = q_ref.shape[0]
    win = sq + 2 * hw
    n_col = win // LANES
    ones_blk = jnp.ones((win, LANES), BF16)
    col = lax.broadcasted_iota(jnp.int32, (1, win), 1)

    def halo_concat(prev_ref, cur_ref, next_ref, kv):
        cols = slice(kv * HEAD_DIM, (kv + 1) * HEAD_DIM)
        return jnp.concatenate(
            [prev_ref[blk - hw:, cols], cur_ref[:, cols], next_ref[:hw, cols]], axis=0)

    k_cat = [halo_concat(kp_ref, kc_ref, kn_ref, kv) for kv in range(n_kv)]
    v_cat = [halo_concat(vp_ref, vc_ref, vn_ref, kv) for kv in range(n_kv)]
    for sb in range(blk // sq):
        rows = slice(sb * sq, (sb + 1) * sq)
        kpos = i * blk + sb * sq - hw + col
        edge = jnp.where((kpos >= 0) & (kpos < SEQ), 0.0, NEG)
        bias = jnp.concatenate([band_ref[...] + edge] * group, axis=0)
        scores = []
        for kv in range(n_kv):
            q = jnp.concatenate(
                [q_ref[rows, hd * HEAD_DIM:(hd + 1) * HEAD_DIM]
                 for hd in range(kv * group, (kv + 1) * group)], axis=0)
            scores.append(lax.dot_general(q, k_cat[kv][sb * sq:sb * sq + win],
                                          (((1,), (1,)), ((), ())),
                                          preferred_element_type=F32) + bias)
        probs = []
        for kv, s in enumerate(scores):
            mx = s[:, :LANES]
            for cb in range(1, n_col):
                mx = jnp.maximum(mx, s[:, cb * LANES:(cb + 1) * LANES])
            sink = sink_ref[kv * group * sq:(kv + 1) * group * sq, :]
            m = jnp.maximum(jnp.broadcast_to(jnp.max(mx, axis=-1, keepdims=True), mx.shape), sink)
            probs.append((m, jnp.exp2(s - jnp.concatenate([m] * n_col, axis=1)).astype(BF16)))
        for kv, (m, p) in enumerate(probs):
            v_ext = jnp.concatenate([v_cat[kv][sb * sq:sb * sq + win], ones_blk], axis=1)
            o_ext = jnp.dot(p, v_ext, preferred_element_type=F32)
            sink = sink_ref[kv * group * sq:(kv + 1) * group * sq, :]
            o = o_ext[:, :HEAD_DIM] / (o_ext[:, HEAD_DIM:] + jnp.exp2(sink - m))
            for c in range(group):
                hd = kv * group + c
                o_ref[rows, hd * HEAD_DIM:(hd + 1) * HEAD_DIM] = (
                    o[c * sq:(c + 1) * sq].astype(o_ref.dtype))


def _banded_attention(qk, v, sink, *, n_q, n_kv, hw, name):
    blk, sq = A_BLOCK, A_TILE
    nb = SEQ // blk
    group = n_q // n_kv
    qw, kw = n_q * HEAD_DIM, n_kv * HEAD_DIM
    win = sq + 2 * hw
    rows = jnp.arange(sq, dtype=jnp.int32)[:, None]
    cols = jnp.arange(win, dtype=jnp.int32)[None, :]
    band = jnp.where(jnp.abs(rows + hw - cols) <= hw, 0.0, NEG).astype(F32)
    sink_rep = jnp.broadcast_to(jnp.repeat(sink, sq)[:, None], (n_q * sq, LANES))

    def kv_spec(col, shift):
        return pl.BlockSpec((blk, kw), lambda i: (jnp.clip(i + shift, 0, nb - 1), col))

    k_col = qw // kw
    return pl.pallas_call(
        functools.partial(_banded_kernel, hw=hw, sq=sq, n_kv=n_kv, group=group),
        grid=(nb,),
        in_specs=[pl.BlockSpec((blk, qw), lambda i: (i, 0)),
                  kv_spec(k_col, -1), kv_spec(k_col, 0), kv_spec(k_col, 1),
                  kv_spec(0, -1), kv_spec(0, 0), kv_spec(0, 1),
                  pl.BlockSpec((sq, win), lambda i: (0, 0)),
                  pl.BlockSpec((n_q * sq, LANES), lambda i: (0, 0))],
        out_specs=pl.BlockSpec((blk, qw), lambda i: (i, 0)),
        out_shape=jax.ShapeDtypeStruct((SEQ, qw), BF16),
        compiler_params=_params("parallel"),
        name=name,
    )(qk, qk, qk, qk, v, v, v, band, sink_rep)


def _flash_kernel(*refs, tk, n_cast):
    q_ref, k_ref, v_ref = refs[:3]
    w_refs = refs[3:3 + n_cast]
    o_ref = refs[3 + n_cast]
    wb_refs = refs[4 + n_cast:4 + 2 * n_cast]
    m_ref, acc_ref = refs[4 + 2 * n_cast:]
    tq = q_ref.shape[0]
    n_chunks = k_ref.shape[0] // tk
    n_col = tk // LANES
    m_ref[...] = jnp.full((tq, LANES), NEG, F32)
    acc_ref[...] = jnp.zeros((tq, B_HEAD_PAD), F32)
    q = q_ref[...]
    for w_ref, wb_ref in zip(w_refs, wb_refs):
        wb_ref[...] = w_ref[...].astype(BF16)

    for c in range(n_chunks):
        k = k_ref[c * tk:(c + 1) * tk, :]
        v = v_ref[c * tk:(c + 1) * tk, :]
        s = lax.dot_general(q, k, (((1,), (1,)), ((), ())), preferred_element_type=F32)
        mx = s[:, :LANES]
        for cb in range(1, n_col):
            mx = jnp.maximum(mx, s[:, cb * LANES:(cb + 1) * LANES])
        m_old = m_ref[...]
        m_new = jnp.maximum(m_old, jnp.max(mx, axis=-1, keepdims=True))
        alpha = jnp.exp2(m_old - m_new)
        p = jnp.exp2(s - jnp.concatenate([m_new] * n_col, axis=1)).astype(BF16)
        pv = jnp.dot(p, v, preferred_element_type=F32)
        acc_ref[...] = acc_ref[...] * jnp.concatenate([alpha, alpha], axis=1) + pv
        m_ref[...] = m_new

    acc = acc_ref[...]
    o_ref[...] = (acc[:, :B_NOPE] / acc[:, B_NOPE:]).astype(o_ref.dtype)


def _dense_attention(q, k, v_ext, casts):
    tq, tk = 1024, 256
    nq = SEQ // tq
    kv_spec = pl.BlockSpec((SEQ, B_HEAD_PAD), lambda h, i: (0, h))
    w_specs, wb_specs, wb_shapes = [], [], []
    for w, first, n_layers, rows in casts:
        _, kdim, ndim = w.shape
        per_layer = kdim // rows
        last = n_layers * per_layer - 1
        assert last < B_HEADS * nq and kdim % rows == 0

        def slab(h, i, per_layer=per_layer, last=last):
            s = jnp.minimum(h * nq + i, last)
            return s // per_layer, s % per_layer

        w_specs.append(pl.BlockSpec(
            (None, rows, ndim),
            lambda h, i, slab=slab, first=first: (first + slab(h, i)[0], slab(h, i)[1], 0)))
        wb_specs.append(pl.BlockSpec(
            (None, rows, ndim), lambda h, i, slab=slab: (slab(h, i)[0], slab(h, i)[1], 0)))
        wb_shapes.append(jax.ShapeDtypeStruct((n_layers, kdim, ndim), BF16))
    outs = pl.pallas_call(
        functools.partial(_flash_kernel, tk=tk, n_cast=len(casts)),
        grid=(B_HEADS, nq),
        in_specs=[pl.BlockSpec((tq, B_HEAD_PAD), lambda h, i: (i, h)), kv_spec, kv_spec] + w_specs,
        out_specs=[pl.BlockSpec((tq, B_NOPE), lambda h, i: (i, h))] + wb_specs,
        out_shape=[jax.ShapeDtypeStruct((SEQ, B_HEADS * B_NOPE), BF16)] + wb_shapes,
        scratch_shapes=[pltpu.VMEM((tq, LANES), F32), pltpu.VMEM((tq, B_HEAD_PAD), F32)],
        compiler_params=_params("arbitrary", "arbitrary"),
        name="b_flash_attention",
    )(q, k, v_ext, *[c[0] for c in casts])
    return outs[0], outs[1:]


def _out_proj_kernel(o_ref, w_ref, h_ref, out_ref, *, n_sub):
    sub = o_ref.shape[0] // n_sub
    for c in range(n_sub):
        rows = slice(c * sub, (c + 1) * sub)
        out_ref[rows, :] = h_ref[rows, :] + jnp.dot(o_ref[rows, :], w_ref[...],
                                                    preferred_element_type=F32)


def _out_projection(o, w, slot, h, name):
    tm = 1024
    row = pl.BlockSpec((tm, D_MODEL), lambda i: (i, 0))
    return pl.pallas_call(
        functools.partial(_out_proj_kernel, n_sub=4),
        grid=(SEQ // tm,),
        in_specs=[row,
                  pl.BlockSpec((None, D_MODEL, D_MODEL), lambda i: (slot, 0, 0),
                               pipeline_mode=pl.Buffered(1)),
                  row],
        out_specs=row,
        out_shape=jax.ShapeDtypeStruct((SEQ, D_MODEL), F32),
        compiler_params=_params("parallel"),
        name=name,
    )(o, w, h)


def _ffn_kernel(x_ref, g_ref, wg_hbm, wu_hbm, wd_hbm, o_ref, xn_ref, wg_buf, wu_buf, wd_buf, sem,
                *, layer, tf):
    i = pl.program_id(0)
    n_chunks = D_FF // tf

    def copies(f):
        slot = f % 2
        cols = pl.ds(f * tf, tf)
        return (pltpu.make_async_copy(wg_hbm.at[layer, :, cols], wg_buf.at[slot], sem.at[0, slot]),
                pltpu.make_async_copy(wu_hbm.at[layer, :, cols], wu_buf.at[slot], sem.at[1, slot]),
                pltpu.make_async_copy(wd_hbm.at[layer, cols, :], wd_buf.at[slot], sem.at[2, slot]))

    def start(f):
        for c in copies(f):
            c.start()

    def wait(f):
        for c in copies(f):
            c.wait()

    def row_block(order):
        @pl.when(i == 0)
        def _():
            start(order[0])
            start(order[1])

        x = x_ref[...]
        xn_ref[...] = (_rms_scale(x, D_MODEL) * g_ref[...]).astype(BF16)
        o_ref[...] = x
        for pos, f in enumerate(order):
            if pos < 2:
                @pl.when(i == 0)
                def _():
                    wait(f)
            else:
                wait(f)
            if 2 <= pos + 1 < n_chunks:
                start(order[pos + 1])
            slot = f % 2
            xn = xn_ref[...]
            gate = jnp.dot(xn, wg_buf[slot], preferred_element_type=F32)
            up = jnp.dot(xn, wu_buf[slot], preferred_element_type=F32)
            act = (gate * jax.nn.sigmoid(gate) * up).astype(BF16)
            o_ref[...] += jnp.dot(act, wd_buf[slot], preferred_element_type=F32)

    @pl.when(i % 2 == 0)
    def _():
        row_block(list(range(n_chunks)))

    @pl.when(i % 2 == 1)
    def _():
        row_block(list(range(n_chunks - 1, -1, -1)))


def _ffn(h, g, wg, wu, wd, layer):
    tm, tf = 1024, 512
    hbm = pl.BlockSpec(memory_space=pl.ANY)
    return pl.pallas_call(
        functools.partial(_ffn_kernel, layer=layer, tf=tf),
        grid=(SEQ // tm,),
        in_specs=[pl.BlockSpec((tm, D_MODEL), lambda i: (i, 0)),
                  pl.BlockSpec((1, D_MODEL), lambda i: (0, 0)),
                  hbm, hbm, hbm],
        out_specs=pl.BlockSpec((tm, D_MODEL), lambda i: (i, 0)),
        out_shape=jax.ShapeDtypeStruct((SEQ, D_MODEL), F32),
        scratch_shapes=[pltpu.VMEM((tm, D_MODEL), BF16),
                        pltpu.VMEM((2, D_MODEL, tf), BF16),
                        pltpu.VMEM((2, D_MODEL, tf), BF16),
                        pltpu.VMEM((2, tf, D_MODEL), BF16),
                        pltpu.SemaphoreType.DMA((3, 2))],
        compiler_params=_params("arbitrary"),
        name="ffn_swiglu",
    )(h, g, wg, wu, wd)


def _ple_kernel(x_ref, g_ref, wg_ref, p_ref, wp_ref, o_ref, *, n_sub):
    sub = x_ref.shape[0] // n_sub
    for c in range(n_sub):
        rows = slice(c * sub, (c + 1) * sub)
        x = x_ref[rows, :]
        xn = (_rms_scale(x, D_MODEL) * g_ref[...]).astype(BF16)
        gate = jnp.dot(xn, wg_ref[...], preferred_element_type=F32)
        proj = jnp.dot(p_ref[rows, :].astype(BF16), wp_ref[...], preferred_element_type=F32)
        o_ref[rows, :] = x + jax.nn.sigmoid(gate) * proj


def _ple(h, g, w_gate, gate_slot, p, w_proj, layer):
    tm = 1024
    resident = pl.Buffered(1)
    return pl.pallas_call(
        functools.partial(_ple_kernel, n_sub=4),
        grid=(SEQ // tm,),
        in_specs=[pl.BlockSpec((tm, D_MODEL), lambda i: (i, 0)),
                  pl.BlockSpec((1, D_MODEL), lambda i: (0, 0)),
                  pl.BlockSpec((None, D_MODEL, D_MODEL), lambda i: (gate_slot, 0, 0),
                               pipeline_mode=resident),
                  pl.BlockSpec((None, tm, PLE_DIM), lambda i: (layer, i, 0)),
                  pl.BlockSpec((None, PLE_DIM, D_MODEL), lambda i: (layer, 0, 0),
                               pipeline_mode=resident)],
        out_specs=pl.BlockSpec((tm, D_MODEL), lambda i: (i, 0)),
        out_shape=jax.ShapeDtypeStruct((SEQ, D_MODEL), F32),
        compiler_params=_params("parallel"),
        name="ple_gate",
    )(h, g, w_gate, p, w_proj)


def _mixer_a(h, g_mix, w_in, slot, gq, gk, sink, w_o, tables):
    nq, nk = A_HEADS * HEAD_DIM, A_KV_HEADS * HEAD_DIM
    scale = LOG2_E / math.sqrt(HEAD_DIM)
    head_gain = jnp.concatenate([jnp.tile(gq * scale, A_HEADS), jnp.tile(gk, A_KV_HEADS)])[None, :]
    qk, v = _qkv_projection(h, g_mix, w_in, slot, nq + nk, nk, head_gain, tables, "a")
    o = _banded_attention(qk, v, sink * LOG2_E, n_q=A_HEADS, n_kv=A_KV_HEADS, hw=A_HALF_WINDOW,
                          name="a_banded_attention")
    return _out_projection(o, w_o, slot, h, "a_out_proj")


def _mixer_b(h, g_mix, w_in, g_qlat, g_kvlat, w_q_up, w_kv_up, slot, gq, gk, w_o, tables, casts):
    scale = LOG2_E / math.sqrt(B_QK)
    half = B_ROPE // 2
    n_lat = B_Q_RANK + B_KV_RANK
    w_in = w_in[slot]
    w_in_pad = jnp.concatenate([w_in[:, :n_lat], _rotary_slab(w_in[:, n_lat:], half)], axis=1)[None]
    tm = 1024
    lat = _norm_matmul(
        h, 0, D_MODEL, g_mix, w_in_pad, 0, 0, 1, tm=tm, tn=B_IN_PAD, extra=(), extra_specs=[],
        epilogue=_plain_epilogue,
        out_shape=jax.ShapeDtypeStruct((SEQ, B_IN_PAD), F32),
        out_specs=pl.BlockSpec((tm, B_IN_PAD), lambda i, j: (i, j)),
        name="b_latent_proj")

    tab = pl.BlockSpec((tm, LANES), lambda i, j: (i, 0))
    tn = 2048
    n_slab = B_HEADS * B_HEAD_PAD

    def head_slab(x):
        return jnp.concatenate([x[..., :B_NOPE], _rotary_slab(x[..., B_NOPE:], half)], axis=-1)

    wq = head_slab(w_q_up[slot].reshape(B_Q_RANK, B_HEADS, B_QK)).reshape(B_Q_RANK, n_slab)[None]
    gq_slab = jnp.tile(head_slab(gq * scale), B_HEADS)[None, :]
    q = _norm_matmul(
        lat, 0, B_Q_RANK, g_qlat, wq, 0, 0, n_slab // tn, tm=tm, tn=tn,
        extra=(gq_slab,) + tuple(tables),
        extra_specs=[pl.BlockSpec((1, tn), lambda i, j: (0, j)), tab, tab],
        epilogue=_b_q_epilogue,
        out_shape=jax.ShapeDtypeStruct((SEQ, n_slab), BF16),
        out_specs=pl.BlockSpec((tm, tn), lambda i, j: (i, j)),
        name="b_q_proj")

    gk_slab = head_slab(gk)[None, :]
    slab_out = pl.BlockSpec((tm, tn), lambda i, j: (i, j))
    k, v_ext = _norm_matmul(
        lat, 1, B_KV_RANK, g_kvlat, w_kv_up, slot, 0, n_slab // tn, tm=tm, tn=tn,
        extra=(gk_slab, lat) + tuple(tables),
        extra_specs=[pl.BlockSpec((1, B_HEAD_PAD), lambda i, j: (0, 0)),
                     pl.BlockSpec((tm, LANES), lambda i, j: (i, n_lat // LANES)),
                     tab, tab],
        epilogue=_b_kv_epilogue,
        out_shape=[jax.ShapeDtypeStruct((SEQ, n_slab), BF16)] * 2,
        out_specs=[slab_out, slab_out],
        name="b_kv_proj")
    o, cast_weights = _dense_attention(q, k, v_ext, casts)
    return _out_projection(o, w_o, slot, h, "b_out_proj"), cast_weights


def _dilated_kernel(q0_ref, q1_ref, q2_ref, kp_ref, kc_ref, kn_ref, vp_ref, vc_ref, vn_ref,
                    band_ref, o_ref, *scratch):
    i = pl.program_id(0)
    blk = q0_ref.shape[0]
    band = band_ref[...]
    win = band.shape[1]
    ones_blk = jnp.ones((win, LANES), BF16)
    col = lax.broadcasted_iota(jnp.int32, (1, win), 1)
    n4, n16 = blk // HOP, blk // (HOP * HOP)
    scratch = list(scratch)
    take = lambda k: [scratch.pop(0) for _ in range(k)]
    o_s, l_s = take(C_GROUPS), take(C_GROUPS)
    hop1 = take(8)
    hop2 = take(7)
    o2_hop1, l2_hop1, merged_tok = take(3)

    for src, dst in zip((q1_ref, q2_ref, kp_ref, kc_ref, kn_ref, vp_ref, vc_ref, vn_ref), hop1):
        for r in range(HOP):
            dst[r * n4:(r + 1) * n4, :] = src[pl.ds(r, n4, stride=HOP), :]
    for src, dst in zip(hop1[1:], hop2):
        for r in range(HOP * HOP):
            dst[r * n16:(r + 1) * n16, :] = src[pl.ds((r % HOP) * n4 + r // HOP, n16, stride=HOP), :]
    operands = ((q0_ref, kp_ref, kc_ref, kn_ref, vp_ref, vc_ref, vn_ref),
                (hop1[0],) + tuple(hop1[2:]),
                tuple(hop2))

    for g, ((window, d), refs) in enumerate(zip(C_PATTERNS, operands)):
        q_ref, kprev, kcur, knext, vprev, vcur, vnext = refs
        n = blk // d
        rq = min(n, win - 2 * C_HALF)
        pad_rows = win - min(n + 2 * C_HALF, win)
        tiles = []
        for r in range(d):
            def chain_window(prev_ref, cur_ref, next_ref):
                parts = [prev_ref[(r + 1) * n - C_HALF:(r + 1) * n, :], cur_ref[r * n:(r + 1) * n, :],
                         next_ref[r * n:r * n + C_HALF, :]]
                if pad_rows:
                    parts.append(jnp.zeros((pad_rows, LANES), F32))
                return jnp.concatenate(parts, axis=0).astype(BF16)

            k_chain = chain_window(kprev, kcur, knext)
            v_chain = chain_window(vprev, vcur, vnext)
            for sb in range(n // rq):
                kpos = i * n + sb * rq - C_HALF + col
                edge = jnp.where((kpos >= 0) & (kpos < SEQ // d), 0.0, NEG)
                dst = slice(r * n + sb * rq, r * n + (sb + 1) * rq)
                tiles.append((dst, q_ref[dst, :].astype(BF16), k_chain[sb * rq:sb * rq + win],
                              v_chain[sb * rq:sb * rq + win], band[:rq] + edge))
        full = win - 2 * C_HALF
        batch = 1 if rq == full else 4 * full // rq
        for b0 in range(0, len(tiles), batch):
            group_tiles = tiles[b0:b0 + batch]
            scores = [lax.dot_general(q, k_win, (((1,), (1,)), ((), ())),
                                      preferred_element_type=F32) + bias
                      for _, q, k_win, _, bias in group_tiles]
            probs = []
            for s in scores:
                mx = jnp.maximum(s[:, :LANES], s[:, LANES:])
                m = jnp.broadcast_to(jnp.max(mx, axis=-1, keepdims=True), mx.shape)
                probs.append((m, jnp.exp2(s - jnp.concatenate([m, m], axis=1)).astype(BF16)))
            for (dst, _, _, v_win, _), (m, p) in zip(group_tiles, probs):
                o_ext = jnp.dot(p, jnp.concatenate([v_win, ones_blk], axis=1),
                                preferred_element_type=F32)
                denom = o_ext[:, HEAD_DIM:]
                o_s[g][dst, :] = o_ext[:, :HEAD_DIM] / denom
                l_s[g][dst, :] = m + jnp.log2(denom)

    for src, dst in ((o_s[2], o2_hop1), (l_s[2], l2_hop1)):
        for r in range(HOP * HOP):
            dst[pl.ds((r % HOP) * n4 + r // HOP, n16, stride=HOP), :] = src[r * n16:(r + 1) * n16, :]
    for r in range(HOP):
        rows = slice(r * n4, (r + 1) * n4)
        tok = pl.ds(r, n4, stride=HOP)
        l0, l1, l2 = l_s[0][tok, :], l_s[1][rows, :], l2_hop1[rows, :]
        m = jnp.maximum(jnp.maximum(l0, l1), l2)
        e0, e1, e2 = jnp.exp2(l0 - m), jnp.exp2(l1 - m), jnp.exp2(l2 - m)
        merged = (e0 * o_s[0][tok, :] + e1 * o_s[1][rows, :] + e2 * o2_hop1[rows, :]) / (e0 + e1 + e2)
        merged_tok[tok, :] = merged
    o_ref[...] = merged_tok[...].astype(o_ref.dtype)


def _dilated_attention(qk, v):
    blk = 1024
    nb = SEQ // blk
    win = 2 * LANES
    rows = jnp.arange(LANES, dtype=jnp.int32)[:, None]
    cols = jnp.arange(win, dtype=jnp.int32)[None, :]
    band = jnp.where(jnp.abs(rows + C_HALF - cols) <= C_HALF, 0.0, NEG).astype(F32)

    def slab(col0, shift):
        return pl.BlockSpec((blk, HEAD_DIM),
                            lambda i, h: (jnp.clip(i + shift, 0, nb - 1), col0 + h))

    k_col = C_GROUPS * C_HEADS
    in_specs = [slab(g * C_HEADS, 0) for g in range(C_GROUPS)]
    in_specs += [slab(k_col, -1), slab(k_col, 0), slab(k_col, 1)]
    in_specs += [slab(0, -1), slab(0, 0), slab(0, 1)]
    in_specs.append(pl.BlockSpec((LANES, win), lambda i, h: (0, 0)))
    return pl.pallas_call(
        _dilated_kernel,
        grid=(nb, C_HEADS),
        in_specs=in_specs,
        out_specs=pl.BlockSpec((blk, HEAD_DIM), lambda i, h: (i, h)),
        out_shape=jax.ShapeDtypeStruct((SEQ, C_HEADS * HEAD_DIM), BF16),
        scratch_shapes=[pltpu.VMEM((blk, LANES), F32)] * (2 * C_GROUPS + 8 + 7 + 3),
        compiler_params=_params("parallel", "parallel"),
        name="c_dilated_attention",
    )(qk, qk, qk, qk, qk, qk, v, v, v, band)


def _mixer_c(h, g_mix, w_in, slot, gq, gk, w_o, tables):
    n_qh = C_GROUPS * C_HEADS
    nq = n_qh * HEAD_DIM
    nkv = C_HEADS * HEAD_DIM
    scale = LOG2_E / math.sqrt(HEAD_DIM)
    head_gain = jnp.concatenate([jnp.tile(gq * scale, n_qh), jnp.tile(gk, C_HEADS)])[None, :]
    qk, v = _qkv_projection(h, g_mix, w_in, slot, nq + nkv, nkv, head_gain, tables, "c", dtype=F32)
    o = _dilated_attention(qk, v)
    return _out_projection(o, w_o, slot, h, "c_out_proj")


def kernel(x, p, positions, g_mix, g_ffn, g_ple, w_ple_gate, w_ple_proj,
           w_ffn_gate, w_ffn_up, w_ffn_down,
           a_w_in, a_q_norm, a_k_norm, a_sink, a_w_o,
           b_w_in, b_q_lat_norm, b_kv_lat_norm, b_w_q_up, b_w_kv_up, b_q_norm, b_k_norm, b_w_o,
           c_w_in, c_q_norm, c_k_norm, c_w_o):
    h = x.reshape(SEQ, D_MODEL)
    p = p.reshape(DEPTH, SEQ, PLE_DIM)
    pos_col = positions.reshape(SEQ, 1)
    tables_ac = _tables_adjacent(pos_col, PARTIAL_ROT)
    tables_b = _tables_split(pos_col, B_ROPE)
    (w_ple_proj, b_w_in, b_w_q_up, b_w_kv_up, b_w_o) = [
        w.astype(BF16) for w in (w_ple_proj, b_w_in, b_w_q_up, b_w_kv_up, b_w_o)]
    late = {"ffn_gate": (w_ffn_gate, 1, DEPTH - 1, 64), "ffn_up": (w_ffn_up, 1, DEPTH - 1, 64),
            "ffn_down": (w_ffn_down, 1, DEPTH - 1, 176), "ple_gate": (w_ple_gate, 1, DEPTH - 1, 64),
            "a_in": (a_w_in, 1, 1, 16), "a_o": (a_w_o, 1, 1, 16),
            "c_in": (c_w_in, 0, 1, 16), "c_o": (c_w_o, 0, 1, 16)}
    early = {name: _cast_layers(w, first, w.shape[1] // 8)
             for name, (w, first, _, _) in late.items() if first > 0}
    cast = {}

    def weight(name, layer):
        first = late[name][1]
        if layer < first:
            return early[name], layer
        return cast[name], layer - first

    for i in range(DEPTH):
        kind, slot = i % 3, i // 3
        gm = g_mix[i][None, :]
        if kind == 0:
            (w_in, l), (w_o, _) = weight("a_in", slot), weight("a_o", slot)
            h = _mixer_a(h, gm, w_in, l, a_q_norm[slot], a_k_norm[slot], a_sink[slot], w_o,
                         tables_ac)
        elif kind == 1:
            h, cast_weights = _mixer_b(
                h, gm, b_w_in, b_q_lat_norm[slot][None, :], b_kv_lat_norm[slot][None, :],
                b_w_q_up, b_w_kv_up, slot, b_q_norm[slot], b_k_norm[slot], b_w_o, tables_b,
                list(late.values()))
            cast = dict(zip(late, cast_weights))
        else:
            h = _mixer_c(h, gm, *weight("c_in", slot), c_q_norm[slot], c_k_norm[slot],
                         weight("c_o", slot)[0], tables_ac)
        (wg, l), (wu, _), (wd, _) = weight("ffn_gate", i), weight("ffn_up", i), weight("ffn_down", i)
        h = _ffn(h, g_ffn[i][None, :], wg, wu, wd, l)
        wpg, l = weight("ple_gate", i)
        h = _ple(h, g_ple[i][None, :], wpg, l, p, w_ple_proj, i)
    return h.reshape(1, SEQ, D_MODEL)
```

```python
import functools
import math

import jax
import jax.numpy as jnp
from jax import lax
from jax.experimental import pallas as pl
from jax.experimental.pallas import tpu as pltpu

F32 = jnp.float32
BF16 = jnp.bfloat16

SEQ = 8192
D_MODEL = 2048
DEPTH = 4
HEAD_DIM = 128
ROPE_THETA = 500000.0
PARTIAL_ROT = HEAD_DIM // 4
NORM_EPS = 1e-6
NEG = -1e30
LOG2_E = math.log2(math.e)
LANES = 128
HALF_LANES = LANES // 2

A_HEADS = 16
A_KV_HEADS = 4
A_HALF_WINDOW = 128
A_BLOCK = 512
A_TILE = 128
B_HEADS = 16
B_Q_RANK = 512
B_KV_RANK = 512
B_NOPE = 128
B_ROPE = 64
B_QK = B_NOPE + B_ROPE
B_HEAD_PAD = 256
B_IN_PAD = B_Q_RANK + B_KV_RANK + LANES
C_PATTERNS = ((128, 1), (512, 4), (2048, 16))
C_GROUPS = 3
C_HEADS = 16
C_HALF = 64
assert all(window // 2 // dil == C_HALF for window, dil in C_PATTERNS)
HOP = 4
assert tuple(dil for _, dil in C_PATTERNS) == (1, HOP, HOP * HOP)
D_FF = 5632
PLE_DIM = 256

VMEM_LIMIT = 56 * 1024 * 1024


def _params(*sem):
    return pltpu.CompilerParams(dimension_semantics=sem, vmem_limit_bytes=VMEM_LIMIT)


def _rms_scale(x, width):
    ss = jnp.sum(x * x, axis=-1, keepdims=True)
    return x * lax.rsqrt(ss * (1.0 / width) + NORM_EPS)


def _rope_adjacent(y, cos, sin_lo, sin_hi, half):
    return (y * cos + pltpu.roll(y, LANES - half, 1) * sin_lo
            + pltpu.roll(y, half, 1) * sin_hi)


def _rope_split(y, cos, sin):
    return y * cos + pltpu.roll(y, HALF_LANES, 1) * sin


def _rotary_slab(x, half):
    pad = jnp.zeros(x.shape[:-1] + (HALF_LANES - half,), x.dtype)
    return jnp.concatenate([x[..., :half], pad, x[..., half:], pad], axis=-1)


def _cast_kernel(w_ref, o_ref):
    o_ref[...] = w_ref[...].astype(o_ref.dtype)


def _cast_layers(w, n_layers, rows):
    _, kdim, ndim = w.shape
    assert kdim % rows == 0
    spec = pl.BlockSpec((None, rows, ndim), lambda l, r: (l, r, 0))
    return pl.pallas_call(
        _cast_kernel,
        grid=(n_layers, kdim // rows),
        in_specs=[spec],
        out_specs=spec,
        out_shape=jax.ShapeDtypeStruct((n_layers, kdim, ndim), BF16),
        compiler_params=_params("parallel", "parallel"),
        name="cast_early_weights",
    )(w)


def _rope_table_kernel(pos_ref, inv_ref, sign_ref, cos_ref, *sin_refs):
    ang = pos_ref[...].astype(F32) * inv_ref[...]
    cos_ref[...] = jnp.cos(ang)
    s = jnp.sin(ang)
    for r, sin_ref in enumerate(sin_refs):
        sin_ref[...] = s * sign_ref[r:r + 1, :]


def _rope_tables(pos_col, inv_lanes, signs):
    tm = 1024
    n = signs.shape[0]
    tab = pl.BlockSpec((tm, LANES), lambda i: (i, 0))
    return pl.pallas_call(
        _rope_table_kernel,
        grid=(SEQ // tm,),
        in_specs=[pl.BlockSpec((tm, 1), lambda i: (i, 0)),
                  pl.BlockSpec((1, LANES), lambda i: (0, 0)),
                  pl.BlockSpec((n, LANES), lambda i: (0, 0))],
        out_specs=[tab] * (n + 1),
        out_shape=[jax.ShapeDtypeStruct((SEQ, LANES), F32)] * (n + 1),
        compiler_params=_params("parallel"),
        name="rope_tables",
    )(pos_col, inv_lanes, signs)


def _inv_freq(rot_dim):
    half = rot_dim // 2
    return ROPE_THETA ** (-jnp.arange(half, dtype=F32) * 2.0 / rot_dim)


def _tables_adjacent(pos_col, rot_dim):
    half = rot_dim // 2
    inv = _inv_freq(rot_dim)
    rest = jnp.zeros((LANES - rot_dim,), F32)
    zero, one = jnp.zeros((half,), F32), jnp.ones((half,), F32)
    inv_l = jnp.concatenate([inv, inv, rest])[None, :]
    signs = jnp.stack([jnp.concatenate([-one, zero, rest]), jnp.concatenate([zero, one, rest])])
    return _rope_tables(pos_col, inv_l, signs)


def _tables_split(pos_col, rot_dim):
    half = rot_dim // 2
    inv = _inv_freq(rot_dim)
    one = jnp.ones((half,), F32)
    inv_l = _rotary_slab(jnp.concatenate([inv, inv]), half)[None, :]
    signs = _rotary_slab(jnp.concatenate([-one, one]), half)[None, :]
    return _rope_tables(pos_col, inv_l, signs)


def _norm_matmul_kernel(x_ref, g_ref, w_ref, *rest, n_extra, epilogue, n_sub, n_tiles,
                        tail_epilogue):
    extra = rest[:n_extra]
    outs = rest[n_extra:-1]
    xn_ref = rest[-1]
    j = pl.program_id(1)

    sub = xn_ref.shape[0] // n_sub

    def column_tile(first, finish):
        for c in range(n_sub):
            rows = slice(c * sub, (c + 1) * sub)
            if first:
                x = x_ref[rows, :]
                xn_ref[rows, :] = (_rms_scale(x, x.shape[-1]) * g_ref[...]).astype(BF16)
            acc = jnp.dot(xn_ref[rows, :], w_ref[...], preferred_element_type=F32)
            finish(acc, rows, extra, outs)

    @pl.when(j == 0)
    def _():
        column_tile(True, epilogue)

    @pl.when((j > 0) & (j < n_tiles))
    def _():
        column_tile(False, epilogue)

    if tail_epilogue is not None:
        @pl.when(j >= n_tiles)
        def _():
            column_tile(False, tail_epilogue)


def _norm_matmul(x, x_col, kin, gain, w, w_slot, w_col0, n_tiles, *, tm, tn, extra, extra_specs,
                 epilogue, out_shape, out_specs, name, n_sub=4, n_tail_tiles=0, tail_epilogue=None):
    m = x.shape[0]
    kernel = functools.partial(_norm_matmul_kernel, n_extra=len(extra), epilogue=epilogue,
                               n_sub=n_sub, n_tiles=n_tiles, tail_epilogue=tail_epilogue)
    n_tiles = n_tiles + n_tail_tiles
    return pl.pallas_call(
        kernel,
        grid=(m // tm, n_tiles),
        in_specs=[pl.BlockSpec((tm, kin), lambda i, j: (i, x_col)),
                  pl.BlockSpec((1, kin), lambda i, j: (0, 0)),
                  pl.BlockSpec((None, kin, tn), lambda i, j: (w_slot, 0, w_col0 + j))]
        + list(extra_specs),
        out_specs=out_specs,
        out_shape=out_shape,
        scratch_shapes=[pltpu.VMEM((tm, kin), BF16)],
        compiler_params=_params("parallel", "arbitrary"),
        name=name,
    )(x, gain, w, *extra)


def _plain_epilogue(acc, rows, extra, outs):
    outs[0][rows, :] = acc.astype(outs[0].dtype)


def _plain_tail_epilogue(acc, rows, extra, outs):
    outs[1][rows, :] = acc.astype(outs[1].dtype)


def _head_norm_rope_epilogue(acc, rows, extra, outs):
    gain_ref, cos_ref, slo_ref, shi_ref = extra
    o_ref = outs[0]
    cos, slo, shi = cos_ref[rows, :], slo_ref[rows, :], shi_ref[rows, :]
    for c in range(acc.shape[1] // HEAD_DIM):
        cols = slice(c * HEAD_DIM, (c + 1) * HEAD_DIM)
        y = _rms_scale(acc[:, cols], HEAD_DIM) * gain_ref[:, cols]
        o_ref[rows, cols] = _rope_adjacent(y, cos, slo, shi, PARTIAL_ROT // 2).astype(o_ref.dtype)


def _qkv_projection(h, g_mix, w, slot, n_qk, n_v, head_gain, tables, name, dtype=BF16):
    tm = 1024
    tn = next(t for t in (1280, 1024, 512) if n_qk % t == 0)
    tn_v = next(t for t in (1024, 512) if n_v % t == 0 and n_qk % t == 0)
    tab = pl.BlockSpec((tm, LANES), lambda i, j: (i, 0))
    if tn == tn_v:
        last_qk = n_qk // tn - 1
        return _norm_matmul(
            h, 0, D_MODEL, g_mix, w, slot, 0, n_qk // tn, tm=tm, tn=tn,
            extra=(head_gain,) + tuple(tables),
            extra_specs=[pl.BlockSpec((1, tn), lambda i, j: (0, jnp.minimum(j, last_qk))),
                         tab, tab, tab],
            epilogue=_head_norm_rope_epilogue,
            n_tail_tiles=n_v // tn, tail_epilogue=_plain_tail_epilogue,
            out_shape=[jax.ShapeDtypeStruct((SEQ, n_qk), dtype),
                       jax.ShapeDtypeStruct((SEQ, n_v), dtype)],
            out_specs=[pl.BlockSpec((tm, tn), lambda i, j: (i, jnp.minimum(j, last_qk))),
                       pl.BlockSpec((tm, tn), lambda i, j: (i, jnp.maximum(j - last_qk - 1, 0)))],
            name=name + "_qkv_proj")
    qk = _norm_matmul(
        h, 0, D_MODEL, g_mix, w, slot, 0, n_qk // tn, tm=tm, tn=tn,
        extra=(head_gain,) + tuple(tables),
        extra_specs=[pl.BlockSpec((1, tn), lambda i, j: (0, j)), tab, tab, tab],
        epilogue=_head_norm_rope_epilogue,
        out_shape=jax.ShapeDtypeStruct((SEQ, n_qk), dtype),
        out_specs=pl.BlockSpec((tm, tn), lambda i, j: (i, j)),
        name=name + "_qk_proj")
    v = _norm_matmul(
        h, 0, D_MODEL, g_mix, w, slot, n_qk // tn_v, n_v // tn_v, tm=tm, tn=tn_v,
        extra=(), extra_specs=[], epilogue=_plain_epilogue,
        out_shape=jax.ShapeDtypeStruct((SEQ, n_v), dtype),
        out_specs=pl.BlockSpec((tm, tn_v), lambda i, j: (i, j)),
        name=name + "_v_proj")
    return qk, v


def _b_q_epilogue(acc, rows, extra, outs):
    gain_ref, cos_ref, sin_ref = extra
    (o_ref,) = outs
    cos, sin = cos_ref[rows, :], sin_ref[rows, :]
    for c in range(acc.shape[1] // B_HEAD_PAD):
        c0 = c * B_HEAD_PAD
        y = _rms_scale(acc[:, c0:c0 + B_HEAD_PAD], B_QK) * gain_ref[:, c0:c0 + B_HEAD_PAD]
        o_ref[rows, c0:c0 + B_NOPE] = y[:, :B_NOPE].astype(BF16)
        o_ref[rows, c0 + B_NOPE:c0 + B_HEAD_PAD] = _rope_split(y[:, B_NOPE:], cos, sin).astype(BF16)


def _b_kv_epilogue(acc, rows, extra, outs):
    gain_ref, krope_ref, cos_ref, sin_ref = extra
    k_ref, v_ref = outs
    cos, sin = cos_ref[rows, :], sin_ref[rows, :]
    kr = krope_ref[rows, :]
    kr_ss = jnp.sum(kr * kr, axis=-1, keepdims=True)
    g_nope = gain_ref[:, :B_NOPE]
    kr_rot = _rope_split(kr * gain_ref[:, B_NOPE:], cos, sin)
    ones_blk = jnp.ones((acc.shape[0], LANES), BF16)
    for c in range(acc.shape[1] // B_HEAD_PAD):
        c0 = c * B_HEAD_PAD
        y = acc[:, c0:c0 + B_NOPE]
        ss = jnp.sum(y * y, axis=-1, keepdims=True) + kr_ss
        rinv = lax.rsqrt(ss * (1.0 / B_QK) + NORM_EPS)
        k_ref[rows, c0:c0 + B_NOPE] = (y * rinv * g_nope).astype(BF16)
        k_ref[rows, c0 + B_NOPE:c0 + B_HEAD_PAD] = (kr_rot * rinv).astype(BF16)
        v_ref[rows, c0:c0 + B_NOPE] = acc[:, c0 + B_NOPE:c0 + B_HEAD_PAD].astype(BF16)
        v_ref[rows, c0 + B_NOPE:c0 + B_HEAD_PAD] = ones_blk


def _banded_kernel(q_ref, kp_ref, kc_ref, kn_ref, vp_ref, vc_ref, vn_ref, band_ref, sink_ref,
                   o_ref, *, hw, sq, n_kv, group):
    i = pl.program_id(0)
    blk = q_ref.shape[0]
    win = sq + 2 * hw
    n_col = win // LANES
    ones_blk = jnp.ones((win, LANES), BF16)
    col = lax.broadcasted_iota(jnp.int32, (1, win), 1)

    def halo_concat(prev_ref, cur_ref, next_ref, kv):
        cols = slice(kv * HEAD_DIM, (kv + 1) * HEAD_DIM)
        return jnp.concatenate(
            [prev_ref[blk - hw:, cols], cur_ref[:, cols], next_ref[:hw, cols]], axis=0)

    k_cat = [halo_concat(kp_ref, kc_ref, kn_ref, kv) for kv in range(n_kv)]
    v_cat = [halo_concat(vp_ref, vc_ref, vn_ref, kv) for kv in range(n_kv)]
    for sb in range(blk // sq):
        rows = slice(sb * sq, (sb + 1) * sq)
        kpos = i * blk + sb * sq - hw + col
        edge = jnp.where((kpos >= 0) & (kpos < SEQ), 0.0, NEG)
        bias = jnp.concatenate([band_ref[...] + edge] * group, axis=0)
        scores = []
        for kv in range(n_kv):
            q = jnp.concatenate(
                [q_ref[rows, hd * HEAD_DIM:(hd + 1) * HEAD_DIM]
                 for hd in range(kv * group, (kv + 1) * group)], axis=0)
            scores.append(lax.dot_general(q, k_cat[kv][sb * sq:sb * sq + win],
                                          (((1,), (1,)), ((), ())),
                                          preferred_element_type=F32) + bias)
        probs = []
        for kv, s in enumerate(scores):
            mx = s[:, :LANES]
            for cb in range(1, n_col):
                mx = jnp.maximum(mx, s[:, cb * LANES:(cb + 1) * LANES])
            sink = sink_ref[kv * group * sq:(kv + 1) * group * sq, :]
            m = jnp.maximum(jnp.broadcast_to(jnp.max(mx, axis=-1, keepdims=True), mx.shape), sink)
            probs.append((m, jnp.exp2(s - jnp.concatenate([m] * n_col, axis=1)).astype(BF16)))
        for kv, (m, p) in enumerate(probs):
            v_ext = jnp.concatenate([v_cat[kv][sb * sq:sb * sq + win], ones_blk], axis=1)
            o_ext = jnp.dot(p, v_ext, preferred_element_type=F32)
            sink = sink_ref[kv * group * sq:(kv + 1) * group * sq, :]
            o = o_ext[:, :HEAD_DIM] / (o_ext[:, HEAD_DIM:] + jnp.exp2(sink - m))
            for c in range(group):
                hd = kv * group + c
                o_ref[rows, hd * HEAD_DIM:(hd + 1) * HEAD_DIM] = (
                    o[c * sq:(c + 1) * sq].astype(o_ref.dtype))


def _banded_attention(qk, v, sink, *, n_q, n_kv, hw, name):
    blk, sq = A_BLOCK, A_TILE
    nb = SEQ // blk
    group = n_q // n_kv
    qw, kw = n_q * HEAD_DIM, n_kv * HEAD_DIM
    win = sq + 2 * hw
    rows = jnp.arange(sq, dtype=jnp.int32)[:, None]
    cols = jnp.arange(win, dtype=jnp.int32)[None, :]
    band = jnp.where(jnp.abs(rows + hw - cols) <= hw, 0.0, NEG).astype(F32)
    sink_rep = jnp.broadcast_to(jnp.repeat(sink, sq)[:, None], (n_q * sq, LANES))

    def kv_spec(col, shift):
        return pl.BlockSpec((blk, kw), lambda i: (jnp.clip(i + shift, 0, nb - 1), col))

    k_col = qw // kw
    return pl.pallas_call(
        functools.partial(_banded_kernel, hw=hw, sq=sq, n_kv=n_kv, group=group),
        grid=(nb,),
        in_specs=[pl.BlockSpec((blk, qw), lambda i: (i, 0)),
                  kv_spec(k_col, -1), kv_spec(k_col, 0), kv_spec(k_col, 1),
                  kv_spec(0, -1), kv_spec(0, 0), kv_spec(0, 1),
                  pl.BlockSpec((sq, win), lambda i: (0, 0)),
                  pl.BlockSpec((n_q * sq, LANES), lambda i: (0, 0))],
        out_specs=pl.BlockSpec((blk, qw), lambda i: (i, 0)),
        out_shape=jax.ShapeDtypeStruct((SEQ, qw), BF16),
        compiler_params=_params("parallel"),
        name=name,
    )(qk, qk, qk, qk, v, v, v, band, sink_rep)


def _flash_kernel(*refs, tk, n_cast):
    q_ref, k_ref, v_ref = refs[:3]
    w_refs = refs[3:3 + n_cast]
    o_ref = refs[3 + n_cast]
    wb_refs = refs[4 + n_cast:4 + 2 * n_cast]
    m_ref, acc_ref = refs[4 + 2 * n_cast:]
    tq = q_ref.shape[0]
    n_chunks = k_ref.shape[0] // tk
    n_col = tk // LANES
    m_ref[...] = jnp.full((tq, LANES), NEG, F32)
    acc_ref[...] = jnp.zeros((tq, B_HEAD_PAD), F32)
    q = q_ref[...]
    for w_ref, wb_ref in zip(w_refs, wb_refs):
        wb_ref[...] = w_ref[...].astype(BF16)

    for c in range(n_chunks):
        k = k_ref[c * tk:(c + 1) * tk, :]
        v = v_ref[c * tk:(c + 1) * tk, :]
        s = lax.dot_general(q, k, (((1,), (1,)), ((), ())), preferred_element_type=F32)
        mx = s[:, :LANES]
        for cb in range(1, n_col):
            mx = jnp.maximum(mx, s[:, cb * LANES:(cb + 1) * LANES])
        m_old = m_ref[...]
        m_new = jnp.maximum(m_old, jnp.max(mx, axis=-1, keepdims=True))
        alpha = jnp.exp2(m_old - m_new)
        p = jnp.exp2(s - jnp.concatenate([m_new] * n_col, axis=1)).astype(BF16)
        pv = jnp.dot(p, v, preferred_element_type=F32)
        acc_ref[...] = acc_ref[...] * jnp.concatenate([alpha, alpha], axis=1) + pv
        m_ref[...] = m_new

    acc = acc_ref[...]
    o_ref[...] = (acc[:, :B_NOPE] / acc[:, B_NOPE:]).astype(o_ref.dtype)


def _dense_attention(q, k, v_ext, casts):
    tq, tk = 1024, 256
    nq = SEQ // tq
    kv_spec = pl.BlockSpec((SEQ, B_HEAD_PAD), lambda h, i: (0, h))
    w_specs, wb_specs, wb_shapes = [], [], []
    for w, first, n_layers, rows in casts:
        _, kdim, ndim = w.shape
        per_layer = kdim // rows
        last = n_layers * per_layer - 1
        assert last < B_HEADS * nq and kdim % rows == 0

        def slab(h, i, per_layer=per_layer, last=last):
            s = jnp.minimum(h * nq + i, last)
            return s // per_layer, s % per_layer

        w_specs.append(pl.BlockSpec(
            (None, rows, ndim),
            lambda h, i, slab=slab, first=first: (first + slab(h, i)[0], slab(h, i)[1], 0)))
        wb_specs.append(pl.BlockSpec(
            (None, rows, ndim), lambda h, i, slab=slab: (slab(h, i)[0], slab(h, i)[1], 0)))
        wb_shapes.append(jax.ShapeDtypeStruct((n_layers, kdim, ndim), BF16))
    outs = pl.pallas_call(
        functools.partial(_flash_kernel, tk=tk, n_cast=len(casts)),
        grid=(B_HEADS, nq),
        in_specs=[pl.BlockSpec((tq, B_HEAD_PAD), lambda h, i: (i, h)), kv_spec, kv_spec] + w_specs,
        out_specs=[pl.BlockSpec((tq, B_NOPE), lambda h, i: (i, h))] + wb_specs,
        out_shape=[jax.ShapeDtypeStruct((SEQ, B_HEADS * B_NOPE), BF16)] + wb_shapes,
        scratch_shapes=[pltpu.VMEM((tq, LANES), F32), pltpu.VMEM((tq, B_HEAD_PAD), F32)],
        compiler_params=_params("arbitrary", "arbitrary"),
        name="b_flash_attention",
    )(q, k, v_ext, *[c[0] for c in casts])
    return outs[0], outs[1:]


def _out_proj_kernel(o_ref, w_ref, h_ref, out_ref, *, n_sub):
    sub = o_ref.shape[0] // n_sub
    for c in range(n_sub):
        rows = slice(c * sub, (c + 1) * sub)
        out_ref[rows, :] = h_ref[rows, :] + jnp.dot(o_ref[rows, :], w_ref[...],
                                                    preferred_element_type=F32)


def _out_projection(o, w, slot, h, name):
    tm = 1024
    row = pl.BlockSpec((tm, D_MODEL), lambda i: (i, 0))
    return pl.pallas_call(
        functools.partial(_out_proj_kernel, n_sub=4),
        grid=(SEQ // tm,),
        in_specs=[row,
                  pl.BlockSpec((None, D_MODEL, D_MODEL), lambda i: (slot, 0, 0),
                               pipeline_mode=pl.Buffered(1)),
                  row],
        out_specs=row,
        out_shape=jax.ShapeDtypeStruct((SEQ, D_MODEL), F32),
        compiler_params=_params("parallel"),
        name=name,
    )(o, w, h)


def _ffn_kernel(x_ref, g_ref, wg_ref, wu_ref, wd_ref, o_ref, xn_ref):
    @pl.when(pl.program_id(1) == 0)
    def _():
        x = x_ref[...]
        xn_ref[...] = (_rms_scale(x, D_MODEL) * g_ref[...]).astype(BF16)
        o_ref[...] = x

    xn = xn_ref[...]
    gate = jnp.dot(xn, wg_ref[...], preferred_element_type=F32)
    up = jnp.dot(xn, wu_ref[...], preferred_element_type=F32)
    act = (gate * jax.nn.sigmoid(gate) * up).astype(BF16)
    o_ref[...] += jnp.dot(act, wd_ref[...], preferred_element_type=F32)


def _ffn(h, g, wg, wu, wd, layer):
    tm, tf = 1024, 512
    return pl.pallas_call(
        _ffn_kernel,
        grid=(SEQ // tm, D_FF // tf),
        in_specs=[pl.BlockSpec((tm, D_MODEL), lambda i, f: (i, 0)),
                  pl.BlockSpec((1, D_MODEL), lambda i, f: (0, 0)),
                  pl.BlockSpec((None, D_MODEL, tf), lambda i, f: (layer, 0, f)),
                  pl.BlockSpec((None, D_MODEL, tf), lambda i, f: (layer, 0, f)),
                  pl.BlockSpec((None, tf, D_MODEL), lambda i, f: (layer, f, 0))],
        out_specs=pl.BlockSpec((tm, D_MODEL), lambda i, f: (i, 0)),
        out_shape=jax.ShapeDtypeStruct((SEQ, D_MODEL), F32),
        scratch_shapes=[pltpu.VMEM((tm, D_MODEL), BF16)],
        compiler_params=_params("parallel", "arbitrary"),
        name="ffn_swiglu",
    )(h, g, wg, wu, wd)


def _ple_kernel(x_ref, g_ref, wg_ref, p_ref, wp_ref, o_ref, *, n_sub):
    sub = x_ref.shape[0] // n_sub
    for c in range(n_sub):
        rows = slice(c * sub, (c + 1) * sub)
        x = x_ref[rows, :]
        xn = (_rms_scale(x, D_MODEL) * g_ref[...]).astype(BF16)
        gate = jnp.dot(xn, wg_ref[...], preferred_element_type=F32)
        proj = jnp.dot(p_ref[rows, :].astype(BF16), wp_ref[...], preferred_element_type=F32)
        o_ref[rows, :] = x + jax.nn.sigmoid(gate) * proj


def _ple(h, g, w_gate, gate_slot, p, w_proj, layer):
    tm = 1024
    resident = pl.Buffered(1)
    return pl.pallas_call(
        functools.partial(_ple_kernel, n_sub=4),
        grid=(SEQ // tm,),
        in_specs=[pl.BlockSpec((tm, D_MODEL), lambda i: (i, 0)),
                  pl.BlockSpec((1, D_MODEL), lambda i: (0, 0)),
                  pl.BlockSpec((None, D_MODEL, D_MODEL), lambda i: (gate_slot, 0, 0),
                               pipeline_mode=resident),
                  pl.BlockSpec((None, tm, PLE_DIM), lambda i: (layer, i, 0)),
                  pl.BlockSpec((None, PLE_DIM, D_MODEL), lambda i: (layer, 0, 0),
                               pipeline_mode=resident)],
        out_specs=pl.BlockSpec((tm, D_MODEL), lambda i: (i, 0)),
        out_shape=jax.ShapeDtypeStruct((SEQ, D_MODEL), F32),
        compiler_params=_params("parallel"),
        name="ple_gate",
    )(h, g, w_gate, p, w_proj)


def _mixer_a(h, g_mix, w_in, slot, gq, gk, sink, w_o, tables):
    nq, nk = A_HEADS * HEAD_DIM, A_KV_HEADS * HEAD_DIM
    scale = LOG2_E / math.sqrt(HEAD_DIM)
    head_gain = jnp.concatenate([jnp.tile(gq * scale, A_HEADS), jnp.tile(gk, A_KV_HEADS)])[None, :]
    qk, v = _qkv_projection(h, g_mix, w_in, slot, nq + nk, nk, head_gain, tables, "a")
    o = _banded_attention(qk, v, sink * LOG2_E, n_q=A_HEADS, n_kv=A_KV_HEADS, hw=A_HALF_WINDOW,
                          name="a_banded_attention")
    return _out_projection(o, w_o, slot, h, "a_out_proj")


def _mixer_b(h, g_mix, w_in, g_qlat, g_kvlat, w_q_up, w_kv_up, slot, gq, gk, w_o, tables, casts):
    scale = LOG2_E / math.sqrt(B_QK)
    half = B_ROPE // 2
    n_lat = B_Q_RANK + B_KV_RANK
    w_in = w_in[slot]
    w_in_pad = jnp.concatenate([w_in[:, :n_lat], _rotary_slab(w_in[:, n_lat:], half)], axis=1)[None]
    tm = 1024
    lat = _norm_matmul(
        h, 0, D_MODEL, g_mix, w_in_pad, 0, 0, 1, tm=tm, tn=B_IN_PAD, extra=(), extra_specs=[],
        epilogue=_plain_epilogue,
        out_shape=jax.ShapeDtypeStruct((SEQ, B_IN_PAD), F32),
        out_specs=pl.BlockSpec((tm, B_IN_PAD), lambda i, j: (i, j)),
        name="b_latent_proj")

    tab = pl.BlockSpec((tm, LANES), lambda i, j: (i, 0))
    tn = 2048
    n_slab = B_HEADS * B_HEAD_PAD

    def head_slab(x):
        return jnp.concatenate([x[..., :B_NOPE], _rotary_slab(x[..., B_NOPE:], half)], axis=-1)

    wq = head_slab(w_q_up[slot].reshape(B_Q_RANK, B_HEADS, B_QK)).reshape(B_Q_RANK, n_slab)[None]
    gq_slab = jnp.tile(head_slab(gq * scale), B_HEADS)[None, :]
    q = _norm_matmul(
        lat, 0, B_Q_RANK, g_qlat, wq, 0, 0, n_slab // tn, tm=tm, tn=tn,
        extra=(gq_slab,) + tuple(tables),
        extra_specs=[pl.BlockSpec((1, tn), lambda i, j: (0, j)), tab, tab],
        epilogue=_b_q_epilogue,
        out_shape=jax.ShapeDtypeStruct((SEQ, n_slab), BF16),
        out_specs=pl.BlockSpec((tm, tn), lambda i, j: (i, j)),
        name="b_q_proj")

    gk_slab = head_slab(gk)[None, :]
    slab_out = pl.BlockSpec((tm, tn), lambda i, j: (i, j))
    k, v_ext = _norm_matmul(
        lat, 1, B_KV_RANK, g_kvlat, w_kv_up, slot, 0, n_slab // tn, tm=tm, tn=tn,
        extra=(gk_slab, lat) + tuple(tables),
        extra_specs=[pl.BlockSpec((1, B_HEAD_PAD), lambda i, j: (0, 0)),
                     pl.BlockSpec((tm, LANES), lambda i, j: (i, n_lat // LANES)),
                     tab, tab],
        epilogue=_b_kv_epilogue,
        out_shape=[jax.ShapeDtypeStruct((SEQ, n_slab), BF16)] * 2,
        out_specs=[slab_out, slab_out],
        name="b_kv_proj")
    o, cast_weights = _dense_attention(q, k, v_ext, casts)
    return _out_projection(o, w_o, slot, h, "b_out_proj"), cast_weights


def _dilated_kernel(q0_ref, q1_ref, q2_ref, kp_ref, kc_ref, kn_ref, vp_ref, vc_ref, vn_ref,
                    band_ref, o_ref, *scratch):
    i = pl.program_id(0)
    blk = q0_ref.shape[0]
    band = band_ref[...]
    win = band.shape[1]
    ones_blk = jnp.ones((win, LANES), BF16)
    col = lax.broadcasted_iota(jnp.int32, (1, win), 1)
    n4, n16 = blk // HOP, blk // (HOP * HOP)
    scratch = list(scratch)
    take = lambda k: [scratch.pop(0) for _ in range(k)]
    o_s, l_s = take(C_GROUPS), take(C_GROUPS)
    hop1 = take(8)
    hop2 = take(7)
    o2_hop1, l2_hop1, merged_tok = take(3)

    for src, dst in zip((q1_ref, q2_ref, kp_ref, kc_ref, kn_ref, vp_ref, vc_ref, vn_ref), hop1):
        for r in range(HOP):
            dst[r * n4:(r + 1) * n4, :] = src[pl.ds(r, n4, stride=HOP), :]
    for src, dst in zip(hop1[1:], hop2):
        for r in range(HOP * HOP):
            dst[r * n16:(r + 1) * n16, :] = src[pl.ds((r % HOP) * n4 + r // HOP, n16, stride=HOP), :]
    operands = ((q0_ref, kp_ref, kc_ref, kn_ref, vp_ref, vc_ref, vn_ref),
                (hop1[0],) + tuple(hop1[2:]),
                tuple(hop2))

    for g, ((window, d), refs) in enumerate(zip(C_PATTERNS, operands)):
        q_ref, kprev, kcur, knext, vprev, vcur, vnext = refs
        n = blk // d
        rq = min(n, win - 2 * C_HALF)
        pad_rows = win - min(n + 2 * C_HALF, win)
        tiles = []
        for r in range(d):
            def chain_window(prev_ref, cur_ref, next_ref):
                parts = [prev_ref[(r + 1) * n - C_HALF:(r + 1) * n, :], cur_ref[r * n:(r + 1) * n, :],
                         next_ref[r * n:r * n + C_HALF, :]]
                if pad_rows:
                    parts.append(jnp.zeros((pad_rows, LANES), F32))
                return jnp.concatenate(parts, axis=0).astype(BF16)

            k_chain = chain_window(kprev, kcur, knext)
            v_chain = chain_window(vprev, vcur, vnext)
            for sb in range(n // rq):
                kpos = i * n + sb * rq - C_HALF + col
                edge = jnp.where((kpos >= 0) & (kpos < SEQ // d), 0.0, NEG)
                dst = slice(r * n + sb * rq, r * n + (sb + 1) * rq)
                tiles.append((dst, q_ref[dst, :].astype(BF16), k_chain[sb * rq:sb * rq + win],
                              v_chain[sb * rq:sb * rq + win], band[:rq] + edge))
        full = win - 2 * C_HALF
        batch = 1 if rq == full else 4 * full // rq
        for b0 in range(0, len(tiles), batch):
            group_tiles = tiles[b0:b0 + batch]
            scores = [lax.dot_general(q, k_win, (((1,), (1,)), ((), ())),
                                      preferred_element_type=F32) + bias
                      for _, q, k_win, _, bias in group_tiles]
            probs = []
            for s in scores:
                mx = jnp.maximum(s[:, :LANES], s[:, LANES:])
                m = jnp.broadcast_to(jnp.max(mx, axis=-1, keepdims=True), mx.shape)
                probs.append((m, jnp.exp2(s - jnp.concatenate([m, m], axis=1)).astype(BF16)))
            for (dst, _, _, v_win, _), (m, p) in zip(group_tiles, probs):
                o_ext = jnp.dot(p, jnp.concatenate([v_win, ones_blk], axis=1),
                                preferred_element_type=F32)
                denom = o_ext[:, HEAD_DIM:]
                o_s[g][dst, :] = o_ext[:, :HEAD_DIM] / denom
                l_s[g][dst, :] = m + jnp.log2(denom)

    for src, dst in ((o_s[2], o2_hop1), (l_s[2], l2_hop1)):
        for r in range(HOP * HOP):
            dst[pl.ds((r % HOP) * n4 + r // HOP, n16, stride=HOP), :] = src[r * n16:(r + 1) * n16, :]
    for r in range(HOP):
        rows = slice(r * n4, (r + 1) * n4)
        tok = pl.ds(r, n4, stride=HOP)
        l0, l1, l2 = l_s[0][tok, :], l_s[1][rows, :], l2_hop1[rows, :]
        m = jnp.maximum(jnp.maximum(l0, l1), l2)
        e0, e1, e2 = jnp.exp2(l0 - m), jnp.exp2(l1 - m), jnp.exp2(l2 - m)
        merged = (e0 * o_s[0][tok, :] + e1 * o_s[1][rows, :] + e2 * o2_hop1[rows, :]) / (e0 + e1 + e2)
        merged_tok[tok, :] = merged
    o_ref[...] = merged_tok[...].astype(o_ref.dtype)


def _dilated_attention(qk, v):
    blk = 1024
    nb = SEQ // blk
    win = 2 * LANES
    rows = jnp.arange(LANES, dtype=jnp.int32)[:, None]
    cols = jnp.arange(win, dtype=jnp.int32)[None, :]
    band = jnp.where(jnp.abs(rows + C_HALF - cols) <= C_HALF, 0.0, NEG).astype(F32)

    def slab(col0, shift):
        return pl.BlockSpec((blk, HEAD_DIM),
                            lambda i, h: (jnp.clip(i + shift, 0, nb - 1), col0 + h))

    k_col = C_GROUPS * C_HEADS
    in_specs = [slab(g * C_HEADS, 0) for g in range(C_GROUPS)]
    in_specs += [slab(k_col, -1), slab(k_col, 0), slab(k_col, 1)]
    in_specs += [slab(0, -1), slab(0, 0), slab(0, 1)]
    in_specs.append(pl.BlockSpec((LANES, win), lambda i, h: (0, 0)))
    return pl.pallas_call(
        _dilated_kernel,
        grid=(nb, C_HEADS),
        in_specs=in_specs,
        out_specs=pl.BlockSpec((blk, HEAD_DIM), lambda i, h: (i, h)),
        out_shape=jax.ShapeDtypeStruct((SEQ, C_HEADS * HEAD_DIM), BF16),
        scratch_shapes=[pltpu.VMEM((blk, LANES), F32)] * (2 * C_GROUPS + 8 + 7 + 3),
        compiler_params=_params("parallel", "parallel"),
        name="c_dilated_attention",
    )(qk, qk, qk, qk, qk, qk, v, v, v, band)


def _mixer_c(h, g_mix, w_in, slot, gq, gk, w_o, tables):
    n_qh = C_GROUPS * C_HEADS
    nq = n_qh * HEAD_DIM
    nkv = C_HEADS * HEAD_DIM
    scale = LOG2_E / math.sqrt(HEAD_DIM)
    head_gain = jnp.concatenate([jnp.tile(gq * scale, n_qh), jnp.tile(gk, C_HEADS)])[None, :]
    qk, v = _qkv_projection(h, g_mix, w_in, slot, nq + nkv, nkv, head_gain, tables, "c", dtype=F32)
    o = _dilated_attention(qk, v)
    return _out_projection(o, w_o, slot, h, "c_out_proj")


def kernel(x, p, positions, g_mix, g_ffn, g_ple, w_ple_gate, w_ple_proj,
           w_ffn_gate, w_ffn_up, w_ffn_down,
           a_w_in, a_q_norm, a_k_norm, a_sink, a_w_o,
           b_w_in, b_q_lat_norm, b_kv_lat_norm, b_w_q_up, b_w_kv_up, b_q_norm, b_k_norm, b_w_o,
           c_w_in, c_q_norm, c_k_norm, c_w_o):
    h = x.reshape(SEQ, D_MODEL)
    p = p.reshape(DEPTH, SEQ, PLE_DIM)
    pos_col = positions.reshape(SEQ, 1)
    tables_ac = _tables_adjacent(pos_col, PARTIAL_ROT)
    tables_b = _tables_split(pos_col, B_ROPE)
    (w_ple_proj, b_w_in, b_w_q_up, b_w_kv_up, b_w_o) = [
        w.astype(BF16) for w in (w_ple_proj, b_w_in, b_w_q_up, b_w_kv_up, b_w_o)]
    late = {"ffn_gate": (w_ffn_gate, 1, DEPTH - 1, 64), "ffn_up": (w_ffn_up, 1, DEPTH - 1, 64),
            "ffn_down": (w_ffn_down, 1, DEPTH - 1, 176), "ple_gate": (w_ple_gate, 1, DEPTH - 1, 64),
            "a_in": (a_w_in, 1, 1, 16), "a_o": (a_w_o, 1, 1, 16),
            "c_in": (c_w_in, 0, 1, 16), "c_o": (c_w_o, 0, 1, 16)}
    early = {name: _cast_layers(w, first, w.shape[1] // 8)
             for name, (w, first, _, _) in late.items() if first > 0}
    cast = {}

    def weight(name, layer):
        first = late[name][1]
        if layer < first:
            return early[name], layer
        return cast[name], layer - first

    for i in range(DEPTH):
        kind, slot = i % 3, i // 3
        gm = g_mix[i][None, :]
        if kind == 0:
            (w_in, l), (w_o, _) = weight("a_in", slot), weight("a_o", slot)
            h = _mixer_a(h, gm, w_in, l, a_q_norm[slot], a_k_norm[slot], a_sink[slot], w_o,
                         tables_ac)
        elif kind == 1:
            h, cast_weights = _mixer_b(
                h, gm, b_w_in, b_q_lat_norm[slot][None, :], b_kv_lat_norm[slot][None, :],
                b_w_q_up, b_w_kv_up, slot, b_q_norm[slot], b_k_norm[slot], b_w_o, tables_b,
                list(late.values()))
            cast = dict(zip(late, cast_weights))
        else:
            h = _mixer_c(h, gm, *weight("c_in", slot), c_q_norm[slot], c_k_norm[slot],
                         weight("c_o", slot)[0], tables_ac)
        (wg, l), (wu, _), (wd, _) = weight("ffn_gate", i), weight("ffn_up", i), weight("ffn_down", i)
        h = _ffn(h, g_ffn[i][None, :], wg, wu, wd, l)
        wpg, l = weight("ple_gate", i)
        h = _ple(h, g_ple[i][None, :], wpg, l, p, w_ple_proj, i)
    return h.reshape(1, SEQ, D_MODEL)
```
